```python
import math, functools
import jax, jax.numpy as jnp
from jax import lax
import numpy as np

D_MODEL = 1024
BATCH = 8
SEQ = 4096
DEPTH = 2
DEC_BATCH = 128
DEC_SEQ = 8
PAST_LEN = 16384
PAGE_SIZE = 128

N_MIXERS = 2
N_MLA = (DEPTH + 1) // 2
N_FOX = DEPTH // 2
D_FF = 2816
NORM_EPS = 1e-6
Q_BLOCK = 128
MLA_HEADS = 8
MLA_NOPE = 128
MLA_ROPE = 64
MLA_V = 128
MLA_Q_LORA = 512
MLA_KV_LORA = 256
ROPE_THETA = 10000.0
MLA_SCALE = 1.0 / math.sqrt(MLA_NOPE + MLA_ROPE)
FOX_HEADS = 16
FOX_KV_HEADS = 4
FOX_GROUP = FOX_HEADS // FOX_KV_HEADS
FOX_HEAD_DIM = 64
FOX_SCALE = 1.0 / math.sqrt(FOX_HEAD_DIM)
FOX_IN = FOX_HEADS * FOX_HEAD_DIM + 2 * FOX_KV_HEADS * FOX_HEAD_DIM + FOX_HEADS

kernel_name = 'hybrid_mla_fox_macaron_adaln_step'


def rmsnorm(x, g):
    xf = x.astype(jnp.float32)
    y = xf * lax.rsqrt(jnp.mean(xf * xf, axis=-1, keepdims=True) + NORM_EPS)
    return (y * g.astype(jnp.float32)).astype(x.dtype)


def adaln_mods(c, ada_w, ada_b):
    m = jax.nn.silu(c) @ ada_w + ada_b
    return m.reshape(c.shape[0], 3, 3, D_MODEL)


def modulate(x, g, mods, j):
    return rmsnorm(x, g) * (1.0 + mods[:, j, 1][:, None, :]) + mods[:, j, 0][:, None, :]


def swiglu(h, w_gu, w_dn):
    gate, up = jnp.split(h @ w_gu, 2, axis=-1)
    return (jax.nn.silu(gate) * up) @ w_dn


def macaron_layer(x, c, mixer, ada_w, ada_b, norm_g, w_gu, w_dn):
    m = adaln_mods(c, ada_w, ada_b)
    x = x + 0.5 * m[:, 0, 2][:, None, :] * swiglu(modulate(x, norm_g[0], m, 0), w_gu[0], w_dn[0])
    mix, state = mixer(modulate(x, norm_g[1], m, 1))
    x = x + m[:, 1, 2][:, None, :] * mix
    x = x + 0.5 * m[:, 2, 2][:, None, :] * swiglu(modulate(x, norm_g[2], m, 2), w_gu[1], w_dn[1])
    return x, state


def to_blocks(a):
    b, t = a.shape[:2]
    return jnp.moveaxis(a.reshape(b, t // Q_BLOCK, Q_BLOCK, *a.shape[2:]), 1, 0)


def from_blocks(a):
    a = jnp.moveaxis(a, 0, 1)
    return a.reshape(a.shape[0], a.shape[1] * a.shape[2], *a.shape[3:])


def gather_pages(cache, page_table):
    g = cache[page_table]
    return g.reshape(g.shape[0], g.shape[1] * g.shape[2], *g.shape[3:])


def causal_mask(q_pos, k_pos):
    return k_pos[None, :] <= q_pos[:, None]


def rope(x, pos):
    half = MLA_ROPE // 2
    inv_freq = ROPE_THETA ** (-jnp.arange(half, dtype=jnp.float32) / half)
    ang = pos.astype(jnp.float32)[:, None] * inv_freq[None, :]
    cos = jnp.cos(ang)[None, :, None, :]
    sin = jnp.sin(ang)[None, :, None, :]
    xf = x.astype(jnp.float32)
    x1, x2 = xf[..., :half], xf[..., half:]
    return jnp.concatenate([x1 * cos - x2 * sin, x1 * sin + x2 * cos], axis=-1).astype(x.dtype)


def mla_project(h, pos, w_in, g_q, w_uq, g_kv, w_uk):
    b, t, _ = h.shape
    proj = h @ w_in
    c_q = rmsnorm(proj[..., :MLA_Q_LORA], g_q)
    c_kv = rmsnorm(proj[..., MLA_Q_LORA:MLA_Q_LORA + MLA_KV_LORA], g_kv)
    k_rope = rope(proj[..., MLA_Q_LORA + MLA_KV_LORA:][:, :, None, :], pos)[:, :, 0, :]
    q = (c_q @ w_uq).reshape(b, t, MLA_HEADS, MLA_NOPE + MLA_ROPE)
    q_rope = rope(q[..., MLA_NOPE:], pos)
    q_lat = jnp.einsum('bthn,rhn->bthr', q[..., :MLA_NOPE], w_uk)
    return q_lat, q_rope, c_kv, k_rope


def mla_attend(q_lat, q_rope, c_kv, k_rope, q_pos, k_pos):
    s = (jnp.einsum('bthr,bsr->bhts', q_lat, c_kv, preferred_element_type=jnp.float32)
         + jnp.einsum('bthe,bse->bhts', q_rope, k_rope, preferred_element_type=jnp.float32)) * MLA_SCALE
    s = jnp.where(causal_mask(q_pos, k_pos), s, -jnp.inf)
    p = jax.nn.softmax(s, axis=-1).astype(c_kv.dtype)
    return jnp.einsum('bhts,bsr->bthr', p, c_kv)


def mla_output(o_lat, w_uv, w_o):
    o = jnp.einsum('bthr,rhv->bthv', o_lat, w_uv)
    return o.reshape(o.shape[0], o.shape[1], MLA_HEADS * MLA_V) @ w_o


def mla_prompt(h, weights):
    w_in, g_q, w_uq, g_kv, w_uk, w_uv, w_o = weights
    pos = jnp.arange(h.shape[1], dtype=jnp.int32)
    q_lat, q_rope, c_kv, k_rope = mla_project(h, pos, w_in, g_q, w_uq, g_kv, w_uk)

    def block(args):
        ql, qr, qp = args
        return mla_attend(ql, qr, c_kv, k_rope, qp, pos)

    o_lat = from_blocks(lax.map(block, (to_blocks(q_lat), to_blocks(q_rope), pos.reshape(-1, Q_BLOCK))))
    return mla_output(o_lat, w_uv, w_o), (c_kv, k_rope)


def mla_sample(h, cache_ckv, cache_kr, page_table, weights):
    w_in, g_q, w_uq, g_kv, w_uk, w_uv, w_o = weights
    t = h.shape[1]
    q_pos = PAST_LEN + jnp.arange(t, dtype=jnp.int32)
    k_pos = jnp.arange(PAST_LEN + t, dtype=jnp.int32)
    q_lat, q_rope, c_kv, k_rope = mla_project(h, q_pos, w_in, g_q, w_uq, g_kv, w_uk)
    keys_ckv = jnp.concatenate([gather_pages(cache_ckv, page_table).astype(c_kv.dtype), c_kv], axis=1)
    keys_kr = jnp.concatenate([gather_pages(cache_kr, page_table).astype(k_rope.dtype), k_rope], axis=1)
    o_lat = mla_attend(q_lat, q_rope, keys_ckv, keys_kr, q_pos, k_pos)
    return mla_output(o_lat, w_uv, w_o), (c_kv, k_rope)


def fox_project(h, w_in, b_f):
    b, t, _ = h.shape
    qw = FOX_HEADS * FOX_HEAD_DIM
    kw = FOX_KV_HEADS * FOX_HEAD_DIM
    proj = h @ w_in
    q = proj[..., :qw].reshape(b, t, FOX_HEADS, FOX_HEAD_DIM)
    k = proj[..., qw:qw + kw].reshape(b, t, FOX_KV_HEADS, FOX_HEAD_DIM)
    v = proj[..., qw + kw:qw + 2 * kw].reshape(b, t, FOX_KV_HEADS, FOX_HEAD_DIM)
    logf = jax.nn.log_sigmoid((proj[..., qw + 2 * kw:] + b_f).astype(jnp.float32))
    return q, k, v, logf


def fox_attend(q, k, v, cum_q, cum_k, q_pos, k_pos):
    b, tq = q.shape[:2]
    qg = q.reshape(b, tq, FOX_KV_HEADS, FOX_GROUP, FOX_HEAD_DIM)
    s = jnp.einsum('btkgd,bskd->bkgts', qg, k, preferred_element_type=jnp.float32) * FOX_SCALE
    cq = jnp.transpose(cum_q.reshape(b, tq, FOX_KV_HEADS, FOX_GROUP), (0, 2, 3, 1))[..., None]
    ck = jnp.transpose(cum_k.reshape(b, -1, FOX_KV_HEADS, FOX_GROUP), (0, 2, 3, 1))[..., None, :]
    s = jnp.where(causal_mask(q_pos, k_pos), s + (cq - ck), -jnp.inf)
    p = jax.nn.softmax(s, axis=-1).astype(v.dtype)
    o = jnp.einsum('bkgts,bskd->btkgd', p, v)
    return o.reshape(b, tq, FOX_HEADS * FOX_HEAD_DIM)


def fox_prompt(h, weights):
    w_in, b_f, w_o = weights
    q, k, v, logf = fox_project(h, w_in, b_f)
    pos = jnp.arange(h.shape[1], dtype=jnp.int32)
    cum = jnp.cumsum(logf, axis=1)

    def block(args):
        qb, cb, qp = args
        return fox_attend(qb, k, v, cb, cum, qp, pos)

    o = from_blocks(lax.map(block, (to_blocks(q), to_blocks(cum), pos.reshape(-1, Q_BLOCK))))
    return o @ w_o, (k, v, logf)


def fox_sample(h, cache_k, cache_v, cache_logf, page_table, weights):
    w_in, b_f, w_o = weights
    t = h.shape[1]
    q, k, v, logf = fox_project(h, w_in, b_f)
    keys = jnp.concatenate([gather_pages(cache_k, page_table).astype(k.dtype), k], axis=1)
    vals = jnp.concatenate([gather_pages(cache_v, page_table).astype(v.dtype), v], axis=1)
    logf_all = jnp.concatenate([gather_pages(cache_logf, page_table).astype(jnp.float32), logf], axis=1)
    cum = jnp.cumsum(logf_all, axis=1)
    q_pos = PAST_LEN + jnp.arange(t, dtype=jnp.int32)
    k_pos = jnp.arange(PAST_LEN + t, dtype=jnp.int32)
    o = fox_attend(q, keys, vals, cum[:, PAST_LEN:], cum, q_pos, k_pos)
    return o @ w_o, (k, v, logf)


def setup_inputs(seed: int = 0) -> dict:
    key = jax.random.key(seed)
    ks = jax.random.split(key, 40)
    cnt = [0]

    def nrm(shape, scale=1.0):
        k = ks[cnt[0]]
        cnt[0] += 1
        return jax.random.normal(k, shape, jnp.float32) * scale

    n_pages = PAST_LEN // PAGE_SIZE
    n_used = DEC_BATCH * n_pages
    n_pool = n_used + n_used // 4
    d = D_MODEL
    inp = {}
    inp['x_prompt'] = nrm((BATCH, SEQ, d))
    inp['x_sample'] = nrm((DEC_BATCH, DEC_SEQ, d))
    inp['c_prompt'] = nrm((BATCH, d))
    inp['c_sample'] = nrm((DEC_BATCH, d))
    inp['cache_mla_ckv'] = nrm((N_MLA, n_pool, PAGE_SIZE, MLA_KV_LORA))
    inp['cache_mla_krope'] = nrm((N_MLA, n_pool, PAGE_SIZE, MLA_ROPE))
    inp['cache_fox_k'] = nrm((N_FOX, n_pool, PAGE_SIZE, FOX_KV_HEADS, FOX_HEAD_DIM))
    inp['cache_fox_v'] = nrm((N_FOX, n_pool, PAGE_SIZE, FOX_KV_HEADS, FOX_HEAD_DIM))
    inp['cache_fox_logf'] = jax.nn.log_sigmoid(2.0 + nrm((N_FOX, n_pool, PAGE_SIZE, FOX_HEADS)))
    perm = jax.random.permutation(ks[cnt[0]], n_pool)
    cnt[0] += 1
    inp['page_table'] = perm[:n_used].reshape(DEC_BATCH, n_pages).astype(jnp.int32)
    inp['ada_w'] = nrm((DEPTH, d, 9 * d), d ** -0.5)
    inp['ada_b'] = nrm((DEPTH, 9 * d), 0.02)
    inp['norm_g'] = 1.0 + nrm((DEPTH, 3, d), 0.05)
    inp['ffn_w_gu'] = nrm((DEPTH, 2, d, 2 * D_FF), d ** -0.5)
    inp['ffn_w_dn'] = nrm((DEPTH, 2, D_FF, d), D_FF ** -0.5)
    inp['final_g'] = 1.0 + nrm((d,), 0.05)
    inp['mla_w_in'] = nrm((N_MLA, d, MLA_Q_LORA + MLA_KV_LORA + MLA_ROPE), d ** -0.5)
    inp['mla_g_q'] = 1.0 + nrm((N_MLA, MLA_Q_LORA), 0.05)
    inp['mla_w_uq'] = nrm((N_MLA, MLA_Q_LORA, MLA_HEADS * (MLA_NOPE + MLA_ROPE)), MLA_Q_LORA ** -0.5)
    inp['mla_g_kv'] = 1.0 + nrm((N_MLA, MLA_KV_LORA), 0.05)
    inp['mla_w_uk'] = nrm((N_MLA, MLA_KV_LORA, MLA_HEADS, MLA_NOPE), MLA_KV_LORA ** -0.5)
    inp['mla_w_uv'] = nrm((N_MLA, MLA_KV_LORA, MLA_HEADS, MLA_V), MLA_KV_LORA ** -0.5)
    inp['mla_w_o'] = nrm((N_MLA, MLA_HEADS * MLA_V, d), (MLA_HEADS * MLA_V) ** -0.5)
    inp['fox_w_in'] = nrm((N_FOX, d, FOX_IN), d ** -0.5)
    inp['fox_b_f'] = 2.0 + nrm((N_FOX, FOX_HEADS), 0.5)
    inp['fox_w_o'] = nrm((N_FOX, FOX_HEADS * FOX_HEAD_DIM, d), (FOX_HEADS * FOX_HEAD_DIM) ** -0.5)
    return inp


def reference(x_prompt, x_sample, c_prompt, c_sample, cache_mla_ckv, cache_mla_krope, cache_fox_k,
              cache_fox_v, cache_fox_logf, page_table, ada_w, ada_b, norm_g, ffn_w_gu, ffn_w_dn, final_g,
              mla_w_in, mla_g_q, mla_w_uq, mla_g_kv, mla_w_uk, mla_w_uv, mla_w_o, fox_w_in, fox_b_f, fox_w_o):
    yp, ys = x_prompt, x_sample
    new = {n: [] for n in ('p_ckv', 'p_kr', 'p_k', 'p_v', 'p_lf', 's_ckv', 's_kr', 's_k', 's_v', 's_lf')}
    for l in range(DEPTH):
        j = l // N_MIXERS
        if l % N_MIXERS == 0:
            w = (mla_w_in[j], mla_g_q[j], mla_w_uq[j], mla_g_kv[j], mla_w_uk[j], mla_w_uv[j], mla_w_o[j])
            mix_p = functools.partial(mla_prompt, weights=w)
            mix_s = functools.partial(mla_sample, cache_ckv=cache_mla_ckv[j], cache_kr=cache_mla_krope[j],
                                      page_table=page_table, weights=w)
            names = ('ckv', 'kr')
        else:
            w = (fox_w_in[j], fox_b_f[j], fox_w_o[j])
            mix_p = functools.partial(fox_prompt, weights=w)
            mix_s = functools.partial(fox_sample, cache_k=cache_fox_k[j], cache_v=cache_fox_v[j],
                                      cache_logf=cache_fox_logf[j], page_table=page_table, weights=w)
            names = ('k', 'v', 'lf')
        lw = (ada_w[l], ada_b[l], norm_g[l], ffn_w_gu[l], ffn_w_dn[l])
        yp, st_p = macaron_layer(yp, c_prompt, mix_p, *lw)
        ys, st_s = macaron_layer(ys, c_sample, mix_s, *lw)
        for n, a, b in zip(names, st_p, st_s):
            new['p_' + n].append(a)
            new['s_' + n].append(b)
    y_prompt = rmsnorm(yp, final_g)
    y_sample = rmsnorm(ys, final_g)
    return (y_prompt, y_sample,
            jnp.stack(new['p_ckv']), jnp.stack(new['p_kr']),
            jnp.stack(new['p_k']), jnp.stack(new['p_v']), jnp.stack(new['p_lf']),
            jnp.stack(new['s_ckv']), jnp.stack(new['s_kr']),
            jnp.stack(new['s_k']), jnp.stack(new['s_v']), jnp.stack(new['s_lf']))
```

```python
import functools
import math

import numpy as np
import jax
import jax.numpy as jnp
from jax import lax
from jax.experimental import pallas as pl
from jax.experimental.pallas import tpu as pltpu

F32 = jnp.float32
BF16 = jnp.bfloat16

LANES = 128
NORM_EPS = 1e-6
PAGE_SIZE = 128
D_FF = 2816
MLA_HEADS = 8
MLA_NOPE = 128
MLA_ROPE = 64
MLA_V = 128
MLA_Q_LORA = 512
MLA_KV_LORA = 256
ROPE_THETA = 10000.0
MLA_SCALE = 1.0 / math.sqrt(MLA_NOPE + MLA_ROPE)
FOX_HEADS = 16
FOX_KV_HEADS = 4
FOX_HEAD_DIM = 64
FOX_SCALE = 1.0 / math.sqrt(FOX_HEAD_DIM)
VMEM_LIMIT_BYTES = 56 * 1024 * 1024


def _params(*sem):
    return pltpu.CompilerParams(dimension_semantics=sem, vmem_limit_bytes=VMEM_LIMIT_BYTES)


def _dot(a, b):
    return jnp.dot(a, b, preferred_element_type=F32)


def _dot_nt(a, b):
    return lax.dot_general(a, b, (((1,), (1,)), ((), ())), preferred_element_type=F32)


def _sigmoid(x):
    return 1.0 / (1.0 + jnp.exp(-x))


def _rms(x, g):
    return x * lax.rsqrt(jnp.mean(x * x, axis=-1, keepdims=True) + NORM_EPS) * g


def _split3(x):
    hi = x.astype(BF16)
    r = x - hi.astype(F32)
    mid = r.astype(BF16)
    lo = (r - mid.astype(F32)).astype(BF16)
    return hi, mid, lo


def _dot3(x, m):
    hi, mid, lo = _split3(x)
    return _dot(hi, m) + _dot(mid, m) + _dot(lo, m)


def _iota(shape, dim):
    return lax.broadcasted_iota(jnp.int32, shape, dim)


def _modulated(x_ref, sh_ref, sc_ref, g_ref):
    x = x_ref[...]
    return x, _rms(x, g_ref[...]) * (1.0 + sc_ref[...]) + sh_ref[...]


def _adaln_kernel(c_ref, w_ref, b_ref, o_ref):
    c = c_ref[...]
    a = (c * _sigmoid(c)).astype(BF16)
    o_ref[0] = _dot(a, w_ref[0].astype(BF16)) + b_ref[0]


def _adaln(c_all, ada_w, ada_b):
    n_layers, d, n = ada_w.shape
    rows = c_all.shape[0]
    tn = 1536 if n % 1536 == 0 else n
    return pl.pallas_call(
        _adaln_kernel,
        grid=(n_layers, n // tn),
        in_specs=[
            pl.BlockSpec((rows, d), lambda l, j: (0, 0)),
            pl.BlockSpec((1, d, tn), lambda l, j: (l, 0, j)),
            pl.BlockSpec((1, 1, tn), lambda l, j: (l, 0, j)),
        ],
        out_specs=pl.BlockSpec((1, rows, tn), lambda l, j: (l, 0, j)),
        out_shape=jax.ShapeDtypeStruct((n_layers, rows, n), F32),
        compiler_params=_params("parallel", "parallel"),
        name="adaln",
    )(c_all, ada_w, ada_b.reshape(n_layers, 1, n))


def _ffn_kernel(*refs, has_mix, final, n_chunks, chunk):
    it = iter(refs)
    x_ref, sh_ref, sc_ref, gt_ref, g_ref, wgu_ref, wdn_ref = (next(it) for _ in range(7))
    if has_mix:
        o_ref, wo_ref, gm_ref = next(it), next(it), next(it)
    if final:
        fg_ref = next(it)
    out_ref, acc_ref = next(it), next(it)
    nb, tt, d = x_ref.shape
    m = nb * tt
    x = x_ref[...]
    if has_mix:
        x = x + gm_ref[...] * _dot(o_ref[...], wo_ref[...]).reshape(nb, tt, d)
    h = _rms(x, g_ref[...]) * (1.0 + sc_ref[...]) + sh_ref[...]
    hb = h.reshape(m, d).astype(BF16)
    d_ff = wdn_ref.shape[0]
    for c in range(n_chunks):
        lo = c * chunk
        gate = _dot(hb, wgu_ref[:, lo:lo + chunk])
        up = _dot(hb, wgu_ref[:, d_ff + lo:d_ff + lo + chunk])
        act = (gate * _sigmoid(gate) * up).astype(BF16)
        y = _dot(act, wdn_ref[lo:lo + chunk, :])
        if c == 0:
            acc_ref[...] = y
        else:
            acc_ref[...] += y
    y = x + (0.5 * gt_ref[...]) * acc_ref[...].reshape(nb, tt, d)
    if final:
        y = _rms(y, fg_ref[...])
    out_ref[...] = y


def _ffn(x3, mods3, sub, norm_g, w_gu, w_dn, nb, tt, mix=None, final_g=None):
    n_b, n_t, d = x3.shape
    d_ff = w_dn.shape[0]
    chunk = 256
    grid = (n_b // nb, n_t // tt)
    m = nb * tt
    x_spec = pl.BlockSpec((nb, tt, d), lambda i, j: (i, j, 0))

    def mod_spec(col):
        return pl.BlockSpec((nb, 1, d), lambda i, j, col=col: (i, 0, col))

    def const_spec(shape):
        return pl.BlockSpec(shape, lambda i, j: (0,) * len(shape), pipeline_mode=pl.Buffered(1))

    ops = [x3, mods3, mods3, mods3, norm_g.reshape(1, 1, d), w_gu, w_dn]
    specs = [x_spec, mod_spec(3 * sub), mod_spec(3 * sub + 1), mod_spec(3 * sub + 2),
             const_spec((1, 1, d)), const_spec(w_gu.shape), const_spec(w_dn.shape)]
    if mix is not None:
        o2, w_o = mix
        n_tt = n_t // tt
        ops += [o2, w_o, mods3]
        specs += [pl.BlockSpec((m, o2.shape[1]), lambda i, j: (i * n_tt + j, 0)),
                  const_spec(w_o.shape), mod_spec(3 * 1 + 2)]
    if final_g is not None:
        ops.append(final_g.reshape(1, 1, d))
        specs.append(const_spec((1, 1, d)))
    return pl.pallas_call(
        functools.partial(_ffn_kernel, has_mix=mix is not None, final=final_g is not None,
                          n_chunks=d_ff // chunk, chunk=chunk),
        grid=grid,
        in_specs=specs,
        out_specs=x_spec,
        out_shape=jax.ShapeDtypeStruct(x3.shape, F32),
        scratch_shapes=[pltpu.VMEM((m, d), F32)],
        compiler_params=_params("parallel", "parallel"),
        name="ffn",
    )(*ops)


def _mla_proj_kernel(*refs, absorb):
    it = iter(refs)
    x_ref, sh_ref, sc_ref, g_ref, win_ref, gq_ref, gkv_ref, wuq_ref, cos_ref, sin_ref = (
        next(it) for _ in range(10))
    if absorb:
        wukt_ref = next(it)
        qlat_ref, qrope_ref, ckv_ref, kr_ref = (next(it) for _ in range(4))
    else:
        wuk_ref, wuv_ref = next(it), next(it)
        q_ref, k_ref, v_ref, ckv_ref, kr_ref = (next(it) for _ in range(5))
    nb, tt, d = x_ref.shape
    m = nb * tt
    _, h = _modulated(x_ref, sh_ref, sc_ref, g_ref)
    hb = h.reshape(m, d).astype(BF16)
    proj = _dot(hb, win_ref[...])
    c_q = _rms(proj[:, :MLA_Q_LORA], gq_ref[...])
    c_kv = _rms(proj[:, MLA_Q_LORA:MLA_Q_LORA + MLA_KV_LORA], gkv_ref[...])
    ckv_ref[...] = c_kv
    cos_a = cos_ref[...]
    sin_b = sin_ref[...]

    def rope(z):
        return z * cos_a + pltpu.roll(z, 32, 1) * sin_b

    kr = rope(proj[:, MLA_Q_LORA + MLA_KV_LORA:])
    kr_ref[...] = kr[:, :MLA_ROPE]
    qf = _dot(c_q.astype(BF16), wuq_ref[...]) * MLA_SCALE
    ckv_b = c_kv.astype(BF16)
    if not absorb:
        kn = _dot(ckv_b, wuk_ref[...])
        v_ref[...] = _dot(ckv_b, wuv_ref[...]).astype(BF16)
        kr_b = kr.astype(BF16)
    for hd in range(MLA_HEADS):
        b0 = hd * 256
        qn = qf[:, b0:b0 + 128]
        qr = rope(qf[:, b0 + 128:b0 + 256])
        if absorb:
            qlat_ref[:, b0:b0 + 256] = _dot(qn.astype(BF16), wukt_ref[hd]).astype(BF16)
            qrope_ref[:, hd * 128:(hd + 1) * 128] = qr.astype(BF16)
        else:
            q_ref[:, b0:b0 + 128] = qn.astype(BF16)
            q_ref[:, b0 + 128:b0 + 256] = qr.astype(BF16)
            k_ref[:, b0:b0 + 128] = kn[:, hd * 128:(hd + 1) * 128].astype(BF16)
            k_ref[:, b0 + 128:b0 + 256] = kr_b


def _mla_proj(x3, mods3, norm_g, w, cos_a, sin_b, nb, tt, absorb):
    n_b, n_t, d = x3.shape
    m = nb * tt
    n_tt = n_t // tt
    rows = n_b * n_t
    grid = (n_b // nb, n_tt)
    x_spec = pl.BlockSpec((nb, tt, d), lambda i, j: (i, j, 0))

    def mod_spec(col):
        return pl.BlockSpec((nb, 1, d), lambda i, j, col=col: (i, 0, col))

    def const_spec(shape):
        return pl.BlockSpec(shape, lambda i, j: (0,) * len(shape))

    def row_spec(width):
        return pl.BlockSpec((m, width), lambda i, j: (i * n_tt + j, 0))

    tab_spec = pl.BlockSpec((m, LANES), lambda i, j: (j, 0))
    ops = [x3, mods3, mods3, norm_g.reshape(1, 1, d), w["w_in"], w["g_q"], w["g_kv"], w["w_uq"],
           cos_a, sin_b]
    specs = [x_spec, mod_spec(3), mod_spec(4), const_spec((1, 1, d)), const_spec(w["w_in"].shape),
             const_spec(w["g_q"].shape), const_spec(w["g_kv"].shape), const_spec(w["w_uq"].shape),
             tab_spec, tab_spec]
    if absorb:
        ops.append(w["w_ukt"])
        specs.append(const_spec(w["w_ukt"].shape))
        widths = [(MLA_HEADS * 256, BF16), (MLA_HEADS * 128, BF16)]
    else:
        ops += [w["w_uk"], w["w_uv"]]
        specs += [const_spec(w["w_uk"].shape), const_spec(w["w_uv"].shape)]
        widths = [(MLA_HEADS * 256, BF16), (MLA_HEADS * 256, BF16), (MLA_HEADS * MLA_V, BF16)]
    widths += [(MLA_KV_LORA, F32), (MLA_ROPE, F32)]
    return pl.pallas_call(
        functools.partial(_mla_proj_kernel, absorb=absorb),
        grid=grid,
        in_specs=specs,
        out_specs=[row_spec(wd) for wd, _ in widths],
        out_shape=[jax.ShapeDtypeStruct((rows, wd), dt) for wd, dt in widths],
        compiler_params=_params("parallel", "parallel"),
        name="mla_proj",
    )(*ops)


def _flash_kernel(*refs, units, outs, n_kv):
    q_ref = refs[0]
    kv = refs[1:1 + n_kv]
    o_ref = refs[1 + n_kv]
    m_ref, l_ref, acc_ref = refs[2 + n_kv:]
    i = pl.program_id(1)
    j = pl.program_id(2)
    tq = q_ref.shape[0]
    tk = kv[0].shape[0]

    @pl.when(j == 0)
    def _():
        m_ref[...] = jnp.full(m_ref.shape, -jnp.inf, F32)
        l_ref[...] = jnp.zeros(l_ref.shape, F32)
        acc_ref[...] = jnp.zeros(acc_ref.shape, F32)

    def step(masked):
        if masked:
            keep = _iota((tq, tk), 0) >= _iota((tq, tk), 1)
        for u, (qs, ki, ks, vi, vs) in enumerate(units):
            s = _dot_nt(q_ref[:, qs], kv[ki][:, ks])
            if masked:
                s = jnp.where(keep, s, -jnp.inf)
            m_prev = m_ref[u]
            m_new = jnp.maximum(m_prev, jnp.max(s, axis=1, keepdims=True))
            alpha = jnp.exp(m_prev - m_new)
            p = jnp.exp(s - m_new)
            l_ref[u] = alpha * l_ref[u] + jnp.sum(p, axis=1, keepdims=True)
            acc_ref[u] = alpha * acc_ref[u] + _dot(p.astype(BF16), kv[vi][:, vs])
            m_ref[u] = m_new

    @pl.when(j < i)
    def _():
        step(False)

    @pl.when(j == i)
    def _():
        step(True)
        for b, us in enumerate(outs):
            o = acc_ref[us[0]] / l_ref[us[0]]
            for u in us[1:]:
                o = o + acc_ref[u] / l_ref[u]
            o_ref[:, b * LANES:(b + 1) * LANES] = o.astype(o_ref.dtype)


def _flash(q, kvs, units, outs, n_batch, seq, tile):
    nt = seq // tile
    n_units = len(units)

    def q_map(b, i, j):
        return (b * nt + i, 0)

    def kv_map(b, i, j):
        return (b * nt + jnp.minimum(i, j), 0)

    out_w = LANES * len(outs)
    return pl.pallas_call(
        functools.partial(_flash_kernel, units=units, outs=outs, n_kv=len(kvs)),
        grid=(n_batch, nt, nt),
        in_specs=[pl.BlockSpec((tile, q.shape[1]), q_map)]
        + [pl.BlockSpec((tile, a.shape[1]), kv_map) for a in kvs],
        out_specs=pl.BlockSpec((tile, out_w), q_map),
        out_shape=jax.ShapeDtypeStruct((q.shape[0], out_w), BF16),
        scratch_shapes=[pltpu.VMEM((n_units, tile, 1), F32), pltpu.VMEM((n_units, tile, 1), F32),
                        pltpu.VMEM((n_units, tile, LANES), F32)],
        compiler_params=_params("parallel", "parallel", "arbitrary"),
        name="flash",
    )(q, *kvs)


def _softmax_update(s, v_b, m_ref, l_ref, acc_ref):
    m_prev = m_ref[...]
    m_new = jnp.maximum(m_prev, jnp.max(s, axis=1, keepdims=True))
    alpha = jnp.exp(m_prev - m_new)
    p = jnp.exp(s - m_new)
    l_ref[...] = alpha * l_ref[...] + jnp.sum(p, axis=1, keepdims=True)
    acc_ref[...] = alpha * acc_ref[...] + _dot(p.astype(BF16), v_b)
    m_ref[...] = m_new


def _mla_paged_kernel(*refs, n_pg, dec_seq):
    pt_ref, qlat_ref, qrope_ref, ckn_ref, krn_ref = refs[:5]
    ckv_pages = refs[5:5 + n_pg]
    kr_pages = refs[5 + n_pg:5 + 2 * n_pg]
    o_ref, kbuf, krbuf, m_ref, l_ref, acc_ref = refs[5 + 2 * n_pg:]
    c = pl.program_id(1)
    rows = qlat_ref.shape[0]
    q_lat = qlat_ref[...]
    q_rope = qrope_ref[:, :MLA_ROPE]

    @pl.when(c == 0)
    def _():
        ckn = ckn_ref[...]
        s = _dot_nt(q_lat, ckn) + _dot_nt(q_rope, krn_ref[...])
        n_new = ckn.shape[0]
        tok = _iota((rows, n_new), 0) // MLA_HEADS
        s = jnp.where(_iota((rows, n_new), 1) <= tok, s, -jnp.inf)
        m0 = jnp.max(s, axis=1, keepdims=True)
        p = jnp.exp(s - m0)
        m_ref[...] = m0
        l_ref[...] = jnp.sum(p, axis=1, keepdims=True)
        acc_ref[...] = _dot(p.astype(BF16), ckn)

    for j in range(n_pg):
        kbuf[j * PAGE_SIZE:(j + 1) * PAGE_SIZE, :] = ckv_pages[j][...].astype(BF16)
        krbuf[j * PAGE_SIZE:(j + 1) * PAGE_SIZE, :] = kr_pages[j][...].astype(BF16)
    k_b = kbuf[...]
    s = _dot_nt(q_lat, k_b) + _dot_nt(q_rope, krbuf[...])
    _softmax_update(s, k_b, m_ref, l_ref, acc_ref)

    @pl.when(c == pl.num_programs(1) - 1)
    def _():
        o_ref[...] = (acc_ref[...] / l_ref[...]).astype(o_ref.dtype)


def _mla_paged(page_table, q_lat, q_rope, ckv_new, kr_new, cache_ckv, cache_kr, n_pg, dec_seq):
    n_seq, rows, _ = q_lat.shape
    n_chunks = page_table.shape[1] // n_pg

    def seq_spec(a):
        return pl.BlockSpec((None,) + a.shape[1:], lambda b, c, pt: (b, 0, 0))

    def page_spec(width, j):
        return pl.BlockSpec((None, PAGE_SIZE, width), lambda b, c, pt, j=j: (pt[b, c * n_pg + j], 0, 0))

    grid_spec = pltpu.PrefetchScalarGridSpec(
        num_scalar_prefetch=1,
        grid=(n_seq, n_chunks),
        in_specs=[seq_spec(q_lat), seq_spec(q_rope), seq_spec(ckv_new), seq_spec(kr_new)]
        + [page_spec(MLA_KV_LORA, j) for j in range(n_pg)]
        + [page_spec(MLA_ROPE, j) for j in range(n_pg)],
        out_specs=pl.BlockSpec((None, rows, MLA_KV_LORA), lambda b, c, pt: (b, 0, 0)),
        scratch_shapes=[pltpu.VMEM((n_pg * PAGE_SIZE, MLA_KV_LORA), BF16),
                        pltpu.VMEM((n_pg * PAGE_SIZE, MLA_ROPE), BF16),
                        pltpu.VMEM((rows, 1), F32), pltpu.VMEM((rows, 1), F32),
                        pltpu.VMEM((rows, MLA_KV_LORA), F32)],
    )
    return pl.pallas_call(
        functools.partial(_mla_paged_kernel, n_pg=n_pg, dec_seq=dec_seq),
        grid_spec=grid_spec,
        out_shape=jax.ShapeDtypeStruct((n_seq, rows, MLA_KV_LORA), BF16),
        compiler_params=_params("parallel", "arbitrary"),
        name="mla_paged",
    )(page_table, q_lat, q_rope, ckv_new, kr_new, *([cache_ckv] * n_pg), *([cache_kr] * n_pg))


def _mla_uv_kernel(o_ref, wuv_ref, out_ref):
    for hd in range(MLA_HEADS):
        out_ref[:, hd * MLA_V:(hd + 1) * MLA_V] = _dot(
            o_ref[:, hd * MLA_KV_LORA:(hd + 1) * MLA_KV_LORA], wuv_ref[hd]).astype(BF16)


def _mla_uv(o_lat2, w_uv3):
    rows = o_lat2.shape[0]
    return pl.pallas_call(
        _mla_uv_kernel,
        out_shape=jax.ShapeDtypeStruct((rows, MLA_HEADS * MLA_V), BF16),
        compiler_params=pltpu.CompilerParams(vmem_limit_bytes=VMEM_LIMIT_BYTES),
        name="mla_uv",
    )(o_lat2, w_uv3)


def _fox_proj_kernel(*refs, prompt):
    it = iter(refs)
    x_ref, sh_ref, sc_ref, g_ref, win_ref, bf_ref = (next(it) for _ in range(6))
    if prompt:
        sel_ref, selc_ref = next(it), next(it)
        qa_ref, klo_ref, khi_ref, vlo_ref, vhi_ref, k_ref, v_ref, lf_ref, carry_ref = (
            next(it) for _ in range(9))
    else:
        q_ref, k_ref, v_ref, lf_ref = (next(it) for _ in range(4))
    nb, tt, d = x_ref.shape
    m = nb * tt
    qw = FOX_HEADS * FOX_HEAD_DIM
    kw = FOX_KV_HEADS * FOX_HEAD_DIM
    _, h = _modulated(x_ref, sh_ref, sc_ref, g_ref)
    hb = h.reshape(m, d).astype(BF16)
    proj = _dot(hb, win_ref[...])
    q = proj[:, :qw] * FOX_SCALE
    k = proj[:, qw:qw + kw]
    v = proj[:, qw + kw:qw + 2 * kw]
    z = proj[:, qw + 2 * kw:] + bf_ref[...]
    logf = jnp.minimum(z, 0.0) - jnp.log(1.0 + jnp.exp(-jnp.abs(z)))
    k_ref[...] = k
    v_ref[...] = v
    lf_ref[...] = logf[:, :FOX_HEADS]
    if not prompt:
        q_ref[...] = q.astype(BF16)
        return

    @pl.when(pl.program_id(1) == 0)
    def _():
        carry_ref[...] = jnp.zeros(carry_ref.shape, F32)

    lane = _iota((m, LANES), 1)
    logf = jnp.where(lane < FOX_HEADS, logf, 0.0)
    tri = (_iota((m, m), 0) >= _iota((m, m), 1)).astype(BF16)
    f_hi, f_mid, f_lo = _split3(logf)
    cum = _dot(tri, f_hi) + _dot(tri, f_mid) + _dot(tri, f_lo) + carry_ref[...]
    carry_ref[...] = cum[m - 1:m, :]
    hi, mid, lo = _split3(cum)
    comb = (hi.astype(F32) + pltpu.roll(mid.astype(F32), FOX_HEADS, 1)
            + pltpu.roll(lo.astype(F32), 2 * FOX_HEADS, 1)).astype(BF16)
    aug = _dot(comb, sel_ref[...]) + selc_ref[...]
    n_pairs = FOX_HEADS // 2
    for i in range(n_pairs):
        qa_ref[:, i * 256:i * 256 + 128] = q[:, i * 128:(i + 1) * 128].astype(BF16)
        qa_ref[:, i * 256 + 128:(i + 1) * 256] = aug[:, i * 128:(i + 1) * 128].astype(BF16)
    low_half = lane < FOX_HEAD_DIM
    a0 = n_pairs * 128
    for grp in range(FOX_KV_HEADS):
        blk = (grp // 2) * 128
        for src, lo_ref, hi_ref, stride, has_aug in ((k, klo_ref, khi_ref, 256, True),
                                                     (v, vlo_ref, vhi_ref, 128, False)):
            same = src[:, blk:blk + 128]
            swapped = pltpu.roll(same, FOX_HEAD_DIM, 1)
            lo_src, hi_src = (same, swapped) if grp % 2 == 0 else (swapped, same)
            b0 = grp * stride
            lo_ref[:, b0:b0 + 128] = jnp.where(low_half, lo_src, 0.0).astype(BF16)
            hi_ref[:, b0:b0 + 128] = jnp.where(low_half, 0.0, hi_src).astype(BF16)
            if has_aug:
                lo_ref[:, b0 + 128:b0 + 256] = aug[:, a0 + grp * 128:a0 + (grp + 1) * 128].astype(BF16)
                hi_ref[:, b0 + 128:b0 + 256] = aug[:, a0 + 512 + grp * 128:
                                                   a0 + 512 + (grp + 1) * 128].astype(BF16)


def _fox_aug_selectors():
    n_pairs = FOX_HEADS // 2
    sel = np.zeros((LANES, (n_pairs + 2 * FOX_KV_HEADS) * LANES), np.float32)
    const = np.zeros((1, sel.shape[1]), np.float32)
    klo0 = n_pairs * LANES
    khi0 = klo0 + FOX_KV_HEADS * LANES
    for part in range(3):
        src = part * FOX_HEADS
        for i in range(n_pairs):
            sel[src + 2 * i, i * LANES + part] = 1.0
            sel[src + 2 * i + 1, i * LANES + 3 + part] = 1.0
            const[0, i * LANES + 6 + 3 * (i % 2) + part] = 1.0
        for g in range(FOX_KV_HEADS):
            const[0, klo0 + g * LANES + part] = 1.0
            const[0, khi0 + g * LANES + 3 + part] = 1.0
            sel[src + 4 * g, klo0 + g * LANES + 6 + part] = -1.0
            sel[src + 4 * g + 2, klo0 + g * LANES + 9 + part] = -1.0
            sel[src + 4 * g + 1, khi0 + g * LANES + 6 + part] = -1.0
            sel[src + 4 * g + 3, khi0 + g * LANES + 9 + part] = -1.0
    return jnp.asarray(sel, BF16), jnp.asarray(const, F32)


def _fox_proj(x3, mods3, norm_g, w_in, b_f, nb, tt, prompt):
    n_b, n_t, d = x3.shape
    m = nb * tt
    n_tt = n_t // tt
    rows = n_b * n_t
    x_spec = pl.BlockSpec((nb, tt, d), lambda i, j: (i, j, 0))

    def mod_spec(col):
        return pl.BlockSpec((nb, 1, d), lambda i, j, col=col: (i, 0, col))

    def const_spec(shape):
        return pl.BlockSpec(shape, lambda i, j: (0,) * len(shape))

    def row_spec(width):
        return pl.BlockSpec((m, width), lambda i, j: (i * n_tt + j, 0))

    kw = FOX_KV_HEADS * FOX_HEAD_DIM
    ops = [x3, mods3, mods3, norm_g.reshape(1, 1, d), w_in, b_f]
    specs = [x_spec, mod_spec(3), mod_spec(4), const_spec((1, 1, d)), const_spec(w_in.shape),
             const_spec(b_f.shape)]
    scratch = []
    if prompt:
        sel, selc = _fox_aug_selectors()
        ops += [sel, selc]
        specs += [const_spec(sel.shape), const_spec(selc.shape)]
        widths = [(FOX_HEADS // 2 * 256, BF16), (FOX_KV_HEADS * 256, BF16), (FOX_KV_HEADS * 256, BF16),
                  (FOX_KV_HEADS * 128, BF16), (FOX_KV_HEADS * 128, BF16)]
        scratch = [pltpu.VMEM((1, LANES), F32)]
    else:
        widths = [(FOX_HEADS * FOX_HEAD_DIM, BF16)]
    widths += [(kw, F32), (kw, F32), (FOX_HEADS, F32)]
    return pl.pallas_call(
        functools.partial(_fox_proj_kernel, prompt=prompt),
        grid=(n_b // nb, n_tt),
        in_specs=specs,
        out_specs=[row_spec(wd) for wd, _ in widths],
        out_shape=[jax.ShapeDtypeStruct((rows, wd), dt) for wd, dt in widths],
        scratch_shapes=scratch,
        compiler_params=_params("parallel", "arbitrary"),
        name="fox_proj",
    )(*ops)


def _fox_paged_kernel(*refs, n_pg, dec_seq):
    pt_ref, q_ref, lfn_ref, kn_ref, vn_ref = refs[:5]
    k_pages = refs[5:5 + n_pg]
    v_pages = refs[5 + n_pg:5 + 2 * n_pg]
    lf_pages = refs[5 + 2 * n_pg:5 + 3 * n_pg]
    o_ref, kbuf, vbuf, m_ref, l_ref, acc_ref, run_ref, ncol_ref = refs[5 + 3 * n_pg:]
    c = pl.program_id(1)
    rows = q_ref.shape[0]
    q = q_ref[...]
    eye = (_iota((LANES, LANES), 0) == _iota((LANES, LANES), 1)).astype(BF16)

    @pl.when(c == 0)
    def _():
        lfn = lfn_ref[...]
        n_new = lfn.shape[0]
        row = _iota((n_new, LANES), 0)
        n_cum = jnp.zeros((n_new, LANES), F32)
        for t in range(dec_seq):
            n_cum = n_cum + jnp.where(row >= t, lfn[t:t + 1, :], 0.0)
        hi, mid, lo = _split3(n_cum)
        n_cum_t = _dot_nt(eye, hi) + _dot_nt(eye, mid) + _dot_nt(eye, lo)
        lane = _iota((dec_seq, LANES), 1)
        s = _dot_nt(q, kn_ref[...])
        pieces = []
        for hd in range(FOX_HEADS):
            col = jnp.sum(jnp.where(lane == hd, n_cum[:dec_seq, :], 0.0), axis=1, keepdims=True)
            ncol_ref[hd * dec_seq:(hd + 1) * dec_seq, :] = col
            pieces.append(s[hd * dec_seq:(hd + 1) * dec_seq, :] + col - n_cum_t[hd:hd + 1, :])
        s = jnp.concatenate(pieces, axis=0)
        tok = _iota((rows, n_new), 0) % dec_seq
        s = jnp.where(_iota((rows, n_new), 1) <= tok, s, -jnp.inf)
        m0 = jnp.max(s, axis=1, keepdims=True)
        p = jnp.exp(s - m0)
        m_ref[...] = m0
        l_ref[...] = jnp.sum(p, axis=1, keepdims=True)
        acc_ref[...] = _dot(p.astype(BF16), vn_ref[...])
        run_ref[...] = jnp.zeros(run_ref.shape, F32)

    for j in range(n_pg):
        kbuf[j * PAGE_SIZE:(j + 1) * PAGE_SIZE, :] = k_pages[j][...].astype(BF16)
        vbuf[j * PAGE_SIZE:(j + 1) * PAGE_SIZE, :] = v_pages[j][...].astype(BF16)
    lft = jnp.concatenate([lf_pages[j][...] for j in range(n_pg)], axis=0)
    later = (_iota((LANES, LANES), 0) > _iota((LANES, LANES), 1)).astype(BF16)
    ones = jnp.ones((LANES, LANES), BF16)
    suf = _dot3(lft, later)
    tot = _dot3(lft, ones)
    run = run_ref[...]
    bias_pages = [None] * n_pg
    for j in reversed(range(n_pg)):
        bias_pages[j] = run + suf[j * FOX_HEADS:(j + 1) * FOX_HEADS, :]
        run = run + tot[j * FOX_HEADS:(j + 1) * FOX_HEADS, :]
    run_ref[...] = run
    bias = jnp.concatenate(bias_pages, axis=1)
    s = _dot_nt(q, kbuf[...])
    s = jnp.concatenate([s[hd * dec_seq:(hd + 1) * dec_seq, :] + bias[hd:hd + 1, :]
                         for hd in range(FOX_HEADS)], axis=0)
    s = s + ncol_ref[...]
    _softmax_update(s, vbuf[...], m_ref, l_ref, acc_ref)

    @pl.when(c == pl.num_programs(1) - 1)
    def _():
        o_ref[...] = acc_ref[...] / l_ref[...]


def _fox_paged(page_table, q_all, lf_new, k_new, v_new, cache_k, cache_v, cache_lft, n_pg, dec_seq):
    n_seq, rows, width = q_all.shape
    n_chunks = page_table.shape[1] // n_pg

    def seq_spec(a):
        return pl.BlockSpec((None,) + a.shape[1:], lambda b, c, pt: (b, 0, 0))

    def page_spec(shape, j):
        return pl.BlockSpec((None,) + shape,
                            lambda b, c, pt, j=j: (pt[b, (n_chunks - 1 - c) * n_pg + j], 0, 0))

    grid_spec = pltpu.PrefetchScalarGridSpec(
        num_scalar_prefetch=1,
        grid=(n_seq, n_chunks),
        in_specs=[seq_spec(q_all), seq_spec(lf_new), seq_spec(k_new), seq_spec(v_new)]
        + [page_spec((PAGE_SIZE, width), j) for j in range(n_pg)]
        + [page_spec((PAGE_SIZE, width), j) for j in range(n_pg)]
        + [page_spec((FOX_HEADS, PAGE_SIZE), j) for j in range(n_pg)],
        out_specs=pl.BlockSpec((None, rows, width), lambda b, c, pt: (b, 0, 0)),
        scratch_shapes=[pltpu.VMEM((n_pg * PAGE_SIZE, width), BF16),
                        pltpu.VMEM((n_pg * PAGE_SIZE, width), BF16),
                        pltpu.VMEM((rows, 1), F32), pltpu.VMEM((rows, 1), F32),
                        pltpu.VMEM((rows, width), F32),
                        pltpu.VMEM((FOX_HEADS, LANES), F32),
                        pltpu.VMEM((rows, 1), F32)],
    )
    return pl.pallas_call(
        functools.partial(_fox_paged_kernel, n_pg=n_pg, dec_seq=dec_seq),
        grid_spec=grid_spec,
        out_shape=jax.ShapeDtypeStruct((n_seq, rows, width), F32),
        compiler_params=_params("parallel", "arbitrary"),
        name="fox_paged",
    )(page_table, q_all, lf_new, k_new, v_new,
      *([cache_k] * n_pg), *([cache_v] * n_pg), *([cache_lft] * n_pg))


def _rope_tables(pos):
    half = MLA_ROPE // 2
    inv_freq = ROPE_THETA ** (-jnp.arange(half, dtype=F32) / half)
    ang = pos.astype(F32)[:, None] * inv_freq[None, :]
    cos, sin, zero = jnp.cos(ang), jnp.sin(ang), jnp.zeros_like(ang)
    return (jnp.concatenate([cos, cos, zero, zero], axis=1),
            jnp.concatenate([-sin, sin, zero, zero], axis=1))


def _mla_weights(w_in, g_q, w_uq, g_kv, w_uk, w_uv, w_o):
    lat = MLA_Q_LORA + MLA_KV_LORA
    half = MLA_ROPE // 2
    kr_cols = np.concatenate([np.arange(lat, lat + MLA_ROPE)] * 2)
    in_cols = np.concatenate([np.arange(lat), kr_cols])
    uq_cols = []
    for hd in range(MLA_HEADS):
        b0 = hd * (MLA_NOPE + MLA_ROPE)
        rope_cols = np.arange(b0 + MLA_NOPE, b0 + MLA_NOPE + MLA_ROPE)
        uq_cols += [np.arange(b0, b0 + MLA_NOPE), rope_cols, rope_cols]
    del half
    return dict(
        w_in=w_in[:, in_cols].astype(BF16),
        g_q=g_q.reshape(1, -1), g_kv=g_kv.reshape(1, -1),
        w_uq=w_uq[:, np.concatenate(uq_cols)].astype(BF16),
        w_uk=w_uk.reshape(MLA_KV_LORA, MLA_HEADS * MLA_NOPE).astype(BF16),
        w_uv=w_uv.reshape(MLA_KV_LORA, MLA_HEADS * MLA_V).astype(BF16),
        w_ukt=jnp.transpose(w_uk, (1, 2, 0)).astype(BF16),
        w_uv3=jnp.transpose(w_uv, (1, 0, 2)).astype(BF16),
        w_o=w_o.astype(BF16),
    )


def _pad_rows(a, rows):
    return jnp.pad(a, ((0, 0), (0, rows - a.shape[1])) + ((0, 0),) * (a.ndim - 2))


def _pages_per_step(n_pages, want):
    n = min(want, n_pages)
    while n_pages % n:
        n -= 1
    return n


def kernel(x_prompt, x_sample, c_prompt, c_sample, cache_mla_ckv, cache_mla_krope, cache_fox_k, cache_fox_v, cache_fox_logf, page_table, ada_w, ada_b, norm_g, ffn_w_gu, ffn_w_dn, final_g, mla_w_in, mla_g_q, mla_w_uq, mla_g_kv, mla_w_uk, mla_w_uv, mla_w_o, fox_w_in, fox_b_f, fox_w_o):
    n_p, seq, d = x_prompt.shape
    n_s, dec_seq, _ = x_sample.shape
    n_pages = page_table.shape[1]
    past_len = n_pages * PAGE_SIZE
    tile = min(512, seq)
    n_pad = 16

    mods = _adaln(jnp.concatenate([c_prompt, c_sample], axis=0), ada_w, ada_b)
    w_gu = ffn_w_gu.astype(BF16)
    w_dn = ffn_w_dn.astype(BF16)
    xp, xs = x_prompt, x_sample

    mp = mods[0, :n_p].reshape(n_p, 1, -1)
    ms = mods[0, n_p:].reshape(n_s, 1, -1)
    w = _mla_weights(mla_w_in[0], mla_g_q[0], mla_w_uq[0], mla_g_kv[0], mla_w_uk[0], mla_w_uv[0],
                     mla_w_o[0])
    xp = _ffn(xp, mp, 0, norm_g[0, 0], w_gu[0, 0], w_dn[0, 0], 1, tile)
    xs = _ffn(xs, ms, 0, norm_g[0, 0], w_gu[0, 0], w_dn[0, 0], n_s, dec_seq)

    cos_p, sin_p = _rope_tables(jnp.arange(seq, dtype=jnp.int32))
    q, k, v, p_ckv, p_kr = _mla_proj(xp, mp, norm_g[0, 1], w, cos_p, sin_p, 1, tile, absorb=False)
    units = [(slice(h * 256, (h + 1) * 256), 0, slice(h * 256, (h + 1) * 256), 1,
              slice(h * 128, (h + 1) * 128)) for h in range(MLA_HEADS)]
    o_p = _flash(q, [k, v], units, [[h] for h in range(MLA_HEADS)], n_p, seq, tile)

    cos_s, sin_s = _rope_tables(past_len + jnp.arange(dec_seq, dtype=jnp.int32))
    cos_s, sin_s = jnp.tile(cos_s, (n_s, 1)), jnp.tile(sin_s, (n_s, 1))
    q_lat, q_rope, s_ckv, s_kr = _mla_proj(xs, ms, norm_g[0, 1], w, cos_s, sin_s, n_s, dec_seq,
                                           absorb=True)
    rows = dec_seq * MLA_HEADS
    n_pg = _pages_per_step(n_pages, 16)
    o_lat = _mla_paged(
        page_table,
        q_lat.reshape(n_s, rows, MLA_KV_LORA), q_rope.reshape(n_s, rows, LANES),
        _pad_rows(s_ckv.reshape(n_s, dec_seq, MLA_KV_LORA), n_pad).astype(BF16),
        _pad_rows(s_kr.reshape(n_s, dec_seq, MLA_ROPE), n_pad).astype(BF16),
        cache_mla_ckv[0], cache_mla_krope[0], n_pg, dec_seq)
    o_s = _mla_uv(o_lat.reshape(n_s * dec_seq, MLA_HEADS * MLA_KV_LORA), w["w_uv3"])

    xp = _ffn(xp, mp, 2, norm_g[0, 2], w_gu[0, 1], w_dn[0, 1], 1, tile, mix=(o_p, w["w_o"]))
    xs = _ffn(xs, ms, 2, norm_g[0, 2], w_gu[0, 1], w_dn[0, 1], n_s, dec_seq, mix=(o_s, w["w_o"]))

    mp = mods[1, :n_p].reshape(n_p, 1, -1)
    ms = mods[1, n_p:].reshape(n_s, 1, -1)
    qw = FOX_HEADS * FOX_HEAD_DIM
    kw = FOX_KV_HEADS * FOX_HEAD_DIM
    fw_in = jnp.pad(fox_w_in[0], ((0, 0), (0, LANES - FOX_HEADS))).astype(BF16)
    fb = jnp.pad(fox_b_f[0], (0, LANES - FOX_HEADS)).reshape(1, LANES)
    fw_o = fox_w_o[0].astype(BF16)
    xp = _ffn(xp, mp, 0, norm_g[1, 0], w_gu[1, 0], w_dn[1, 0], 1, tile)
    xs = _ffn(xs, ms, 0, norm_g[1, 0], w_gu[1, 0], w_dn[1, 0], n_s, dec_seq)

    qa, klo, khi, vlo, vhi, p_k, p_v, p_lf = _fox_proj(xp, mp, norm_g[1, 1], fw_in, fb, 1, tile,
                                                       prompt=True)
    units = []
    for h in range(FOX_HEADS):
        i, par, g = h // 2, h % 2, h // (FOX_HEADS // FOX_KV_HEADS)
        units.append((slice(i * 256, (i + 1) * 256), par, slice(g * 256, (g + 1) * 256), 2 + par,
                      slice(g * 128, (g + 1) * 128)))
    o_p = _flash(qa, [klo, khi, vlo, vhi], units, [[2 * i, 2 * i + 1] for i in range(FOX_HEADS // 2)],
                 n_p, seq, tile)

    q_s, s_k, s_v, s_lf = _fox_proj(xs, ms, norm_g[1, 1], fw_in, fb, n_s, dec_seq, prompt=False)
    q4 = jnp.transpose(q_s.reshape(n_s, dec_seq, FOX_HEADS, FOX_HEAD_DIM), (0, 2, 1, 3))
    grp_of_head = jnp.arange(FOX_HEADS) // (FOX_HEADS // FOX_KV_HEADS)
    onehot = (grp_of_head[:, None] == jnp.arange(FOX_KV_HEADS)[None, :]).astype(BF16)
    q_all = (q4[:, :, :, None, :] * onehot[None, :, None, :, None]).reshape(
        n_s, FOX_HEADS * dec_seq, kw)
    lf_new = jnp.pad(s_lf.reshape(n_s, dec_seq, FOX_HEADS),
                     ((0, 0), (0, n_pad - dec_seq), (0, LANES - FOX_HEADS)))
    cache_lft = jnp.swapaxes(cache_fox_logf[0], 1, 2)
    n_pg = _pages_per_step(n_pages, 16)
    o_all = _fox_paged(
        page_table, q_all, lf_new,
        _pad_rows(s_k.reshape(n_s, dec_seq, kw), n_pad).astype(BF16),
        _pad_rows(s_v.reshape(n_s, dec_seq, kw), n_pad).astype(BF16),
        cache_fox_k[0].reshape(-1, PAGE_SIZE, kw), cache_fox_v[0].reshape(-1, PAGE_SIZE, kw),
        cache_lft, n_pg, dec_seq)
    o5 = o_all.reshape(n_s, FOX_HEADS, dec_seq, FOX_KV_HEADS, FOX_HEAD_DIM)
    o_sel = jnp.take_along_axis(o5, grp_of_head[None, :, None, None, None], axis=3)[:, :, :, 0, :]
    o_s = jnp.transpose(o_sel, (0, 2, 1, 3)).reshape(n_s * dec_seq, qw).astype(BF16)

    xp = _ffn(xp, mp, 2, norm_g[1, 2], w_gu[1, 1], w_dn[1, 1], 1, tile, mix=(o_p, fw_o),
              final_g=final_g)
    xs = _ffn(xs, ms, 2, norm_g[1, 2], w_gu[1, 1], w_dn[1, 1], n_s, dec_seq, mix=(o_s, fw_o),
              final_g=final_g)

    def prompt_state(a, *tail):
        return a.reshape(1, n_p, seq, *tail)

    def sample_state(a, *tail):
        return a.reshape(1, n_s, dec_seq, *tail)

    return (xp, xs,
            prompt_state(p_ckv, MLA_KV_LORA), prompt_state(p_kr, MLA_ROPE),
            prompt_state(p_k, FOX_KV_HEADS, FOX_HEAD_DIM), prompt_state(p_v, FOX_KV_HEADS, FOX_HEAD_DIM),
            prompt_state(p_lf, FOX_HEADS),
            sample_state(s_ckv, MLA_KV_LORA), sample_state(s_kr, MLA_ROPE),
            sample_state(s_k, FOX_KV_HEADS, FOX_HEAD_DIM), sample_state(s_v, FOX_KV_HEADS, FOX_HEAD_DIM),
            sample_state(s_lf, FOX_HEADS))
```

```python
import functools
import math

import numpy as np
import jax
import jax.numpy as jnp
from jax import lax
from jax.experimental import pallas as pl
from jax.experimental.pallas import tpu as pltpu

F32 = jnp.float32
BF16 = jnp.bfloat16

LANES = 128
NORM_EPS = 1e-6
PAGE_SIZE = 128
D_FF = 2816
MLA_HEADS = 8
MLA_NOPE = 128
MLA_ROPE = 64
MLA_V = 128
MLA_Q_LORA = 512
MLA_KV_LORA = 256
ROPE_THETA = 10000.0
MLA_SCALE = 1.0 / math.sqrt(MLA_NOPE + MLA_ROPE)
FOX_HEADS = 16
FOX_KV_HEADS = 4
FOX_HEAD_DIM = 64
FOX_SCALE = 1.0 / math.sqrt(FOX_HEAD_DIM)
VMEM_LIMIT_BYTES = 56 * 1024 * 1024


def _params(*sem):
    return pltpu.CompilerParams(dimension_semantics=sem, vmem_limit_bytes=VMEM_LIMIT_BYTES)


def _dot(a, b):
    return jnp.dot(a, b, preferred_element_type=F32)


def _dot_nt(a, b):
    return lax.dot_general(a, b, (((1,), (1,)), ((), ())), preferred_element_type=F32)


def _sigmoid(x):
    return 1.0 / (1.0 + jnp.exp(-x))


def _rms(x, g):
    return x * lax.rsqrt(jnp.mean(x * x, axis=-1, keepdims=True) + NORM_EPS) * g


def _split3(x):
    hi = x.astype(BF16)
    r = x - hi.astype(F32)
    mid = r.astype(BF16)
    lo = (r - mid.astype(F32)).astype(BF16)
    return hi, mid, lo


def _dot3(x, m):
    hi, mid, lo = _split3(x)
    return _dot(hi, m) + _dot(mid, m) + _dot(lo, m)


def _iota(shape, dim):
    return lax.broadcasted_iota(jnp.int32, shape, dim)


def _modulated(x_ref, sh_ref, sc_ref, g_ref):
    x = x_ref[...]
    return x, _rms(x, g_ref[...]) * (1.0 + sc_ref[...]) + sh_ref[...]


def _adaln_kernel(c_ref, w_ref, b_ref, o_ref):
    c = c_ref[...]
    a = (c * _sigmoid(c)).astype(BF16)
    o_ref[0] = _dot(a, w_ref[0].astype(BF16)) + b_ref[0]


def _adaln(c_all, ada_w, ada_b):
    n_layers, d, n = ada_w.shape
    rows = c_all.shape[0]
    tn = 1536 if n % 1536 == 0 else n
    return pl.pallas_call(
        _adaln_kernel,
        grid=(n_layers, n // tn),
        in_specs=[
            pl.BlockSpec((rows, d), lambda l, j: (0, 0)),
            pl.BlockSpec((1, d, tn), lambda l, j: (l, 0, j)),
            pl.BlockSpec((1, 1, tn), lambda l, j: (l, 0, j)),
        ],
        out_specs=pl.BlockSpec((1, rows, tn), lambda l, j: (l, 0, j)),
        out_shape=jax.ShapeDtypeStruct((n_layers, rows, n), F32),
        compiler_params=_params("parallel", "parallel"),
        name="adaln",
    )(c_all, ada_w, ada_b.reshape(n_layers, 1, n))


def _ffn_kernel(*refs, has_mix, final, n_chunks, chunk):
    it = iter(refs)
    x_ref, sh_ref, sc_ref, gt_ref, g_ref, wgu_ref, wdn_ref = (next(it) for _ in range(7))
    if has_mix:
        o_ref, wo_ref, gm_ref = next(it), next(it), next(it)
    if final:
        fg_ref = next(it)
    out_ref, acc_ref = next(it), next(it)
    nb, tt, d = x_ref.shape
    m = nb * tt
    x = x_ref[...]
    if has_mix:
        x = x + gm_ref[...] * _dot(o_ref[...], wo_ref[...]).reshape(nb, tt, d)
    h = _rms(x, g_ref[...]) * (1.0 + sc_ref[...]) + sh_ref[...]
    hb = h.reshape(m, d).astype(BF16)
    d_ff = wdn_ref.shape[0]
    for c in range(n_chunks):
        lo = c * chunk
        gate = _dot(hb, wgu_ref[:, lo:lo + chunk])
        up = _dot(hb, wgu_ref[:, d_ff + lo:d_ff + lo + chunk])
        act = (gate * _sigmoid(gate) * up).astype(BF16)
        y = _dot(act, wdn_ref[lo:lo + chunk, :])
        if c == 0:
            acc_ref[...] = y
        else:
            acc_ref[...] += y
    y = x + (0.5 * gt_ref[...]) * acc_ref[...].reshape(nb, tt, d)
    if final:
        y = _rms(y, fg_ref[...])
    out_ref[...] = y


def _ffn(x3, mods3, sub, norm_g, w_gu, w_dn, nb, tt, mix=None, final_g=None):
    n_b, n_t, d = x3.shape
    d_ff = w_dn.shape[0]
    chunk = 256
    grid = (n_b // nb, n_t // tt)
    m = nb * tt
    x_spec = pl.BlockSpec((nb, tt, d), lambda i, j: (i, j, 0))

    def mod_spec(col):
        return pl.BlockSpec((nb, 1, d), lambda i, j, col=col: (i, 0, col))

    def const_spec(shape):
        return pl.BlockSpec(shape, lambda i, j: (0,) * len(shape), pipeline_mode=pl.Buffered(1))

    ops = [x3, mods3, mods3, mods3, norm_g.reshape(1, 1, d), w_gu, w_dn]
    specs = [x_spec, mod_spec(3 * sub), mod_spec(3 * sub + 1), mod_spec(3 * sub + 2),
             const_spec((1, 1, d)), const_spec(w_gu.shape), const_spec(w_dn.shape)]
    if mix is not None:
        o2, w_o = mix
        n_tt = n_t // tt
        ops += [o2, w_o, mods3]
        specs += [pl.BlockSpec((m, o2.shape[1]), lambda i, j: (i * n_tt + j, 0)),
                  const_spec(w_o.shape), mod_spec(3 * 1 + 2)]
    if final_g is not None:
        ops.append(final_g.reshape(1, 1, d))
        specs.append(const_spec((1, 1, d)))
    return pl.pallas_call(
        functools.partial(_ffn_kernel, has_mix=mix is not None, final=final_g is not None,
                          n_chunks=d_ff // chunk, chunk=chunk),
        grid=grid,
        in_specs=specs,
        out_specs=x_spec,
        out_shape=jax.ShapeDtypeStruct(x3.shape, F32),
        scratch_shapes=[pltpu.VMEM((m, d), F32)],
        compiler_params=_params("parallel", "parallel"),
        name="ffn",
    )(*ops)


def _mla_proj_kernel(*refs, absorb):
    it = iter(refs)
    x_ref, sh_ref, sc_ref, g_ref, win_ref, gq_ref, gkv_ref, wuq_ref, cos_ref, sin_ref = (
        next(it) for _ in range(10))
    if absorb:
        wukt_ref = next(it)
        qlat_ref, qrope_ref, ckv_ref, kr_ref = (next(it) for _ in range(4))
    else:
        wuk_ref, wuv_ref = next(it), next(it)
        q_ref, k_ref, v_ref, ckv_ref, kr_ref = (next(it) for _ in range(5))
    nb, tt, d = x_ref.shape
    m = nb * tt
    _, h = _modulated(x_ref, sh_ref, sc_ref, g_ref)
    hb = h.reshape(m, d).astype(BF16)
    proj = _dot(hb, win_ref[...])
    c_q = _rms(proj[:, :MLA_Q_LORA], gq_ref[...])
    c_kv = _rms(proj[:, MLA_Q_LORA:MLA_Q_LORA + MLA_KV_LORA], gkv_ref[...])
    ckv_ref[...] = c_kv
    cos_a = cos_ref[...]
    sin_b = sin_ref[...]

    def rope(z):
        return z * cos_a + pltpu.roll(z, 32, 1) * sin_b

    kr = rope(proj[:, MLA_Q_LORA + MLA_KV_LORA:])
    kr_ref[...] = kr[:, :MLA_ROPE]
    qf = _dot(c_q.astype(BF16), wuq_ref[...]) * MLA_SCALE
    ckv_b = c_kv.astype(BF16)
    if not absorb:
        kn = _dot(ckv_b, wuk_ref[...])
        v_ref[...] = _dot(ckv_b, wuv_ref[...]).astype(BF16)
        kr_b = kr.astype(BF16)
    for hd in range(MLA_HEADS):
        b0 = hd * 256
        qn = qf[:, b0:b0 + 128]
        qr = rope(qf[:, b0 + 128:b0 + 256])
        if absorb:
            qlat_ref[:, b0:b0 + 256] = _dot(qn.astype(BF16), wukt_ref[hd]).astype(BF16)
            qrope_ref[:, hd * 128:(hd + 1) * 128] = qr.astype(BF16)
        else:
            q_ref[:, b0:b0 + 128] = qn.astype(BF16)
            q_ref[:, b0 + 128:b0 + 256] = qr.astype(BF16)
            k_ref[:, b0:b0 + 128] = kn[:, hd * 128:(hd + 1) * 128].astype(BF16)
            k_ref[:, b0 + 128:b0 + 256] = kr_b


def _mla_proj(x3, mods3, norm_g, w, cos_a, sin_b, nb, tt, absorb):
    n_b, n_t, d = x3.shape
    m = nb * tt
    n_tt = n_t // tt
    rows = n_b * n_t
    grid = (n_b // nb, n_tt)
    x_spec = pl.BlockSpec((nb, tt, d), lambda i, j: (i, j, 0))

    def mod_spec(col):
        return pl.BlockSpec((nb, 1, d), lambda i, j, col=col: (i, 0, col))

    def const_spec(shape):
        return pl.BlockSpec(shape, lambda i, j: (0,) * len(shape))

    def row_spec(width):
        return pl.BlockSpec((m, width), lambda i, j: (i * n_tt + j, 0))

    tab_spec = pl.BlockSpec((m, LANES), lambda i, j: (j, 0))
    ops = [x3, mods3, mods3, norm_g.reshape(1, 1, d), w["w_in"], w["g_q"], w["g_kv"], w["w_uq"],
           cos_a, sin_b]
    specs = [x_spec, mod_spec(3), mod_spec(4), const_spec((1, 1, d)), const_spec(w["w_in"].shape),
             const_spec(w["g_q"].shape), const_spec(w["g_kv"].shape), const_spec(w["w_uq"].shape),
             tab_spec, tab_spec]
    if absorb:
        ops.append(w["w_ukt"])
        specs.append(const_spec(w["w_ukt"].shape))
        widths = [(MLA_HEADS * 256, BF16), (MLA_HEADS * 128, BF16)]
    else:
        ops += [w["w_uk"], w["w_uv"]]
        specs += [const_spec(w["w_uk"].shape), const_spec(w["w_uv"].shape)]
        widths = [(MLA_HEADS * 256, BF16), (MLA_HEADS * 256, BF16), (MLA_HEADS * MLA_V, BF16)]
    widths += [(MLA_KV_LORA, F32), (MLA_ROPE, F32)]
    return pl.pallas_call(
        functools.partial(_mla_proj_kernel, absorb=absorb),
        grid=grid,
        in_specs=specs,
        out_specs=[row_spec(wd) for wd, _ in widths],
        out_shape=[jax.ShapeDtypeStruct((rows, wd), dt) for wd, dt in widths],
        compiler_params=_params("parallel", "parallel"),
        name="mla_proj",
    )(*ops)


def _online_softmax(s, m_prev, l_prev, row_bias=None):
    blocks = [s[:, b * LANES:(b + 1) * LANES] for b in range(s.shape[1] // LANES)]
    if row_bias is not None:
        blocks = [blk + row_bias for blk in blocks]
    mx = blocks[0]
    for blk in blocks[1:]:
        mx = jnp.maximum(mx, blk)
    m_new = jnp.maximum(m_prev, jnp.max(mx, axis=1, keepdims=True))
    alpha = jnp.exp(m_prev - m_new)
    ps = [jnp.exp(blk - m_new) for blk in blocks]
    psum = ps[0]
    for p in ps[1:]:
        psum = psum + p
    p_b = jnp.concatenate([p.astype(BF16) for p in ps], axis=1) if len(ps) > 1 else ps[0].astype(BF16)
    return m_new, alpha, alpha * l_prev + psum, p_b


def _pipelined(n, produce, consume):
    nxt = produce(0)
    for b in range(n):
        cur = nxt
        if b + 1 < n:
            nxt = produce(b + 1)
        consume(b, cur)


def _flash_kernel(*refs, units, outs, n_kv):
    q_ref = refs[0]
    kv = refs[1:1 + n_kv]
    o_ref = refs[1 + n_kv]
    m_ref, l_ref, acc_ref = refs[2 + n_kv:]
    i = pl.program_id(1)
    j = pl.program_id(2)
    tq = q_ref.shape[0]
    tk = kv[0].shape[0]
    j_last = ((i + 1) * tq - 1) // tk

    @pl.when(j == 0)
    def _():
        m_ref[...] = jnp.full(m_ref.shape, -jnp.inf, F32)
        l_ref[...] = jnp.zeros(l_ref.shape, F32)
        acc_ref[...] = jnp.zeros(acc_ref.shape, F32)

    def step(masked):
        if masked:
            keep = (i * tq + _iota((tq, tk), 0)) >= (j * tk + _iota((tq, tk), 1))

        def scores(u):
            qs, ki, ks, _, _ = units[u]
            s = _dot_nt(q_ref[:, qs], kv[ki][:, ks])
            return jnp.where(keep, s, -jnp.inf) if masked else s

        def update(u, s):
            _, _, _, vi, vs = units[u]
            m_new, alpha, l_new, p_b = _online_softmax(s, m_ref[u], l_ref[u])
            l_ref[u] = l_new
            acc_ref[u] = alpha * acc_ref[u] + _dot(p_b, kv[vi][:, vs])
            m_ref[u] = m_new

        _pipelined(len(units), scores, update)

    fully_visible = (j + 1) * tk <= i * tq

    @pl.when(fully_visible)
    def _():
        step(False)

    @pl.when(jnp.logical_and(jnp.logical_not(fully_visible), j <= j_last))
    def _():
        step(True)

    @pl.when(j == j_last)
    def _():
        for b, us in enumerate(outs):
            o = None
            for u in us:
                term = acc_ref[u] / jnp.sum(l_ref[u], axis=1, keepdims=True)
                o = term if o is None else o + term
            o_ref[:, b * LANES:(b + 1) * LANES] = o.astype(o_ref.dtype)


def _flash(q, kvs, units, outs, n_batch, seq, tq, tk):
    nq = seq // tq
    nk = seq // tk
    n_units = len(units)

    def q_map(b, i, j):
        return (b * nq + i, 0)

    def kv_map(b, i, j):
        return (b * nk + jnp.minimum(j, ((i + 1) * tq - 1) // tk), 0)

    out_w = LANES * len(outs)
    return pl.pallas_call(
        functools.partial(_flash_kernel, units=units, outs=outs, n_kv=len(kvs)),
        grid=(n_batch, nq, nk),
        in_specs=[pl.BlockSpec((tq, q.shape[1]), q_map)]
        + [pl.BlockSpec((tk, a.shape[1]), kv_map) for a in kvs],
        out_specs=pl.BlockSpec((tq, out_w), q_map),
        out_shape=jax.ShapeDtypeStruct((q.shape[0], out_w), BF16),
        scratch_shapes=[pltpu.VMEM((n_units, tq, LANES), F32), pltpu.VMEM((n_units, tq, LANES), F32),
                        pltpu.VMEM((n_units, tq, LANES), F32)],
        compiler_params=_params("parallel", "parallel", "arbitrary"),
        name="flash",
    )(q, *kvs)


def _mla_paged_kernel(*refs, n_pg, n_sub):
    pt_ref, qlat_ref, qrope_ref, ckn_ref, krn_ref = refs[:5]
    ckv_pages = refs[5:5 + n_pg]
    kr_pages = refs[5 + n_pg:5 + 2 * n_pg]
    o_ref, kbuf, krbuf, m_ref, l_ref, acc_ref = refs[5 + 2 * n_pg:]
    c = pl.program_id(1)
    rows = qlat_ref.shape[0]
    q_lat = qlat_ref[...]
    q_rope = qrope_ref[:, :MLA_ROPE]

    @pl.when(c == 0)
    def _():
        ckn = ckn_ref[...]
        n_new = ckn.shape[0]
        s = _dot_nt(q_lat, ckn) + _dot_nt(q_rope, krn_ref[...])
        s = jnp.where(_iota((rows, n_new), 1) <= _iota((rows, n_new), 0) // MLA_HEADS, s, -jnp.inf)
        m_new, _, l_new, p_b = _online_softmax(
            s, jnp.full((rows, LANES), -jnp.inf, F32), jnp.zeros((rows, LANES), F32))
        m_ref[...] = m_new
        l_ref[...] = l_new
        acc_ref[...] = _dot(p_b, ckn)

    sub = n_pg // n_sub
    width = sub * PAGE_SIZE

    def scores(b):
        for j in range(b * sub, (b + 1) * sub):
            kbuf[j * PAGE_SIZE:(j + 1) * PAGE_SIZE, :] = ckv_pages[j][...].astype(BF16)
            krbuf[:, j * PAGE_SIZE:(j + 1) * PAGE_SIZE] = kr_pages[j][...].astype(BF16)
        return (_dot_nt(q_lat, kbuf[b * width:(b + 1) * width, :])
                + _dot(q_rope, krbuf[:, b * width:(b + 1) * width]))

    def update(b, s):
        m_new, alpha, l_new, p_b = _online_softmax(s, m_ref[...], l_ref[...])
        l_ref[...] = l_new
        acc_ref[...] = jnp.concatenate([alpha] * (MLA_KV_LORA // LANES), axis=1) * acc_ref[...] + _dot(
            p_b, kbuf[b * width:(b + 1) * width, :])
        m_ref[...] = m_new

    _pipelined(n_sub, scores, update)

    @pl.when(c == pl.num_programs(1) - 1)
    def _():
        l_tot = jnp.sum(l_ref[...], axis=1, keepdims=True)
        o_ref[...] = (acc_ref[...] / l_tot).astype(o_ref.dtype)


def _mla_paged(page_table, q_lat, q_rope, ckv_new, kr_new, cache_ckv, cache_krt, n_pg, n_sub):
    n_seq, rows, _ = q_lat.shape
    n_chunks = page_table.shape[1] // n_pg

    def seq_spec(a):
        return pl.BlockSpec((None,) + a.shape[1:], lambda b, c, pt: (b, 0, 0))

    def page_spec(shape, j):
        return pl.BlockSpec((None,) + shape, lambda b, c, pt, j=j: (pt[b, c * n_pg + j], 0, 0))

    grid_spec = pltpu.PrefetchScalarGridSpec(
        num_scalar_prefetch=1,
        grid=(n_seq, n_chunks),
        in_specs=[seq_spec(q_lat), seq_spec(q_rope), seq_spec(ckv_new), seq_spec(kr_new)]
        + [page_spec((PAGE_SIZE, MLA_KV_LORA), j) for j in range(n_pg)]
        + [page_spec((MLA_ROPE, PAGE_SIZE), j) for j in range(n_pg)],
        out_specs=pl.BlockSpec((None, rows, MLA_KV_LORA), lambda b, c, pt: (b, 0, 0)),
        scratch_shapes=[pltpu.VMEM((n_pg * PAGE_SIZE, MLA_KV_LORA), BF16),
                        pltpu.VMEM((MLA_ROPE, n_pg * PAGE_SIZE), BF16),
                        pltpu.VMEM((rows, LANES), F32), pltpu.VMEM((rows, LANES), F32),
                        pltpu.VMEM((rows, MLA_KV_LORA), F32)],
    )
    return pl.pallas_call(
        functools.partial(_mla_paged_kernel, n_pg=n_pg, n_sub=n_sub),
        grid_spec=grid_spec,
        out_shape=jax.ShapeDtypeStruct((n_seq, rows, MLA_KV_LORA), BF16),
        compiler_params=_params("parallel", "arbitrary"),
        name="mla_paged",
    )(page_table, q_lat, q_rope, ckv_new, kr_new, *([cache_ckv] * n_pg), *([cache_krt] * n_pg))


def _mla_uv_kernel(o_ref, wuv_ref, out_ref):
    for hd in range(MLA_HEADS):
        out_ref[:, hd * MLA_V:(hd + 1) * MLA_V] = _dot(
            o_ref[:, hd * MLA_KV_LORA:(hd + 1) * MLA_KV_LORA], wuv_ref[hd]).astype(BF16)


def _mla_uv(o_lat2, w_uv3):
    rows = o_lat2.shape[0]
    return pl.pallas_call(
        _mla_uv_kernel,
        out_shape=jax.ShapeDtypeStruct((rows, MLA_HEADS * MLA_V), BF16),
        compiler_params=pltpu.CompilerParams(vmem_limit_bytes=VMEM_LIMIT_BYTES),
        name="mla_uv",
    )(o_lat2, w_uv3)


def _fox_proj_kernel(*refs, prompt):
    it = iter(refs)
    x_ref, sh_ref, sc_ref, g_ref, win_ref, bf_ref = (next(it) for _ in range(6))
    if prompt:
        sel_ref, selc_ref = next(it), next(it)
        qa_ref, klo_ref, khi_ref, vlo_ref, vhi_ref, k_ref, v_ref, lf_ref, carry_ref = (
            next(it) for _ in range(9))
    else:
        q_ref, k_ref, v_ref, lf_ref = (next(it) for _ in range(4))
    nb, tt, d = x_ref.shape
    m = nb * tt
    qw = FOX_HEADS * FOX_HEAD_DIM
    kw = FOX_KV_HEADS * FOX_HEAD_DIM
    _, h = _modulated(x_ref, sh_ref, sc_ref, g_ref)
    hb = h.reshape(m, d).astype(BF16)
    proj = _dot(hb, win_ref[...])
    q = proj[:, :qw] * FOX_SCALE
    k = proj[:, qw:qw + kw]
    v = proj[:, qw + kw:qw + 2 * kw]
    z = proj[:, qw + 2 * kw:] + bf_ref[...]
    logf = jnp.minimum(z, 0.0) - jnp.log(1.0 + jnp.exp(-jnp.abs(z)))
    k_ref[...] = k
    v_ref[...] = v
    lf_ref[...] = logf[:, :FOX_HEADS]
    if not prompt:
        q_ref[...] = q.astype(BF16)
        return

    @pl.when(pl.program_id(1) == 0)
    def _():
        carry_ref[...] = jnp.zeros(carry_ref.shape, F32)

    lane = _iota((m, LANES), 1)
    logf = jnp.where(lane < FOX_HEADS, logf, 0.0)
    tri = (_iota((m, m), 0) >= _iota((m, m), 1)).astype(BF16)
    f_hi, f_mid, f_lo = _split3(logf)
    cum = _dot(tri, f_hi) + _dot(tri, f_mid) + _dot(tri, f_lo) + carry_ref[...]
    carry_ref[...] = cum[m - 1:m, :]
    hi, mid, lo = _split3(cum)
    comb = (hi.astype(F32) + pltpu.roll(mid.astype(F32), FOX_HEADS, 1)
            + pltpu.roll(lo.astype(F32), 2 * FOX_HEADS, 1)).astype(BF16)
    aug = _dot(comb, sel_ref[...]) + selc_ref[...]
    n_pairs = FOX_HEADS // 2
    for i in range(n_pairs):
        qa_ref[:, i * 256:i * 256 + 128] = q[:, i * 128:(i + 1) * 128].astype(BF16)
        qa_ref[:, i * 256 + 128:(i + 1) * 256] = aug[:, i * 128:(i + 1) * 128].astype(BF16)
    low_half = lane < FOX_HEAD_DIM
    a0 = n_pairs * 128
    for grp in range(FOX_KV_HEADS):
        blk = (grp // 2) * 128
        for src, lo_ref, hi_ref, stride, has_aug in ((k, klo_ref, khi_ref, 256, True),
                                                     (v, vlo_ref, vhi_ref, 128, False)):
            same = src[:, blk:blk + 128]
            swapped = pltpu.roll(same, FOX_HEAD_DIM, 1)
            lo_src, hi_src = (same, swapped) if grp % 2 == 0 else (swapped, same)
            b0 = grp * stride
            lo_ref[:, b0:b0 + 128] = jnp.where(low_half, lo_src, 0.0).astype(BF16)
            hi_ref[:, b0:b0 + 128] = jnp.where(low_half, 0.0, hi_src).astype(BF16)
            if has_aug:
                lo_ref[:, b0 + 128:b0 + 256] = aug[:, a0 + grp * 128:a0 + (grp + 1) * 128].astype(BF16)
                hi_ref[:, b0 + 128:b0 + 256] = aug[:, a0 + 512 + grp * 128:
                                                   a0 + 512 + (grp + 1) * 128].astype(BF16)


def _fox_aug_selectors():
    n_pairs = FOX_HEADS // 2
    sel = np.zeros((LANES, (n_pairs + 2 * FOX_KV_HEADS) * LANES), np.float32)
    const = np.zeros((1, sel.shape[1]), np.float32)
    klo0 = n_pairs * LANES
    khi0 = klo0 + FOX_KV_HEADS * LANES
    for part in range(3):
        src = part * FOX_HEADS
        for i in range(n_pairs):
            sel[src + 2 * i, i * LANES + part] = 1.0
            sel[src + 2 * i + 1, i * LANES + 3 + part] = 1.0
            const[0, i * LANES + 6 + 3 * (i % 2) + part] = 1.0
        for g in range(FOX_KV_HEADS):
            const[0, klo0 + g * LANES + part] = 1.0
            const[0, khi0 + g * LANES + 3 + part] = 1.0
            sel[src + 4 * g, klo0 + g * LANES + 6 + part] = -1.0
            sel[src + 4 * g + 2, klo0 + g * LANES + 9 + part] = -1.0
            sel[src + 4 * g + 1, khi0 + g * LANES + 6 + part] = -1.0
            sel[src + 4 * g + 3, khi0 + g * LANES + 9 + part] = -1.0
    return jnp.asarray(sel, BF16), jnp.asarray(const, F32)


def _fox_proj(x3, mods3, norm_g, w_in, b_f, nb, tt, prompt):
    n_b, n_t, d = x3.shape
    m = nb * tt
    n_tt = n_t // tt
    rows = n_b * n_t
    x_spec = pl.BlockSpec((nb, tt, d), lambda i, j: (i, j, 0))

    def mod_spec(col):
        return pl.BlockSpec((nb, 1, d), lambda i, j, col=col: (i, 0, col))

    def const_spec(shape):
        return pl.BlockSpec(shape, lambda i, j: (0,) * len(shape))

    def row_spec(width):
        return pl.BlockSpec((m, width), lambda i, j: (i * n_tt + j, 0))

    kw = FOX_KV_HEADS * FOX_HEAD_DIM
    ops = [x3, mods3, mods3, norm_g.reshape(1, 1, d), w_in, b_f]
    specs = [x_spec, mod_spec(3), mod_spec(4), const_spec((1, 1, d)), const_spec(w_in.shape),
             const_spec(b_f.shape)]
    scratch = []
    if prompt:
        sel, selc = _fox_aug_selectors()
        ops += [sel, selc]
        specs += [const_spec(sel.shape), const_spec(selc.shape)]
        widths = [(FOX_HEADS // 2 * 256, BF16), (FOX_KV_HEADS * 256, BF16), (FOX_KV_HEADS * 256, BF16),
                  (FOX_KV_HEADS * 128, BF16), (FOX_KV_HEADS * 128, BF16)]
        scratch = [pltpu.VMEM((1, LANES), F32)]
    else:
        widths = [(FOX_HEADS * FOX_HEAD_DIM, BF16)]
    widths += [(kw, F32), (kw, F32), (FOX_HEADS, F32)]
    return pl.pallas_call(
        functools.partial(_fox_proj_kernel, prompt=prompt),
        grid=(n_b // nb, n_tt),
        in_specs=specs,
        out_specs=[row_spec(wd) for wd, _ in widths],
        out_shape=[jax.ShapeDtypeStruct((rows, wd), dt) for wd, dt in widths],
        scratch_shapes=scratch,
        compiler_params=_params("parallel", "arbitrary"),
        name="fox_proj",
    )(*ops)


def _fox_paged_kernel(*refs, n_pg, n_sub, dec_seq):
    pt_ref, q_ref, lfn_ref, kn_ref, vn_ref = refs[:5]
    k_pages = refs[5:5 + n_pg]
    v_pages = refs[5 + n_pg:5 + 2 * n_pg]
    lf_pages = refs[5 + 2 * n_pg:5 + 3 * n_pg]
    o_ref, kbuf, vbuf, m_ref, l_ref, acc_ref, run_ref, ncol_ref = refs[5 + 3 * n_pg:]
    c = pl.program_id(1)
    rows, width = q_ref.shape
    q = q_ref[...]

    def add_head_rows(s, per_head):
        return jnp.concatenate([s[hd * dec_seq:(hd + 1) * dec_seq, :] + per_head[hd:hd + 1, :]
                                for hd in range(FOX_HEADS)], axis=0)

    @pl.when(c == 0)
    def _():
        lfn = lfn_ref[...]
        n_new = lfn.shape[0]
        row = _iota((n_new, LANES), 0)
        n_cum = jnp.zeros((n_new, LANES), F32)
        for t in range(dec_seq):
            n_cum = n_cum + jnp.where(row >= t, lfn[t:t + 1, :], 0.0)
        eye = (_iota((LANES, LANES), 0) == _iota((LANES, LANES), 1)).astype(BF16)
        hi, mid, lo = _split3(n_cum)
        n_cum_t = _dot_nt(eye, hi) + _dot_nt(eye, mid) + _dot_nt(eye, lo)
        lane = _iota((dec_seq, LANES), 1)
        for hd in range(FOX_HEADS):
            col = jnp.sum(jnp.where(lane == hd, n_cum[:dec_seq, :], 0.0), axis=1, keepdims=True)
            ncol_ref[hd * dec_seq:(hd + 1) * dec_seq, :] = jnp.broadcast_to(col, (dec_seq, LANES))
        s = add_head_rows(_dot_nt(q, kn_ref[...]), -n_cum_t)
        s = jnp.where(_iota((rows, n_new), 1) <= _iota((rows, n_new), 0) % dec_seq, s, -jnp.inf)
        m_new, _, l_new, p_b = _online_softmax(
            s, jnp.full((rows, LANES), -jnp.inf, F32), jnp.zeros((rows, LANES), F32), ncol_ref[...])
        m_ref[...] = m_new
        l_ref[...] = l_new
        acc_ref[...] = _dot(p_b, vn_ref[...])
        run_ref[...] = jnp.zeros(run_ref.shape, F32)

    lft = jnp.concatenate([lf_pages[j][...] for j in range(n_pg)], axis=0)
    later = (_iota((LANES, LANES), 0) > _iota((LANES, LANES), 1)).astype(BF16)
    both = _dot3(lft, jnp.concatenate([later, jnp.ones((LANES, LANES), BF16)], axis=1))
    run = run_ref[...]
    bias_pages = [None] * n_pg
    for j in reversed(range(n_pg)):
        bias_pages[j] = run + both[j * FOX_HEADS:(j + 1) * FOX_HEADS, :LANES]
        run = run + both[j * FOX_HEADS:(j + 1) * FOX_HEADS, LANES:]
    run_ref[...] = run

    sub = n_pg // n_sub
    span = sub * PAGE_SIZE
    ncol = ncol_ref[...]

    def scores(b):
        for j in range(b * sub, (b + 1) * sub):
            kbuf[:, j * PAGE_SIZE:(j + 1) * PAGE_SIZE] = k_pages[j][...].astype(BF16)
            vbuf[:, j * PAGE_SIZE:(j + 1) * PAGE_SIZE] = v_pages[j][...].astype(BF16)
        return _dot(q, kbuf[:, b * span:(b + 1) * span])

    def update(b, s):
        bias = jnp.concatenate(bias_pages[b * sub:(b + 1) * sub], axis=1)
        m_new, alpha, l_new, p_b = _online_softmax(add_head_rows(s, bias), m_ref[...], l_ref[...], ncol)
        l_ref[...] = l_new
        acc_ref[...] = jnp.concatenate([alpha] * (width // LANES), axis=1) * acc_ref[...] + _dot_nt(
            p_b, vbuf[:, b * span:(b + 1) * span])
        m_ref[...] = m_new

    _pipelined(n_sub, scores, update)

    @pl.when(c == pl.num_programs(1) - 1)
    def _():
        o_ref[...] = acc_ref[...] / jnp.sum(l_ref[...], axis=1, keepdims=True)


def _fox_paged(page_table, q_all, lf_new, k_new, v_new, cache_kt, cache_vt, cache_lft, n_pg, n_sub, dec_seq):
    n_seq, rows, width = q_all.shape
    n_chunks = page_table.shape[1] // n_pg

    def seq_spec(a):
        return pl.BlockSpec((None,) + a.shape[1:], lambda b, c, pt: (b, 0, 0))

    def page_spec(shape, j):
        return pl.BlockSpec((None,) + shape,
                            lambda b, c, pt, j=j: (pt[b, (n_chunks - 1 - c) * n_pg + j], 0, 0))

    grid_spec = pltpu.PrefetchScalarGridSpec(
        num_scalar_prefetch=1,
        grid=(n_seq, n_chunks),
        in_specs=[seq_spec(q_all), seq_spec(lf_new), seq_spec(k_new), seq_spec(v_new)]
        + [page_spec((width, PAGE_SIZE), j) for j in range(n_pg)]
        + [page_spec((width, PAGE_SIZE), j) for j in range(n_pg)]
        + [page_spec((FOX_HEADS, PAGE_SIZE), j) for j in range(n_pg)],
        out_specs=pl.BlockSpec((None, rows, width), lambda b, c, pt: (b, 0, 0)),
        scratch_shapes=[pltpu.VMEM((width, n_pg * PAGE_SIZE), BF16),
                        pltpu.VMEM((width, n_pg * PAGE_SIZE), BF16),
                        pltpu.VMEM((rows, LANES), F32), pltpu.VMEM((rows, LANES), F32),
                        pltpu.VMEM((rows, width), F32),
                        pltpu.VMEM((FOX_HEADS, LANES), F32),
                        pltpu.VMEM((rows, LANES), F32)],
    )
    return pl.pallas_call(
        functools.partial(_fox_paged_kernel, n_pg=n_pg, n_sub=n_sub, dec_seq=dec_seq),
        grid_spec=grid_spec,
        out_shape=jax.ShapeDtypeStruct((n_seq, rows, width), F32),
        compiler_params=_params("parallel", "arbitrary"),
        name="fox_paged",
    )(page_table, q_all, lf_new, k_new, v_new,
      *([cache_kt] * n_pg), *([cache_vt] * n_pg), *([cache_lft] * n_pg))


def _rope_tables(pos):
    half = MLA_ROPE // 2
    inv_freq = ROPE_THETA ** (-jnp.arange(half, dtype=F32) / half)
    ang = pos.astype(F32)[:, None] * inv_freq[None, :]
    cos, sin, zero = jnp.cos(ang), jnp.sin(ang), jnp.zeros_like(ang)
    return (jnp.concatenate([cos, cos, zero, zero], axis=1),
            jnp.concatenate([-sin, sin, zero, zero], axis=1))


def _mla_weights(w_in, g_q, w_uq, g_kv, w_uk, w_uv, w_o):
    lat = MLA_Q_LORA + MLA_KV_LORA
    kr_cols = np.concatenate([np.arange(lat, lat + MLA_ROPE)] * 2)
    in_cols = np.concatenate([np.arange(lat), kr_cols])
    uq_cols = []
    for hd in range(MLA_HEADS):
        b0 = hd * (MLA_NOPE + MLA_ROPE)
        rope_cols = np.arange(b0 + MLA_NOPE, b0 + MLA_NOPE + MLA_ROPE)
        uq_cols += [np.arange(b0, b0 + MLA_NOPE), rope_cols, rope_cols]
    return dict(
        w_in=w_in[:, in_cols].astype(BF16),
        g_q=g_q.reshape(1, -1), g_kv=g_kv.reshape(1, -1),
        w_uq=w_uq[:, np.concatenate(uq_cols)].astype(BF16),
        w_uk=w_uk.reshape(MLA_KV_LORA, MLA_HEADS * MLA_NOPE).astype(BF16),
        w_uv=w_uv.reshape(MLA_KV_LORA, MLA_HEADS * MLA_V).astype(BF16),
        w_ukt=jnp.transpose(w_uk, (1, 2, 0)).astype(BF16),
        w_uv3=jnp.transpose(w_uv, (1, 0, 2)).astype(BF16),
        w_o=w_o.astype(BF16),
    )


def _pad_rows(a, rows):
    return jnp.pad(a, ((0, 0), (0, rows - a.shape[1])) + ((0, 0),) * (a.ndim - 2))


def _pages_per_step(n_pages, want):
    n = min(want, n_pages)
    while n_pages % n:
        n -= 1
    return n


def kernel(x_prompt, x_sample, c_prompt, c_sample, cache_mla_ckv, cache_mla_krope, cache_fox_k, cache_fox_v, cache_fox_logf, page_table, ada_w, ada_b, norm_g, ffn_w_gu, ffn_w_dn, final_g, mla_w_in, mla_g_q, mla_w_uq, mla_g_kv, mla_w_uk, mla_w_uv, mla_w_o, fox_w_in, fox_b_f, fox_w_o):
    n_p, seq, d = x_prompt.shape
    n_s, dec_seq, _ = x_sample.shape
    n_pages = page_table.shape[1]
    past_len = n_pages * PAGE_SIZE
    tile = min(512, seq)
    tk = min(1024, seq)
    n_pad = LANES

    mods = _adaln(jnp.concatenate([c_prompt, c_sample], axis=0), ada_w, ada_b)
    w_gu = ffn_w_gu.astype(BF16)
    w_dn = ffn_w_dn.astype(BF16)
    xp, xs = x_prompt, x_sample

    mp = mods[0, :n_p].reshape(n_p, 1, -1)
    ms = mods[0, n_p:].reshape(n_s, 1, -1)
    w = _mla_weights(mla_w_in[0], mla_g_q[0], mla_w_uq[0], mla_g_kv[0], mla_w_uk[0], mla_w_uv[0],
                     mla_w_o[0])
    xp = _ffn(xp, mp, 0, norm_g[0, 0], w_gu[0, 0], w_dn[0, 0], 1, tile)
    xs = _ffn(xs, ms, 0, norm_g[0, 0], w_gu[0, 0], w_dn[0, 0], n_s, dec_seq)

    cos_p, sin_p = _rope_tables(jnp.arange(seq, dtype=jnp.int32))
    q, k, v, p_ckv, p_kr = _mla_proj(xp, mp, norm_g[0, 1], w, cos_p, sin_p, 1, tile, absorb=False)
    units = [(slice(h * 256, (h + 1) * 256), 0, slice(h * 256, (h + 1) * 256), 1,
              slice(h * 128, (h + 1) * 128)) for h in range(MLA_HEADS)]
    o_p = _flash(q, [k, v], units, [[h] for h in range(MLA_HEADS)], n_p, seq, tile, tk)

    cos_s, sin_s = _rope_tables(past_len + jnp.arange(dec_seq, dtype=jnp.int32))
    cos_s, sin_s = jnp.tile(cos_s, (n_s, 1)), jnp.tile(sin_s, (n_s, 1))
    q_lat, q_rope, s_ckv, s_kr = _mla_proj(xs, ms, norm_g[0, 1], w, cos_s, sin_s, n_s, dec_seq,
                                           absorb=True)
    rows = dec_seq * MLA_HEADS
    n_pg = _pages_per_step(n_pages, 32)
    n_sub = 4 if n_pg % 4 == 0 else 1
    o_lat = _mla_paged(
        page_table,
        q_lat.reshape(n_s, rows, MLA_KV_LORA), q_rope.reshape(n_s, rows, LANES),
        _pad_rows(s_ckv.reshape(n_s, dec_seq, MLA_KV_LORA), n_pad).astype(BF16),
        _pad_rows(s_kr.reshape(n_s, dec_seq, MLA_ROPE), n_pad).astype(BF16),
        cache_mla_ckv[0], jnp.swapaxes(cache_mla_krope[0], 1, 2), n_pg, n_sub)
    o_s = _mla_uv(o_lat.reshape(n_s * dec_seq, MLA_HEADS * MLA_KV_LORA), w["w_uv3"])

    xp = _ffn(xp, mp, 2, norm_g[0, 2], w_gu[0, 1], w_dn[0, 1], 1, tile, mix=(o_p, w["w_o"]))
    xs = _ffn(xs, ms, 2, norm_g[0, 2], w_gu[0, 1], w_dn[0, 1], n_s, dec_seq, mix=(o_s, w["w_o"]))

    mp = mods[1, :n_p].reshape(n_p, 1, -1)
    ms = mods[1, n_p:].reshape(n_s, 1, -1)
    qw = FOX_HEADS * FOX_HEAD_DIM
    kw = FOX_KV_HEADS * FOX_HEAD_DIM
    fw_in = jnp.pad(fox_w_in[0], ((0, 0), (0, LANES - FOX_HEADS))).astype(BF16)
    fb = jnp.pad(fox_b_f[0], (0, LANES - FOX_HEADS)).reshape(1, LANES)
    fw_o = fox_w_o[0].astype(BF16)
    xp = _ffn(xp, mp, 0, norm_g[1, 0], w_gu[1, 0], w_dn[1, 0], 1, tile)
    xs = _ffn(xs, ms, 0, norm_g[1, 0], w_gu[1, 0], w_dn[1, 0], n_s, dec_seq)

    qa, klo, khi, vlo, vhi, p_k, p_v, p_lf = _fox_proj(xp, mp, norm_g[1, 1], fw_in, fb, 1, tile,
                                                       prompt=True)
    units = []
    for h in range(FOX_HEADS):
        i, par, g = h // 2, h % 2, h // (FOX_HEADS // FOX_KV_HEADS)
        units.append((slice(i * 256, (i + 1) * 256), par, slice(g * 256, (g + 1) * 256), 2 + par,
                      slice(g * 128, (g + 1) * 128)))
    o_p = _flash(qa, [klo, khi, vlo, vhi], units, [[2 * i, 2 * i + 1] for i in range(FOX_HEADS // 2)],
                 n_p, seq, tile, tk)

    q_s, s_k, s_v, s_lf = _fox_proj(xs, ms, norm_g[1, 1], fw_in, fb, n_s, dec_seq, prompt=False)
    q4 = jnp.transpose(q_s.reshape(n_s, dec_seq, FOX_HEADS, FOX_HEAD_DIM), (0, 2, 1, 3))
    grp_of_head = jnp.arange(FOX_HEADS) // (FOX_HEADS // FOX_KV_HEADS)
    onehot = (grp_of_head[:, None] == jnp.arange(FOX_KV_HEADS)[None, :]).astype(BF16)
    q_all = (q4[:, :, :, None, :] * onehot[None, :, None, :, None]).reshape(
        n_s, FOX_HEADS * dec_seq, kw)
    lf_new = jnp.pad(s_lf.reshape(n_s, dec_seq, FOX_HEADS),
                     ((0, 0), (0, n_pad - dec_seq), (0, LANES - FOX_HEADS)))
    cache_lft = jnp.swapaxes(cache_fox_logf[0], 1, 2)
    cache_kt = jnp.transpose(cache_fox_k[0], (0, 2, 3, 1)).reshape(-1, kw, PAGE_SIZE)
    cache_vt = jnp.transpose(cache_fox_v[0], (0, 2, 3, 1)).reshape(-1, kw, PAGE_SIZE)
    o_all = _fox_paged(
        page_table, q_all, lf_new,
        _pad_rows(s_k.reshape(n_s, dec_seq, kw), n_pad).astype(BF16),
        _pad_rows(s_v.reshape(n_s, dec_seq, kw), n_pad).astype(BF16),
        cache_kt, cache_vt, cache_lft, n_pg, n_sub, dec_seq)
    o5 = o_all.reshape(n_s, FOX_HEADS, dec_seq, FOX_KV_HEADS, FOX_HEAD_DIM)
    o_sel = jnp.take_along_axis(o5, grp_of_head[None, :, None, None, None], axis=3)[:, :, :, 0, :]
    o_s = jnp.transpose(o_sel, (0, 2, 1, 3)).reshape(n_s * dec_seq, qw).astype(BF16)

    xp = _ffn(xp, mp, 2, norm_g[1, 2], w_gu[1, 1], w_dn[1, 1], 1, tile, mix=(o_p, fw_o),
              final_g=final_g)
    xs = _ffn(xs, ms, 2, norm_g[1, 2], w_gu[1, 1], w_dn[1, 1], n_s, dec_seq, mix=(o_s, fw_o),
              final_g=final_g)

    def prompt_state(a, *tail):
        return a.reshape(1, n_p, seq, *tail)

    def sample_state(a, *tail):
        return a.reshape(1, n_s, dec_seq, *tail)

    return (xp, xs,
            prompt_state(p_ckv, MLA_KV_LORA), prompt_state(p_kr, MLA_ROPE),
            prompt_state(p_k, FOX_KV_HEADS, FOX_HEAD_DIM), prompt_state(p_v, FOX_KV_HEADS, FOX_HEAD_DIM),
            prompt_state(p_lf, FOX_HEADS),
            sample_state(s_ckv, MLA_KV_LORA), sample_state(s_kr, MLA_ROPE),
            sample_state(s_k, FOX_KV_HEADS, FOX_HEAD_DIM), sample_state(s_v, FOX_KV_HEADS, FOX_HEAD_DIM),
            sample_state(s_lf, FOX_HEADS))
```

```python
import functools
import math

import numpy as np
import jax
import jax.numpy as jnp
from jax import lax
from jax.experimental import pallas as pl
from jax.experimental.pallas import tpu as pltpu

F32 = jnp.float32
BF16 = jnp.bfloat16

LANES = 128
NORM_EPS = 1e-6
PAGE_SIZE = 128
D_FF = 2816
MLA_HEADS = 8
MLA_NOPE = 128
MLA_ROPE = 64
MLA_V = 128
MLA_Q_LORA = 512
MLA_KV_LORA = 256
ROPE_THETA = 10000.0
MLA_SCALE = 1.0 / math.sqrt(MLA_NOPE + MLA_ROPE)
FOX_HEADS = 16
FOX_KV_HEADS = 4
FOX_HEAD_DIM = 64
FOX_SCALE = 1.0 / math.sqrt(FOX_HEAD_DIM)
LOG2E = math.log2(math.e)
VMEM_LIMIT_BYTES = 56 * 1024 * 1024


def _params(*sem):
    return pltpu.CompilerParams(dimension_semantics=sem, vmem_limit_bytes=VMEM_LIMIT_BYTES)


def _dot(a, b):
    return jnp.dot(a, b, preferred_element_type=F32)


def _dot_nt(a, b):
    return lax.dot_general(a, b, (((1,), (1,)), ((), ())), preferred_element_type=F32)


def _sigmoid(x):
    return 1.0 / (1.0 + jnp.exp(-x))


def _rms(x, g):
    return x * lax.rsqrt(jnp.mean(x * x, axis=-1, keepdims=True) + NORM_EPS) * g


def _split3(x):
    hi = x.astype(BF16)
    r = x - hi.astype(F32)
    mid = r.astype(BF16)
    lo = (r - mid.astype(F32)).astype(BF16)
    return hi, mid, lo


def _dot3(x, m):
    hi, mid, lo = _split3(x)
    return _dot(hi, m) + _dot(mid, m) + _dot(lo, m)


def _iota(shape, dim):
    return lax.broadcasted_iota(jnp.int32, shape, dim)


def _modulated(x_ref, sh_ref, sc_ref, g_ref):
    x = x_ref[...]
    return x, _rms(x, g_ref[...]) * (1.0 + sc_ref[...]) + sh_ref[...]


def _adaln_kernel(c_ref, w_ref, b_ref, o_ref):
    c = c_ref[...]
    a = (c * _sigmoid(c)).astype(BF16)
    o_ref[0] = _dot(a, w_ref[0].astype(BF16)) + b_ref[0]


def _adaln(c_all, ada_w, ada_b):
    n_layers, d, n = ada_w.shape
    rows = c_all.shape[0]
    tn = 1536 if n % 1536 == 0 else n
    return pl.pallas_call(
        _adaln_kernel,
        grid=(n_layers, n // tn),
        in_specs=[
            pl.BlockSpec((rows, d), lambda l, j: (0, 0)),
            pl.BlockSpec((1, d, tn), lambda l, j: (l, 0, j)),
            pl.BlockSpec((1, 1, tn), lambda l, j: (l, 0, j)),
        ],
        out_specs=pl.BlockSpec((1, rows, tn), lambda l, j: (l, 0, j)),
        out_shape=jax.ShapeDtypeStruct((n_layers, rows, n), F32),
        compiler_params=_params("parallel", "parallel"),
        name="adaln",
    )(c_all, ada_w, ada_b.reshape(n_layers, 1, n))


def _ffn_kernel(*refs, has_mix, final, n_chunks, chunk):
    it = iter(refs)
    x_ref, sh_ref, sc_ref, gt_ref, g_ref, wgu_ref, wdn_ref = (next(it) for _ in range(7))
    if has_mix:
        o_ref, wo_ref, gm_ref = next(it), next(it), next(it)
    if final:
        fg_ref = next(it)
    out_ref, acc_ref = next(it), next(it)
    nb, tt, d = x_ref.shape
    m = nb * tt
    x = x_ref[...]
    if has_mix:
        x = x + gm_ref[...] * _dot(o_ref[...], wo_ref[...]).reshape(nb, tt, d)
    h = _rms(x, g_ref[...]) * (1.0 + sc_ref[...]) + sh_ref[...]
    hb = h.reshape(m, d).astype(BF16)
    d_ff = wdn_ref.shape[0]
    for c in range(n_chunks):
        lo = c * chunk
        gate = _dot(hb, wgu_ref[:, lo:lo + chunk])
        up = _dot(hb, wgu_ref[:, d_ff + lo:d_ff + lo + chunk])
        act = (gate * _sigmoid(gate) * up).astype(BF16)
        y = _dot(act, wdn_ref[lo:lo + chunk, :])
        if c == 0:
            acc_ref[...] = y
        else:
            acc_ref[...] += y
    y = x + (0.5 * gt_ref[...]) * acc_ref[...].reshape(nb, tt, d)
    if final:
        y = _rms(y, fg_ref[...])
    out_ref[...] = y


def _ffn(x3, mods3, sub, norm_g, w_gu, w_dn, nb, tt, mix=None, final_g=None):
    n_b, n_t, d = x3.shape
    d_ff = w_dn.shape[0]
    chunk = 256
    grid = (n_b // nb, n_t // tt)
    m = nb * tt
    x_spec = pl.BlockSpec((nb, tt, d), lambda i, j: (i, j, 0))

    def mod_spec(col):
        return pl.BlockSpec((nb, 1, d), lambda i, j, col=col: (i, 0, col))

    def const_spec(shape):
        return pl.BlockSpec(shape, lambda i, j: (0,) * len(shape), pipeline_mode=pl.Buffered(1))

    ops = [x3, mods3, mods3, mods3, norm_g.reshape(1, 1, d), w_gu, w_dn]
    specs = [x_spec, mod_spec(3 * sub), mod_spec(3 * sub + 1), mod_spec(3 * sub + 2),
             const_spec((1, 1, d)), const_spec(w_gu.shape), const_spec(w_dn.shape)]
    if mix is not None:
        o2, w_o = mix
        n_tt = n_t // tt
        ops += [o2, w_o, mods3]
        specs += [pl.BlockSpec((m, o2.shape[1]), lambda i, j: (i * n_tt + j, 0)),
                  const_spec(w_o.shape), mod_spec(3 * 1 + 2)]
    if final_g is not None:
        ops.append(final_g.reshape(1, 1, d))
        specs.append(const_spec((1, 1, d)))
    return pl.pallas_call(
        functools.partial(_ffn_kernel, has_mix=mix is not None, final=final_g is not None,
                          n_chunks=d_ff // chunk, chunk=chunk),
        grid=grid,
        in_specs=specs,
        out_specs=x_spec,
        out_shape=jax.ShapeDtypeStruct(x3.shape, F32),
        scratch_shapes=[pltpu.VMEM((m, d), F32)],
        compiler_params=_params("parallel", "parallel"),
        name="ffn",
    )(*ops)


def _mla_proj_kernel(*refs, absorb):
    it = iter(refs)
    x_ref, sh_ref, sc_ref, g_ref, win_ref, gq_ref, gkv_ref, wuq_ref, cos_ref, sin_ref = (
        next(it) for _ in range(10))
    if absorb:
        wukt_ref = next(it)
        qlat_ref, qrope_ref, ckv_ref, kr_ref = (next(it) for _ in range(4))
    else:
        wuk_ref, wuv_ref = next(it), next(it)
        q_ref, k_ref, v_ref, ckv_ref, kr_ref = (next(it) for _ in range(5))
    nb, tt, d = x_ref.shape
    m = nb * tt
    _, h = _modulated(x_ref, sh_ref, sc_ref, g_ref)
    hb = h.reshape(m, d).astype(BF16)
    proj = _dot(hb, win_ref[...])
    c_q = _rms(proj[:, :MLA_Q_LORA], gq_ref[...])
    c_kv = _rms(proj[:, MLA_Q_LORA:MLA_Q_LORA + MLA_KV_LORA], gkv_ref[...])
    ckv_ref[...] = c_kv
    cos_a = cos_ref[...]
    sin_b = sin_ref[...]

    def rope(z):
        return z * cos_a + pltpu.roll(z, 32, 1) * sin_b

    kr = rope(proj[:, MLA_Q_LORA + MLA_KV_LORA:])
    kr_ref[...] = kr[:, :MLA_ROPE]
    qf = _dot(c_q.astype(BF16), wuq_ref[...]) * (MLA_SCALE * LOG2E)
    ckv_b = c_kv.astype(BF16)
    if not absorb:
        kn = _dot(ckv_b, wuk_ref[...])
        v_ref[...] = _dot(ckv_b, wuv_ref[...]).astype(BF16)
        kr_b = kr.astype(BF16)
    for hd in range(MLA_HEADS):
        b0 = hd * 256
        qn = qf[:, b0:b0 + 128]
        qr = rope(qf[:, b0 + 128:b0 + 256])
        if absorb:
            qlat_ref[:, b0:b0 + 256] = _dot(qn.astype(BF16), wukt_ref[hd]).astype(BF16)
            qrope_ref[:, hd * 128:(hd + 1) * 128] = qr.astype(BF16)
        else:
            q_ref[:, b0:b0 + 128] = qn.astype(BF16)
            q_ref[:, b0 + 128:b0 + 256] = qr.astype(BF16)
            k_ref[:, b0:b0 + 128] = kn[:, hd * 128:(hd + 1) * 128].astype(BF16)
            k_ref[:, b0 + 128:b0 + 256] = kr_b


def _mla_proj(x3, mods3, norm_g, w, cos_a, sin_b, nb, tt, absorb):
    n_b, n_t, d = x3.shape
    m = nb * tt
    n_tt = n_t // tt
    rows = n_b * n_t
    grid = (n_b // nb, n_tt)
    x_spec = pl.BlockSpec((nb, tt, d), lambda i, j: (i, j, 0))

    def mod_spec(col):
        return pl.BlockSpec((nb, 1, d), lambda i, j, col=col: (i, 0, col))

    def const_spec(shape):
        return pl.BlockSpec(shape, lambda i, j: (0,) * len(shape))

    def row_spec(width):
        return pl.BlockSpec((m, width), lambda i, j: (i * n_tt + j, 0))

    tab_spec = pl.BlockSpec((m, LANES), lambda i, j: (j, 0))
    ops = [x3, mods3, mods3, norm_g.reshape(1, 1, d), w["w_in"], w["g_q"], w["g_kv"], w["w_uq"],
           cos_a, sin_b]
    specs = [x_spec, mod_spec(3), mod_spec(4), const_spec((1, 1, d)), const_spec(w["w_in"].shape),
             const_spec(w["g_q"].shape), const_spec(w["g_kv"].shape), const_spec(w["w_uq"].shape),
             tab_spec, tab_spec]
    if absorb:
        ops.append(w["w_ukt"])
        specs.append(const_spec(w["w_ukt"].shape))
        widths = [(MLA_HEADS * 256, BF16), (MLA_HEADS * 128, BF16)]
    else:
        ops += [w["w_uk"], w["w_uv"]]
        specs += [const_spec(w["w_uk"].shape), const_spec(w["w_uv"].shape)]
        widths = [(MLA_HEADS * 256, BF16), (MLA_HEADS * 256, BF16), (MLA_HEADS * MLA_V, BF16)]
    widths += [(MLA_KV_LORA, F32), (MLA_ROPE, F32)]
    return pl.pallas_call(
        functools.partial(_mla_proj_kernel, absorb=absorb),
        grid=grid,
        in_specs=specs,
        out_specs=[row_spec(wd) for wd, _ in widths],
        out_shape=[jax.ShapeDtypeStruct((rows, wd), dt) for wd, dt in widths],
        compiler_params=_params("parallel", "parallel"),
        name="mla_proj",
    )(*ops)


def _online_softmax(s, m_prev, l_prev, row_bias=None):
    blocks = [s[:, b * LANES:(b + 1) * LANES] for b in range(s.shape[1] // LANES)]
    if row_bias is not None:
        blocks = [blk + row_bias for blk in blocks]
    mx = blocks[0]
    for blk in blocks[1:]:
        mx = jnp.maximum(mx, blk)
    m_new = jnp.maximum(m_prev, jnp.max(mx, axis=1, keepdims=True))
    alpha = jnp.exp2(m_prev - m_new)
    ps = [jnp.exp2(blk - m_new) for blk in blocks]
    psum = ps[0]
    for p in ps[1:]:
        psum = psum + p
    p_b = jnp.concatenate([p.astype(BF16) for p in ps], axis=1) if len(ps) > 1 else ps[0].astype(BF16)
    return m_new, alpha, alpha * l_prev + psum, p_b


def _pipelined(n, produce, consume):
    nxt = produce(0)
    for b in range(n):
        cur = nxt
        if b + 1 < n:
            nxt = produce(b + 1)
        consume(b, cur)


def _flash_kernel(*refs, units, outs, n_kv):
    q_ref = refs[0]
    kv = refs[1:1 + n_kv]
    o_ref = refs[1 + n_kv]
    m_ref, l_ref, acc_ref = refs[2 + n_kv:]
    i = pl.program_id(1)
    j = pl.program_id(2)
    tq = q_ref.shape[0]
    tk = kv[0].shape[0]
    j_last = ((i + 1) * tq - 1) // tk

    @pl.when(j == 0)
    def _():
        m_ref[...] = jnp.full(m_ref.shape, -jnp.inf, F32)
        l_ref[...] = jnp.zeros(l_ref.shape, F32)
        acc_ref[...] = jnp.zeros(acc_ref.shape, F32)

    def step(masked):
        if masked:
            keep = (i * tq + _iota((tq, tk), 0)) >= (j * tk + _iota((tq, tk), 1))

        def scores(u):
            qs, ki, ks, _, _ = units[u]
            s = _dot_nt(q_ref[:, qs], kv[ki][:, ks])
            return jnp.where(keep, s, -jnp.inf) if masked else s

        def update(u, s):
            _, _, _, vi, vs = units[u]
            m_new, alpha, l_new, p_b = _online_softmax(s, m_ref[u], l_ref[u])
            l_ref[u] = l_new
            acc_ref[u] = alpha * acc_ref[u] + _dot(p_b, kv[vi][:, vs])
            m_ref[u] = m_new

        _pipelined(len(units), scores, update)

    fully_visible = (j + 1) * tk <= i * tq

    @pl.when(fully_visible)
    def _():
        step(False)

    @pl.when(jnp.logical_and(jnp.logical_not(fully_visible), j <= j_last))
    def _():
        step(True)

    @pl.when(j == j_last)
    def _():
        for b, us in enumerate(outs):
            o = None
            for u in us:
                term = acc_ref[u] / jnp.sum(l_ref[u], axis=1, keepdims=True)
                o = term if o is None else o + term
            o_ref[:, b * LANES:(b + 1) * LANES] = o.astype(o_ref.dtype)


def _flash(q, kvs, units, outs, n_batch, seq, tq, tk):
    nq = seq // tq
    nk = seq // tk
    n_units = len(units)

    def q_map(b, i, j):
        return (b * nq + i, 0)

    def kv_map(b, i, j):
        return (b * nk + jnp.minimum(j, ((i + 1) * tq - 1) // tk), 0)

    out_w = LANES * len(outs)
    return pl.pallas_call(
        functools.partial(_flash_kernel, units=units, outs=outs, n_kv=len(kvs)),
        grid=(n_batch, nq, nk),
        in_specs=[pl.BlockSpec((tq, q.shape[1]), q_map)]
        + [pl.BlockSpec((tk, a.shape[1]), kv_map) for a in kvs],
        out_specs=pl.BlockSpec((tq, out_w), q_map),
        out_shape=jax.ShapeDtypeStruct((q.shape[0], out_w), BF16),
        scratch_shapes=[pltpu.VMEM((n_units, tq, LANES), F32), pltpu.VMEM((n_units, tq, LANES), F32),
                        pltpu.VMEM((n_units, tq, LANES), F32)],
        compiler_params=_params("parallel", "parallel", "arbitrary"),
        name="flash",
    )(q, *kvs)


def _page_stream(pt_ref, caches, bufs, sems, n_pg, chunk_of):
    seq = pl.program_id(0)
    c = pl.program_id(1)
    n_steps = pl.num_programs(1)

    def copies(s, chunk, slot):
        out = []
        for j in range(n_pg):
            page = pt_ref[s, chunk * n_pg + j]
            for a, (cache, buf) in enumerate(zip(caches, bufs)):
                out.append(pltpu.make_async_copy(cache.at[page], buf.at[slot, j], sems.at[a, slot]))
        return out

    def start(s, chunk, slot):
        for cp in copies(s, chunk, slot):
            cp.start()

    def wait(s, chunk, slot):
        for cp in copies(s, chunk, slot):
            cp.wait()

    def run(consume):
        @pl.when(jnp.logical_and(seq == 0, c == 0))
        def _():
            start(seq, chunk_of(c, 0), 0)

        start(seq, chunk_of(c, 1), 1)
        wait(seq, chunk_of(c, 0), 0)
        consume(0)
        nxt = seq * n_steps + c + 1
        nxt_seq = nxt // n_steps
        nxt_c = nxt % n_steps

        @pl.when(nxt < pl.num_programs(0) * n_steps)
        def _():
            start(nxt_seq, chunk_of(nxt_c, 0), 0)

        wait(seq, chunk_of(c, 1), 1)
        consume(1)

    return run


def _mla_paged_kernel(pt_ref, qlat_ref, qrope_ref, ckn_ref, krn_ref, ckv_hbm, krt_hbm, o_ref,
                      ckv_buf, kr_buf, sems, kbuf, krbuf, m_ref, l_ref, acc_ref, *, n_pg, n_sub):
    c = pl.program_id(1)
    rows = qlat_ref.shape[0]
    q_lat = qlat_ref[...]
    q_rope = qrope_ref[:, :MLA_ROPE]
    stream = _page_stream(pt_ref, (ckv_hbm, krt_hbm), (ckv_buf, kr_buf), sems, n_pg,
                          lambda step, half: 2 * step + half)

    @pl.when(c == 0)
    def _():
        ckn = ckn_ref[...]
        n_new = ckn.shape[0]
        s = _dot_nt(q_lat, ckn) + _dot_nt(q_rope, krn_ref[...])
        s = jnp.where(_iota((rows, n_new), 1) <= _iota((rows, n_new), 0) // MLA_HEADS, s, -jnp.inf)
        m_new, _, l_new, p_b = _online_softmax(
            s, jnp.full((rows, LANES), -jnp.inf, F32), jnp.zeros((rows, LANES), F32))
        m_ref[...] = m_new
        l_ref[...] = l_new
        acc_ref[...] = _dot(p_b, ckn)

    sub = n_pg // n_sub
    width = sub * PAGE_SIZE

    def consume(slot):
        def scores(b):
            for j in range(b * sub, (b + 1) * sub):
                kbuf[j * PAGE_SIZE:(j + 1) * PAGE_SIZE, :] = ckv_buf[slot, j].astype(BF16)
                krbuf[:, j * PAGE_SIZE:(j + 1) * PAGE_SIZE] = kr_buf[slot, j].astype(BF16)
            return (_dot_nt(q_lat, kbuf[b * width:(b + 1) * width, :])
                    + _dot(q_rope, krbuf[:, b * width:(b + 1) * width]))

        def update(b, s):
            m_new, alpha, l_new, p_b = _online_softmax(s, m_ref[...], l_ref[...])
            l_ref[...] = l_new
            acc_ref[...] = (jnp.concatenate([alpha] * (MLA_KV_LORA // LANES), axis=1) * acc_ref[...]
                            + _dot(p_b, kbuf[b * width:(b + 1) * width, :]))
            m_ref[...] = m_new

        _pipelined(n_sub, scores, update)

    stream(consume)

    @pl.when(c == pl.num_programs(1) - 1)
    def _():
        l_tot = jnp.sum(l_ref[...], axis=1, keepdims=True)
        o_ref[...] = (acc_ref[...] / l_tot).astype(o_ref.dtype)


def _mla_paged(page_table, q_lat, q_rope, ckv_new, kr_new, cache_ckv, cache_krt, n_pg, n_sub):
    n_seq, rows, _ = q_lat.shape
    n_steps = page_table.shape[1] // (2 * n_pg)

    def seq_spec(a):
        return pl.BlockSpec((None,) + a.shape[1:], lambda b, c, pt: (b, 0, 0))

    hbm_spec = pl.BlockSpec(memory_space=pl.ANY)
    grid_spec = pltpu.PrefetchScalarGridSpec(
        num_scalar_prefetch=1,
        grid=(n_seq, n_steps),
        in_specs=[seq_spec(q_lat), seq_spec(q_rope), seq_spec(ckv_new), seq_spec(kr_new),
                  hbm_spec, hbm_spec],
        out_specs=pl.BlockSpec((None, rows, MLA_KV_LORA), lambda b, c, pt: (b, 0, 0)),
        scratch_shapes=[pltpu.VMEM((2, n_pg, PAGE_SIZE, MLA_KV_LORA), F32),
                        pltpu.VMEM((2, n_pg, MLA_ROPE, PAGE_SIZE), F32),
                        pltpu.SemaphoreType.DMA((2, 2)),
                        pltpu.VMEM((n_pg * PAGE_SIZE, MLA_KV_LORA), BF16),
                        pltpu.VMEM((MLA_ROPE, n_pg * PAGE_SIZE), BF16),
                        pltpu.VMEM((rows, LANES), F32), pltpu.VMEM((rows, LANES), F32),
                        pltpu.VMEM((rows, MLA_KV_LORA), F32)],
    )
    return pl.pallas_call(
        functools.partial(_mla_paged_kernel, n_pg=n_pg, n_sub=n_sub),
        grid_spec=grid_spec,
        out_shape=jax.ShapeDtypeStruct((n_seq, rows, MLA_KV_LORA), BF16),
        compiler_params=_params("arbitrary", "arbitrary"),
        name="mla_paged",
    )(page_table, q_lat, q_rope, ckv_new, kr_new, cache_ckv, cache_krt)


def _mla_uv_kernel(o_ref, wuv_ref, out_ref):
    for hd in range(MLA_HEADS):
        out_ref[:, hd * MLA_V:(hd + 1) * MLA_V] = _dot(
            o_ref[:, hd * MLA_KV_LORA:(hd + 1) * MLA_KV_LORA], wuv_ref[hd]).astype(BF16)


def _mla_uv(o_lat2, w_uv3):
    rows = o_lat2.shape[0]
    return pl.pallas_call(
        _mla_uv_kernel,
        out_shape=jax.ShapeDtypeStruct((rows, MLA_HEADS * MLA_V), BF16),
        compiler_params=pltpu.CompilerParams(vmem_limit_bytes=VMEM_LIMIT_BYTES),
        name="mla_uv",
    )(o_lat2, w_uv3)


def _fox_proj_kernel(*refs, prompt):
    it = iter(refs)
    x_ref, sh_ref, sc_ref, g_ref, win_ref, bf_ref = (next(it) for _ in range(6))
    if prompt:
        sel_ref, selc_ref = next(it), next(it)
        qa_ref, klo_ref, khi_ref, vlo_ref, vhi_ref, k_ref, v_ref, lf_ref, carry_ref = (
            next(it) for _ in range(9))
    else:
        q_ref, k_ref, v_ref, lf_ref = (next(it) for _ in range(4))
    nb, tt, d = x_ref.shape
    m = nb * tt
    qw = FOX_HEADS * FOX_HEAD_DIM
    kw = FOX_KV_HEADS * FOX_HEAD_DIM
    _, h = _modulated(x_ref, sh_ref, sc_ref, g_ref)
    hb = h.reshape(m, d).astype(BF16)
    proj = _dot(hb, win_ref[...])
    q = proj[:, :qw] * (FOX_SCALE * LOG2E)
    k = proj[:, qw:qw + kw]
    v = proj[:, qw + kw:qw + 2 * kw]
    z = proj[:, qw + 2 * kw:] + bf_ref[...]
    logf = jnp.minimum(z, 0.0) - jnp.log(1.0 + jnp.exp(-jnp.abs(z)))
    k_ref[...] = k
    v_ref[...] = v
    lf_ref[...] = logf[:, :FOX_HEADS]
    if not prompt:
        q_ref[...] = q.astype(BF16)
        return

    @pl.when(pl.program_id(1) == 0)
    def _():
        carry_ref[...] = jnp.zeros(carry_ref.shape, F32)

    lane = _iota((m, LANES), 1)
    logf = jnp.where(lane < FOX_HEADS, logf, 0.0)
    tri = (_iota((m, m), 0) >= _iota((m, m), 1)).astype(BF16)
    f_hi, f_mid, f_lo = _split3(logf)
    cum = _dot(tri, f_hi) + _dot(tri, f_mid) + _dot(tri, f_lo) + carry_ref[...]
    carry_ref[...] = cum[m - 1:m, :]
    hi, mid, lo = _split3(cum * LOG2E)
    comb = (hi.astype(F32) + pltpu.roll(mid.astype(F32), FOX_HEADS, 1)
            + pltpu.roll(lo.astype(F32), 2 * FOX_HEADS, 1)).astype(BF16)
    aug = _dot(comb, sel_ref[...]) + selc_ref[...]
    n_pairs = FOX_HEADS // 2
    for i in range(n_pairs):
        qa_ref[:, i * 256:i * 256 + 128] = q[:, i * 128:(i + 1) * 128].astype(BF16)
        qa_ref[:, i * 256 + 128:(i + 1) * 256] = aug[:, i * 128:(i + 1) * 128].astype(BF16)
    low_half = lane < FOX_HEAD_DIM
    a0 = n_pairs * 128
    for grp in range(FOX_KV_HEADS):
        blk = (grp // 2) * 128
        for src, lo_ref, hi_ref, stride, has_aug in ((k, klo_ref, khi_ref, 256, True),
                                                     (v, vlo_ref, vhi_ref, 128, False)):
            same = src[:, blk:blk + 128]
            swapped = pltpu.roll(same, FOX_HEAD_DIM, 1)
            lo_src, hi_src = (same, swapped) if grp % 2 == 0 else (swapped, same)
            b0 = grp * stride
            lo_ref[:, b0:b0 + 128] = jnp.where(low_half, lo_src, 0.0).astype(BF16)
            hi_ref[:, b0:b0 + 128] = jnp.where(low_half, 0.0, hi_src).astype(BF16)
            if has_aug:
                lo_ref[:, b0 + 128:b0 + 256] = aug[:, a0 + grp * 128:a0 + (grp + 1) * 128].astype(BF16)
                hi_ref[:, b0 + 128:b0 + 256] = aug[:, a0 + 512 + grp * 128:
                                                   a0 + 512 + (grp + 1) * 128].astype(BF16)


def _fox_aug_selectors():
    n_pairs = FOX_HEADS // 2
    sel = np.zeros((LANES, (n_pairs + 2 * FOX_KV_HEADS) * LANES), np.float32)
    const = np.zeros((1, sel.shape[1]), np.float32)
    klo0 = n_pairs * LANES
    khi0 = klo0 + FOX_KV_HEADS * LANES
    for part in range(3):
        src = part * FOX_HEADS
        for i in range(n_pairs):
            sel[src + 2 * i, i * LANES + part] = 1.0
            sel[src + 2 * i + 1, i * LANES + 3 + part] = 1.0
            const[0, i * LANES + 6 + 3 * (i % 2) + part] = 1.0
        for g in range(FOX_KV_HEADS):
            const[0, klo0 + g * LANES + part] = 1.0
            const[0, khi0 + g * LANES + 3 + part] = 1.0
            sel[src + 4 * g, klo0 + g * LANES + 6 + part] = -1.0
            sel[src + 4 * g + 2, klo0 + g * LANES + 9 + part] = -1.0
            sel[src + 4 * g + 1, khi0 + g * LANES + 6 + part] = -1.0
            sel[src + 4 * g + 3, khi0 + g * LANES + 9 + part] = -1.0
    return jnp.asarray(sel, BF16), jnp.asarray(const, F32)


def _fox_proj(x3, mods3, norm_g, w_in, b_f, nb, tt, prompt):
    n_b, n_t, d = x3.shape
    m = nb * tt
    n_tt = n_t // tt
    rows = n_b * n_t
    x_spec = pl.BlockSpec((nb, tt, d), lambda i, j: (i, j, 0))

    def mod_spec(col):
        return pl.BlockSpec((nb, 1, d), lambda i, j, col=col: (i, 0, col))

    def const_spec(shape):
        return pl.BlockSpec(shape, lambda i, j: (0,) * len(shape))

    def row_spec(width):
        return pl.BlockSpec((m, width), lambda i, j: (i * n_tt + j, 0))

    kw = FOX_KV_HEADS * FOX_HEAD_DIM
    ops = [x3, mods3, mods3, norm_g.reshape(1, 1, d), w_in, b_f]
    specs = [x_spec, mod_spec(3), mod_spec(4), const_spec((1, 1, d)), const_spec(w_in.shape),
             const_spec(b_f.shape)]
    scratch = []
    if prompt:
        sel, selc = _fox_aug_selectors()
        ops += [sel, selc]
        specs += [const_spec(sel.shape), const_spec(selc.shape)]
        widths = [(FOX_HEADS // 2 * 256, BF16), (FOX_KV_HEADS * 256, BF16), (FOX_KV_HEADS * 256, BF16),
                  (FOX_KV_HEADS * 128, BF16), (FOX_KV_HEADS * 128, BF16)]
        scratch = [pltpu.VMEM((1, LANES), F32)]
    else:
        widths = [(FOX_HEADS * FOX_HEAD_DIM, BF16)]
    widths += [(kw, F32), (kw, F32), (FOX_HEADS, F32)]
    return pl.pallas_call(
        functools.partial(_fox_proj_kernel, prompt=prompt),
        grid=(n_b // nb, n_tt),
        in_specs=specs,
        out_specs=[row_spec(wd) for wd, _ in widths],
        out_shape=[jax.ShapeDtypeStruct((rows, wd), dt) for wd, dt in widths],
        scratch_shapes=scratch,
        compiler_params=_params("parallel", "arbitrary"),
        name="fox_proj",
    )(*ops)


def _fox_paged_kernel(pt_ref, q_ref, lfn_ref, kn_ref, vn_ref, kt_hbm, vt_hbm, lft_hbm, o_ref,
                      k_buf, v_buf, lf_buf, sems, kbuf, vbuf, m_ref, l_ref, acc_ref, run_ref, ncol_ref,
                      *, n_pg, n_sub, dec_seq):
    c = pl.program_id(1)
    n_steps = pl.num_programs(1)
    rows, width = q_ref.shape
    q = q_ref[...]
    stream = _page_stream(pt_ref, (kt_hbm, vt_hbm, lft_hbm), (k_buf, v_buf, lf_buf), sems, n_pg,
                          lambda step, half: 2 * (n_steps - 1 - step) + 1 - half)

    def add_head_rows(s, per_head):
        return jnp.concatenate([s[hd * dec_seq:(hd + 1) * dec_seq, :] + per_head[hd:hd + 1, :]
                                for hd in range(FOX_HEADS)], axis=0)

    @pl.when(c == 0)
    def _():
        lfn = lfn_ref[...]
        n_new = lfn.shape[0]
        row = _iota((n_new, LANES), 0)
        n_cum = jnp.zeros((n_new, LANES), F32)
        for t in range(dec_seq):
            n_cum = n_cum + jnp.where(row >= t, lfn[t:t + 1, :], 0.0)
        n_cum = n_cum * LOG2E
        eye = (_iota((LANES, LANES), 0) == _iota((LANES, LANES), 1)).astype(BF16)
        hi, mid, lo = _split3(n_cum)
        n_cum_t = _dot_nt(eye, hi) + _dot_nt(eye, mid) + _dot_nt(eye, lo)
        lane = _iota((dec_seq, LANES), 1)
        for hd in range(FOX_HEADS):
            col = jnp.sum(jnp.where(lane == hd, n_cum[:dec_seq, :], 0.0), axis=1, keepdims=True)
            ncol_ref[hd * dec_seq:(hd + 1) * dec_seq, :] = jnp.broadcast_to(col, (dec_seq, LANES))
        s = add_head_rows(_dot_nt(q, kn_ref[...]), -n_cum_t)
        s = jnp.where(_iota((rows, n_new), 1) <= _iota((rows, n_new), 0) % dec_seq, s, -jnp.inf)
        m_new, _, l_new, p_b = _online_softmax(
            s, jnp.full((rows, LANES), -jnp.inf, F32), jnp.zeros((rows, LANES), F32), ncol_ref[...])
        m_ref[...] = m_new
        l_ref[...] = l_new
        acc_ref[...] = _dot(p_b, vn_ref[...])
        run_ref[...] = jnp.zeros(run_ref.shape, F32)

    sub = n_pg // n_sub
    span = sub * PAGE_SIZE
    later = (_iota((LANES, LANES), 0) > _iota((LANES, LANES), 1)).astype(BF16)
    suffix_and_total = jnp.concatenate([later, jnp.ones((LANES, LANES), BF16)], axis=1)

    def consume(slot):
        lft = lf_buf[slot].reshape(n_pg * FOX_HEADS, PAGE_SIZE)
        both = _dot3(lft, suffix_and_total) * LOG2E
        run = run_ref[...]
        bias_pages = [None] * n_pg
        for j in reversed(range(n_pg)):
            bias_pages[j] = run + both[j * FOX_HEADS:(j + 1) * FOX_HEADS, :LANES]
            run = run + both[j * FOX_HEADS:(j + 1) * FOX_HEADS, LANES:]
        run_ref[...] = run
        ncol = ncol_ref[...]

        def scores(b):
            for j in range(b * sub, (b + 1) * sub):
                kbuf[:, j * PAGE_SIZE:(j + 1) * PAGE_SIZE] = k_buf[slot, j].astype(BF16)
                vbuf[:, j * PAGE_SIZE:(j + 1) * PAGE_SIZE] = v_buf[slot, j].astype(BF16)
            return _dot(q, kbuf[:, b * span:(b + 1) * span])

        def update(b, s):
            bias = jnp.concatenate(bias_pages[b * sub:(b + 1) * sub], axis=1)
            m_new, alpha, l_new, p_b = _online_softmax(add_head_rows(s, bias), m_ref[...], l_ref[...],
                                                       ncol)
            l_ref[...] = l_new
            acc_ref[...] = (jnp.concatenate([alpha] * (width // LANES), axis=1) * acc_ref[...]
                            + _dot_nt(p_b, vbuf[:, b * span:(b + 1) * span]))
            m_ref[...] = m_new

        _pipelined(n_sub, scores, update)

    stream(consume)

    @pl.when(c == n_steps - 1)
    def _():
        o_ref[...] = acc_ref[...] / jnp.sum(l_ref[...], axis=1, keepdims=True)


def _fox_paged(page_table, q_all, lf_new, k_new, v_new, cache_kt, cache_vt, cache_lft, n_pg, n_sub, dec_seq):
    n_seq, rows, width = q_all.shape
    n_steps = page_table.shape[1] // (2 * n_pg)

    def seq_spec(a):
        return pl.BlockSpec((None,) + a.shape[1:], lambda b, c, pt: (b, 0, 0))

    hbm_spec = pl.BlockSpec(memory_space=pl.ANY)
    grid_spec = pltpu.PrefetchScalarGridSpec(
        num_scalar_prefetch=1,
        grid=(n_seq, n_steps),
        in_specs=[seq_spec(q_all), seq_spec(lf_new), seq_spec(k_new), seq_spec(v_new),
                  hbm_spec, hbm_spec, hbm_spec],
        out_specs=pl.BlockSpec((None, rows, width), lambda b, c, pt: (b, 0, 0)),
        scratch_shapes=[pltpu.VMEM((2, n_pg, width, PAGE_SIZE), F32),
                        pltpu.VMEM((2, n_pg, width, PAGE_SIZE), F32),
                        pltpu.VMEM((2, n_pg, FOX_HEADS, PAGE_SIZE), F32),
                        pltpu.SemaphoreType.DMA((3, 2)),
                        pltpu.VMEM((width, n_pg * PAGE_SIZE), BF16),
                        pltpu.VMEM((width, n_pg * PAGE_SIZE), BF16),
                        pltpu.VMEM((rows, LANES), F32), pltpu.VMEM((rows, LANES), F32),
                        pltpu.VMEM((rows, width), F32),
                        pltpu.VMEM((FOX_HEADS, LANES), F32),
                        pltpu.VMEM((rows, LANES), F32)],
    )
    return pl.pallas_call(
        functools.partial(_fox_paged_kernel, n_pg=n_pg, n_sub=n_sub, dec_seq=dec_seq),
        grid_spec=grid_spec,
        out_shape=jax.ShapeDtypeStruct((n_seq, rows, width), F32),
        compiler_params=_params("arbitrary", "arbitrary"),
        name="fox_paged",
    )(page_table, q_all, lf_new, k_new, v_new, cache_kt, cache_vt, cache_lft)


def _rope_tables(pos):
    half = MLA_ROPE // 2
    inv_freq = ROPE_THETA ** (-jnp.arange(half, dtype=F32) / half)
    ang = pos.astype(F32)[:, None] * inv_freq[None, :]
    cos, sin, zero = jnp.cos(ang), jnp.sin(ang), jnp.zeros_like(ang)
    return (jnp.concatenate([cos, cos, zero, zero], axis=1),
            jnp.concatenate([-sin, sin, zero, zero], axis=1))


def _mla_weights(w_in, g_q, w_uq, g_kv, w_uk, w_uv, w_o):
    lat = MLA_Q_LORA + MLA_KV_LORA
    kr_cols = np.concatenate([np.arange(lat, lat + MLA_ROPE)] * 2)
    in_cols = np.concatenate([np.arange(lat), kr_cols])
    uq_cols = []
    for hd in range(MLA_HEADS):
        b0 = hd * (MLA_NOPE + MLA_ROPE)
        rope_cols = np.arange(b0 + MLA_NOPE, b0 + MLA_NOPE + MLA_ROPE)
        uq_cols += [np.arange(b0, b0 + MLA_NOPE), rope_cols, rope_cols]
    return dict(
        w_in=w_in[:, in_cols].astype(BF16),
        g_q=g_q.reshape(1, -1), g_kv=g_kv.reshape(1, -1),
        w_uq=w_uq[:, np.concatenate(uq_cols)].astype(BF16),
        w_uk=w_uk.reshape(MLA_KV_LORA, MLA_HEADS * MLA_NOPE).astype(BF16),
        w_uv=w_uv.reshape(MLA_KV_LORA, MLA_HEADS * MLA_V).astype(BF16),
        w_ukt=jnp.transpose(w_uk, (1, 2, 0)).astype(BF16),
        w_uv3=jnp.transpose(w_uv, (1, 0, 2)).astype(BF16),
        w_o=w_o.astype(BF16),
    )


def _pad_rows(a, rows):
    return jnp.pad(a, ((0, 0), (0, rows - a.shape[1])) + ((0, 0),) * (a.ndim - 2))


def _pages_per_step(n_pages, want):
    n = min(want, n_pages)
    while n_pages % n:
        n -= 1
    return n


def kernel(x_prompt, x_sample, c_prompt, c_sample, cache_mla_ckv, cache_mla_krope, cache_fox_k, cache_fox_v, cache_fox_logf, page_table, ada_w, ada_b, norm_g, ffn_w_gu, ffn_w_dn, final_g, mla_w_in, mla_g_q, mla_w_uq, mla_g_kv, mla_w_uk, mla_w_uv, mla_w_o, fox_w_in, fox_b_f, fox_w_o):
    n_p, seq, d = x_prompt.shape
    n_s, dec_seq, _ = x_sample.shape
    n_pages = page_table.shape[1]
    past_len = n_pages * PAGE_SIZE
    tile = min(512, seq)
    tk = min(1024, seq)
    n_pad = LANES

    mods = _adaln(jnp.concatenate([c_prompt, c_sample], axis=0), ada_w, ada_b)
    w_gu = ffn_w_gu.astype(BF16)
    w_dn = ffn_w_dn.astype(BF16)
    xp, xs = x_prompt, x_sample

    mp = mods[0, :n_p].reshape(n_p, 1, -1)
    ms = mods[0, n_p:].reshape(n_s, 1, -1)
    w = _mla_weights(mla_w_in[0], mla_g_q[0], mla_w_uq[0], mla_g_kv[0], mla_w_uk[0], mla_w_uv[0],
                     mla_w_o[0])
    xp = _ffn(xp, mp, 0, norm_g[0, 0], w_gu[0, 0], w_dn[0, 0], 1, tile)
    xs = _ffn(xs, ms, 0, norm_g[0, 0], w_gu[0, 0], w_dn[0, 0], n_s, dec_seq)

    cos_p, sin_p = _rope_tables(jnp.arange(seq, dtype=jnp.int32))
    q, k, v, p_ckv, p_kr = _mla_proj(xp, mp, norm_g[0, 1], w, cos_p, sin_p, 1, tile, absorb=False)
    units = [(slice(h * 256, (h + 1) * 256), 0, slice(h * 256, (h + 1) * 256), 1,
              slice(h * 128, (h + 1) * 128)) for h in range(MLA_HEADS)]
    o_p = _flash(q, [k, v], units, [[h] for h in range(MLA_HEADS)], n_p, seq, tile, tk)

    cos_s, sin_s = _rope_tables(past_len + jnp.arange(dec_seq, dtype=jnp.int32))
    cos_s, sin_s = jnp.tile(cos_s, (n_s, 1)), jnp.tile(sin_s, (n_s, 1))
    q_lat, q_rope, s_ckv, s_kr = _mla_proj(xs, ms, norm_g[0, 1], w, cos_s, sin_s, n_s, dec_seq,
                                           absorb=True)
    rows = dec_seq * MLA_HEADS
    n_pg = _pages_per_step(n_pages // 2, 32)
    n_sub = 4 if n_pg % 4 == 0 else 1
    o_lat = _mla_paged(
        page_table,
        q_lat.reshape(n_s, rows, MLA_KV_LORA), q_rope.reshape(n_s, rows, LANES),
        _pad_rows(s_ckv.reshape(n_s, dec_seq, MLA_KV_LORA), n_pad).astype(BF16),
        _pad_rows(s_kr.reshape(n_s, dec_seq, MLA_ROPE), n_pad).astype(BF16),
        cache_mla_ckv[0], jnp.swapaxes(cache_mla_krope[0], 1, 2), n_pg, n_sub)
    o_s = _mla_uv(o_lat.reshape(n_s * dec_seq, MLA_HEADS * MLA_KV_LORA), w["w_uv3"])

    xp = _ffn(xp, mp, 2, norm_g[0, 2], w_gu[0, 1], w_dn[0, 1], 1, tile, mix=(o_p, w["w_o"]))
    xs = _ffn(xs, ms, 2, norm_g[0, 2], w_gu[0, 1], w_dn[0, 1], n_s, dec_seq, mix=(o_s, w["w_o"]))

    mp = mods[1, :n_p].reshape(n_p, 1, -1)
    ms = mods[1, n_p:].reshape(n_s, 1, -1)
    qw = FOX_HEADS * FOX_HEAD_DIM
    kw = FOX_KV_HEADS * FOX_HEAD_DIM
    fw_in = jnp.pad(fox_w_in[0], ((0, 0), (0, LANES - FOX_HEADS))).astype(BF16)
    fb = jnp.pad(fox_b_f[0], (0, LANES - FOX_HEADS)).reshape(1, LANES)
    fw_o = fox_w_o[0].astype(BF16)
    xp = _ffn(xp, mp, 0, norm_g[1, 0], w_gu[1, 0], w_dn[1, 0], 1, tile)
    xs = _ffn(xs, ms, 0, norm_g[1, 0], w_gu[1, 0], w_dn[1, 0], n_s, dec_seq)

    qa, klo, khi, vlo, vhi, p_k, p_v, p_lf = _fox_proj(xp, mp, norm_g[1, 1], fw_in, fb, 1, tile,
                                                       prompt=True)
    units = []
    for h in range(FOX_HEADS):
        i, par, g = h // 2, h % 2, h // (FOX_HEADS // FOX_KV_HEADS)
        units.append((slice(i * 256, (i + 1) * 256), par, slice(g * 256, (g + 1) * 256), 2 + par,
                      slice(g * 128, (g + 1) * 128)))
    o_p = _flash(qa, [klo, khi, vlo, vhi], units, [[2 * i, 2 * i + 1] for i in range(FOX_HEADS // 2)],
                 n_p, seq, tile, tk)

    q_s, s_k, s_v, s_lf = _fox_proj(xs, ms, norm_g[1, 1], fw_in, fb, n_s, dec_seq, prompt=False)
    q4 = jnp.transpose(q_s.reshape(n_s, dec_seq, FOX_HEADS, FOX_HEAD_DIM), (0, 2, 1, 3))
    grp_of_head = jnp.arange(FOX_HEADS) // (FOX_HEADS // FOX_KV_HEADS)
    onehot = (grp_of_head[:, None] == jnp.arange(FOX_KV_HEADS)[None, :]).astype(BF16)
    q_all = (q4[:, :, :, None, :] * onehot[None, :, None, :, None]).reshape(
        n_s, FOX_HEADS * dec_seq, kw)
    lf_new = jnp.pad(s_lf.reshape(n_s, dec_seq, FOX_HEADS),
                     ((0, 0), (0, n_pad - dec_seq), (0, LANES - FOX_HEADS)))
    cache_lft = jnp.swapaxes(cache_fox_logf[0], 1, 2)
    cache_kt = jnp.transpose(cache_fox_k[0], (0, 2, 3, 1)).reshape(-1, kw, PAGE_SIZE)
    cache_vt = jnp.transpose(cache_fox_v[0], (0, 2, 3, 1)).reshape(-1, kw, PAGE_SIZE)
    o_all = _fox_paged(
        page_table, q_all, lf_new,
        _pad_rows(s_k.reshape(n_s, dec_seq, kw), n_pad).astype(BF16),
        _pad_rows(s_v.reshape(n_s, dec_seq, kw), n_pad).astype(BF16),
        cache_kt, cache_vt, cache_lft, n_pg, n_sub, dec_seq)
    o5 = o_all.reshape(n_s, FOX_HEADS, dec_seq, FOX_KV_HEADS, FOX_HEAD_DIM)
    o_sel = jnp.take_along_axis(o5, grp_of_head[None, :, None, None, None], axis=3)[:, :, :, 0, :]
    o_s = jnp.transpose(o_sel, (0, 2, 1, 3)).reshape(n_s * dec_seq, qw).astype(BF16)

    xp = _ffn(xp, mp, 2, norm_g[1, 2], w_gu[1, 1], w_dn[1, 1], 1, tile, mix=(o_p, fw_o),
              final_g=final_g)
    xs = _ffn(xs, ms, 2, norm_g[1, 2], w_gu[1, 1], w_dn[1, 1], n_s, dec_seq, mix=(o_s, fw_o),
              final_g=final_g)

    def prompt_state(a, *tail):
        return a.reshape(1, n_p, seq, *tail)

    def sample_state(a, *tail):
        return a.reshape(1, n_s, dec_seq, *tail)

    return (xp, xs,
            prompt_state(p_ckv, MLA_KV_LORA), prompt_state(p_kr, MLA_ROPE),
            prompt_state(p_k, FOX_KV_HEADS, FOX_HEAD_DIM), prompt_state(p_v, FOX_KV_HEADS, FOX_HEAD_DIM),
            prompt_state(p_lf, FOX_HEADS),
            sample_state(s_ckv, MLA_KV_LORA), sample_state(s_kr, MLA_ROPE),
            sample_state(s_k, FOX_KV_HEADS, FOX_HEAD_DIM), sample_state(s_v, FOX_KV_HEADS, FOX_HEAD_DIM),
            sample_state(s_lf, FOX_HEADS))
```

```python
import functools
import math

import numpy as np
import jax
import jax.numpy as jnp
from jax import lax
from jax.experimental import pallas as pl
from jax.experimental.pallas import tpu as pltpu

F32 = jnp.float32
BF16 = jnp.bfloat16

LANES = 128
NORM_EPS = 1e-6
PAGE_SIZE = 128
D_FF = 2816
MLA_HEADS = 8
MLA_NOPE = 128
MLA_ROPE = 64
MLA_V = 128
MLA_Q_LORA = 512
MLA_KV_LORA = 256
ROPE_THETA = 10000.0
MLA_SCALE = 1.0 / math.sqrt(MLA_NOPE + MLA_ROPE)
FOX_HEADS = 16
FOX_KV_HEADS = 4
FOX_HEAD_DIM = 64
FOX_SCALE = 1.0 / math.sqrt(FOX_HEAD_DIM)
LOG2E = math.log2(math.e)
VMEM_LIMIT_BYTES = 56 * 1024 * 1024


def _params(*sem):
    return pltpu.CompilerParams(dimension_semantics=sem, vmem_limit_bytes=VMEM_LIMIT_BYTES)


def _dot(a, b):
    return jnp.dot(a, b, preferred_element_type=F32)


def _dot_nt(a, b):
    return lax.dot_general(a, b, (((1,), (1,)), ((), ())), preferred_element_type=F32)


def _sigmoid(x):
    return 1.0 / (1.0 + jnp.exp(-x))


def _rms(x, g):
    return x * lax.rsqrt(jnp.mean(x * x, axis=-1, keepdims=True) + NORM_EPS) * g


def _split3(x):
    hi = x.astype(BF16)
    r = x - hi.astype(F32)
    mid = r.astype(BF16)
    lo = (r - mid.astype(F32)).astype(BF16)
    return hi, mid, lo


def _dot3(x, m):
    hi, mid, lo = _split3(x)
    return _dot(hi, m) + _dot(mid, m) + _dot(lo, m)


def _iota(shape, dim):
    return lax.broadcasted_iota(jnp.int32, shape, dim)


def _modulated(x_ref, sh_ref, sc_ref, g_ref):
    x = x_ref[...]
    return x, _rms(x, g_ref[...]) * (1.0 + sc_ref[...]) + sh_ref[...]


def _adaln_kernel(c_ref, w_ref, b_ref, o_ref):
    c = c_ref[...]
    a = (c * _sigmoid(c)).astype(BF16)
    o_ref[0] = _dot(a, w_ref[0].astype(BF16)) + b_ref[0]


def _adaln(c_all, ada_w, ada_b):
    n_layers, d, n = ada_w.shape
    rows = c_all.shape[0]
    tn = 1536 if n % 1536 == 0 else n
    return pl.pallas_call(
        _adaln_kernel,
        grid=(n_layers, n // tn),
        in_specs=[
            pl.BlockSpec((rows, d), lambda l, j: (0, 0)),
            pl.BlockSpec((1, d, tn), lambda l, j: (l, 0, j)),
            pl.BlockSpec((1, 1, tn), lambda l, j: (l, 0, j)),
        ],
        out_specs=pl.BlockSpec((1, rows, tn), lambda l, j: (l, 0, j)),
        out_shape=jax.ShapeDtypeStruct((n_layers, rows, n), F32),
        compiler_params=_params("parallel", "parallel"),
        name="adaln",
    )(c_all, ada_w, ada_b.reshape(n_layers, 1, n))


def _ffn_kernel(*refs, has_mix, final, n_chunks, chunk):
    it = iter(refs)
    x_ref, sh_ref, sc_ref, gt_ref, g_ref, wgu_ref, wdn_ref = (next(it) for _ in range(7))
    if has_mix:
        o_ref, wo_ref, gm_ref = next(it), next(it), next(it)
    if final:
        fg_ref = next(it)
    out_ref, acc_ref = next(it), next(it)
    nb, tt, d = x_ref.shape
    m = nb * tt
    x = x_ref[...]
    if has_mix:
        x = x + gm_ref[...] * _dot(o_ref[...], wo_ref[...]).reshape(nb, tt, d)
    h = _rms(x, g_ref[...]) * (1.0 + sc_ref[...]) + sh_ref[...]
    hb = h.reshape(m, d).astype(BF16)
    d_ff = wdn_ref.shape[0]
    for c in range(n_chunks):
        lo = c * chunk
        gate = _dot(hb, wgu_ref[:, lo:lo + chunk])
        up = _dot(hb, wgu_ref[:, d_ff + lo:d_ff + lo + chunk])
        act = (gate * _sigmoid(gate) * up).astype(BF16)
        y = _dot(act, wdn_ref[lo:lo + chunk, :])
        if c == 0:
            acc_ref[...] = y
        else:
            acc_ref[...] += y
    y = x + (0.5 * gt_ref[...]) * acc_ref[...].reshape(nb, tt, d)
    if final:
        y = _rms(y, fg_ref[...])
    out_ref[...] = y


def _ffn(x3, mods3, sub, norm_g, w_gu, w_dn, nb, tt, mix=None, final_g=None):
    n_b, n_t, d = x3.shape
    d_ff = w_dn.shape[0]
    chunk = 256
    grid = (n_b // nb, n_t // tt)
    m = nb * tt
    x_spec = pl.BlockSpec((nb, tt, d), lambda i, j: (i, j, 0))

    def mod_spec(col):
        return pl.BlockSpec((nb, 1, d), lambda i, j, col=col: (i, 0, col))

    def const_spec(shape):
        return pl.BlockSpec(shape, lambda i, j: (0,) * len(shape), pipeline_mode=pl.Buffered(1))

    ops = [x3, mods3, mods3, mods3, norm_g.reshape(1, 1, d), w_gu, w_dn]
    specs = [x_spec, mod_spec(3 * sub), mod_spec(3 * sub + 1), mod_spec(3 * sub + 2),
             const_spec((1, 1, d)), const_spec(w_gu.shape), const_spec(w_dn.shape)]
    if mix is not None:
        o2, w_o = mix
        n_tt = n_t // tt
        ops += [o2, w_o, mods3]
        specs += [pl.BlockSpec((m, o2.shape[1]), lambda i, j: (i * n_tt + j, 0)),
                  const_spec(w_o.shape), mod_spec(3 * 1 + 2)]
    if final_g is not None:
        ops.append(final_g.reshape(1, 1, d))
        specs.append(const_spec((1, 1, d)))
    return pl.pallas_call(
        functools.partial(_ffn_kernel, has_mix=mix is not None, final=final_g is not None,
                          n_chunks=d_ff // chunk, chunk=chunk),
        grid=grid,
        in_specs=specs,
        out_specs=x_spec,
        out_shape=jax.ShapeDtypeStruct(x3.shape, F32),
        scratch_shapes=[pltpu.VMEM((m, d), F32)],
        compiler_params=_params("parallel", "parallel"),
        name="ffn",
    )(*ops)


def _mla_proj_kernel(*refs, absorb):
    it = iter(refs)
    x_ref, sh_ref, sc_ref, g_ref, win_ref, gq_ref, gkv_ref, wuq_ref, cos_ref, sin_ref = (
        next(it) for _ in range(10))
    if absorb:
        wukt_ref = next(it)
        qlat_ref, qrope_ref, ckv_ref, kr_ref = (next(it) for _ in range(4))
    else:
        wuk_ref, wuv_ref = next(it), next(it)
        q_ref, k_ref, v_ref, ckv_ref, kr_ref = (next(it) for _ in range(5))
    nb, tt, d = x_ref.shape
    m = nb * tt
    _, h = _modulated(x_ref, sh_ref, sc_ref, g_ref)
    hb = h.reshape(m, d).astype(BF16)
    proj = _dot(hb, win_ref[...])
    c_q = _rms(proj[:, :MLA_Q_LORA], gq_ref[...])
    c_kv = _rms(proj[:, MLA_Q_LORA:MLA_Q_LORA + MLA_KV_LORA], gkv_ref[...])
    ckv_ref[...] = c_kv
    cos_a = cos_ref[...]
    sin_b = sin_ref[...]

    def rope(z):
        return z * cos_a + pltpu.roll(z, 32, 1) * sin_b

    kr = rope(proj[:, MLA_Q_LORA + MLA_KV_LORA:])
    kr_ref[...] = kr[:, :MLA_ROPE]
    qf = _dot(c_q.astype(BF16), wuq_ref[...]) * (MLA_SCALE * LOG2E)
    ckv_b = c_kv.astype(BF16)
    if not absorb:
        kn = _dot(ckv_b, wuk_ref[...])
        v_ref[...] = _dot(ckv_b, wuv_ref[...]).astype(BF16)
        kr_b = kr.astype(BF16)
    for hd in range(MLA_HEADS):
        b0 = hd * 256
        qn = qf[:, b0:b0 + 128]
        qr = rope(qf[:, b0 + 128:b0 + 256])
        if absorb:
            qlat_ref[:, b0:b0 + 256] = _dot(qn.astype(BF16), wukt_ref[hd]).astype(BF16)
            qrope_ref[:, hd * 128:(hd + 1) * 128] = qr.astype(BF16)
        else:
            q_ref[:, b0:b0 + 128] = qn.astype(BF16)
            q_ref[:, b0 + 128:b0 + 256] = qr.astype(BF16)
            k_ref[:, b0:b0 + 128] = kn[:, hd * 128:(hd + 1) * 128].astype(BF16)
            k_ref[:, b0 + 128:b0 + 256] = kr_b


def _mla_proj(x3, mods3, norm_g, w, cos_a, sin_b, nb, tt, absorb):
    n_b, n_t, d = x3.shape
    m = nb * tt
    n_tt = n_t // tt
    rows = n_b * n_t
    grid = (n_b // nb, n_tt)
    x_spec = pl.BlockSpec((nb, tt, d), lambda i, j: (i, j, 0))

    def mod_spec(col):
        return pl.BlockSpec((nb, 1, d), lambda i, j, col=col: (i, 0, col))

    def const_spec(shape):
        return pl.BlockSpec(shape, lambda i, j: (0,) * len(shape))

    def row_spec(width):
        return pl.BlockSpec((m, width), lambda i, j: (i * n_tt + j, 0))

    tab_spec = pl.BlockSpec((m, LANES), lambda i, j: (j, 0))
    ops = [x3, mods3, mods3, norm_g.reshape(1, 1, d), w["w_in"], w["g_q"], w["g_kv"], w["w_uq"],
           cos_a, sin_b]
    specs = [x_spec, mod_spec(3), mod_spec(4), const_spec((1, 1, d)), const_spec(w["w_in"].shape),
             const_spec(w["g_q"].shape), const_spec(w["g_kv"].shape), const_spec(w["w_uq"].shape),
             tab_spec, tab_spec]
    if absorb:
        ops.append(w["w_ukt"])
        specs.append(const_spec(w["w_ukt"].shape))
        widths = [(MLA_HEADS * 256, BF16), (MLA_HEADS * 128, BF16)]
    else:
        ops += [w["w_uk"], w["w_uv"]]
        specs += [const_spec(w["w_uk"].shape), const_spec(w["w_uv"].shape)]
        widths = [(MLA_HEADS * 256, BF16), (MLA_HEADS * 256, BF16), (MLA_HEADS * MLA_V, BF16)]
    widths += [(MLA_KV_LORA, F32), (MLA_ROPE, F32)]
    return pl.pallas_call(
        functools.partial(_mla_proj_kernel, absorb=absorb),
        grid=grid,
        in_specs=specs,
        out_specs=[row_spec(wd) for wd, _ in widths],
        out_shape=[jax.ShapeDtypeStruct((rows, wd), dt) for wd, dt in widths],
        compiler_params=_params("parallel", "parallel"),
        name="mla_proj",
    )(*ops)


def _online_softmax(s, m_prev, l_prev, row_bias=None):
    blocks = [s[:, b * LANES:(b + 1) * LANES] for b in range(s.shape[1] // LANES)]
    if row_bias is not None:
        blocks = [blk + row_bias for blk in blocks]
    mx = blocks[0]
    for blk in blocks[1:]:
        mx = jnp.maximum(mx, blk)
    m_new = jnp.maximum(m_prev, jnp.max(mx, axis=1, keepdims=True))
    alpha = jnp.exp2(m_prev - m_new)
    ps = [jnp.exp2(blk - m_new) for blk in blocks]
    psum = ps[0]
    for p in ps[1:]:
        psum = psum + p
    p_b = jnp.concatenate([p.astype(BF16) for p in ps], axis=1) if len(ps) > 1 else ps[0].astype(BF16)
    return m_new, alpha, alpha * l_prev + psum, p_b


def _pipelined(n, produce, consume):
    nxt = produce(0)
    for b in range(n):
        cur = nxt
        if b + 1 < n:
            nxt = produce(b + 1)
        consume(b, cur)


def _flash_kernel(*refs, units, outs, n_kv):
    q_ref = refs[0]
    kv = refs[1:1 + n_kv]
    o_ref = refs[1 + n_kv]
    m_ref, l_ref, acc_ref = refs[2 + n_kv:]
    i = pl.program_id(1)
    j = pl.program_id(2)
    tq = q_ref.shape[0]
    tk = kv[0].shape[0]
    j_last = ((i + 1) * tq - 1) // tk

    @pl.when(j == 0)
    def _():
        m_ref[...] = jnp.full(m_ref.shape, -jnp.inf, F32)
        l_ref[...] = jnp.zeros(l_ref.shape, F32)
        acc_ref[...] = jnp.zeros(acc_ref.shape, F32)

    def step(masked, n_keys):
        if masked:
            keep = (i * tq + _iota((tq, n_keys), 0)) >= (j * tk + _iota((tq, n_keys), 1))

        def scores(u):
            qs, ki, ks, _, _ = units[u]
            s = _dot_nt(q_ref[:, qs], kv[ki][:n_keys, ks])
            return jnp.where(keep, s, -jnp.inf) if masked else s

        def update(u, s):
            _, _, _, vi, vs = units[u]
            m_new, alpha, l_new, p_b = _online_softmax(s, m_ref[u], l_ref[u])
            l_ref[u] = l_new
            acc_ref[u] = alpha * acc_ref[u] + _dot(p_b, kv[vi][:n_keys, vs])
            m_ref[u] = m_new

        _pipelined(len(units), scores, update)

    fully_visible = (j + 1) * tk <= i * tq
    partly_visible = jnp.logical_and(jnp.logical_not(fully_visible), j <= j_last)
    half = tk // 2
    first_half_only = j * tk + half > (i + 1) * tq - 1 if half % LANES == 0 else False

    @pl.when(fully_visible)
    def _():
        step(False, tk)

    if half % LANES == 0:
        @pl.when(jnp.logical_and(partly_visible, first_half_only))
        def _():
            step(True, half)

    @pl.when(jnp.logical_and(partly_visible, jnp.logical_not(first_half_only)))
    def _():
        step(True, tk)

    @pl.when(j == j_last)
    def _():
        for b, us in enumerate(outs):
            o = None
            for u in us:
                term = acc_ref[u] / jnp.sum(l_ref[u], axis=1, keepdims=True)
                o = term if o is None else o + term
            o_ref[:, b * LANES:(b + 1) * LANES] = o.astype(o_ref.dtype)


def _flash(q, kvs, units, outs, n_batch, seq, tq, tk):
    nq = seq // tq
    nk = seq // tk
    n_units = len(units)

    def q_map(b, i, j):
        return (b * nq + i, 0)

    def kv_map(b, i, j):
        return (b * nk + jnp.minimum(j, ((i + 1) * tq - 1) // tk), 0)

    out_w = LANES * len(outs)
    return pl.pallas_call(
        functools.partial(_flash_kernel, units=units, outs=outs, n_kv=len(kvs)),
        grid=(n_batch, nq, nk),
        in_specs=[pl.BlockSpec((tq, q.shape[1]), q_map)]
        + [pl.BlockSpec((tk, a.shape[1]), kv_map) for a in kvs],
        out_specs=pl.BlockSpec((tq, out_w), q_map),
        out_shape=jax.ShapeDtypeStruct((q.shape[0], out_w), BF16),
        scratch_shapes=[pltpu.VMEM((n_units, tq, LANES), F32), pltpu.VMEM((n_units, tq, LANES), F32),
                        pltpu.VMEM((n_units, tq, LANES), F32)],
        compiler_params=_params("parallel", "parallel", "arbitrary"),
        name="flash",
    )(q, *kvs)


def _page_stream(pt_ref, caches, bufs, sems, n_pg, chunk_of):
    seq = pl.program_id(0)
    c = pl.program_id(1)
    n_steps = pl.num_programs(1)

    def copies(s, chunk, slot):
        out = []
        for j in range(n_pg):
            page = pt_ref[s, chunk * n_pg + j]
            for a, (cache, buf) in enumerate(zip(caches, bufs)):
                out.append(pltpu.make_async_copy(cache.at[page], buf.at[slot, j], sems.at[a, slot]))
        return out

    def start(s, chunk, slot):
        for n, cp in enumerate(copies(s, chunk, slot)):
            cp.start(priority=(n // len(caches)) % 2)

    def wait(s, chunk, slot):
        for cp in copies(s, chunk, slot):
            cp.wait()

    def run(consume):
        @pl.when(jnp.logical_and(seq == 0, c == 0))
        def _():
            start(seq, chunk_of(c, 0), 0)

        start(seq, chunk_of(c, 1), 1)
        wait(seq, chunk_of(c, 0), 0)
        consume(0)
        nxt = seq * n_steps + c + 1
        nxt_seq = nxt // n_steps
        nxt_c = nxt % n_steps

        @pl.when(nxt < pl.num_programs(0) * n_steps)
        def _():
            start(nxt_seq, chunk_of(nxt_c, 0), 0)

        wait(seq, chunk_of(c, 1), 1)
        consume(1)

    return run


def _mla_paged_kernel(pt_ref, qlat_ref, qrope_ref, ckn_ref, krn_ref, ckv_hbm, krt_hbm, o_ref,
                      ckv_buf, kr_buf, sems, kbuf, krbuf, m_ref, l_ref, acc_ref, *, n_pg, n_sub):
    c = pl.program_id(1)
    rows = qlat_ref.shape[0]
    q_lat = qlat_ref[...]
    q_rope = qrope_ref[:, :MLA_ROPE]
    stream = _page_stream(pt_ref, (ckv_hbm, krt_hbm), (ckv_buf, kr_buf), sems, n_pg,
                          lambda step, half: 2 * step + half)

    @pl.when(c == 0)
    def _():
        ckn = ckn_ref[...]
        n_new = ckn.shape[0]
        s = _dot_nt(q_lat, ckn) + _dot_nt(q_rope, krn_ref[...])
        s = jnp.where(_iota((rows, n_new), 1) <= _iota((rows, n_new), 0) // MLA_HEADS, s, -jnp.inf)
        m_new, _, l_new, p_b = _online_softmax(
            s, jnp.full((rows, LANES), -jnp.inf, F32), jnp.zeros((rows, LANES), F32))
        m_ref[...] = m_new
        l_ref[...] = l_new
        acc_ref[...] = _dot(p_b, ckn)

    sub = n_pg // n_sub
    width = sub * PAGE_SIZE

    def consume(slot):
        def scores(b):
            for j in range(b * sub, (b + 1) * sub):
                kbuf[j * PAGE_SIZE:(j + 1) * PAGE_SIZE, :] = ckv_buf[slot, j].astype(BF16)
                krbuf[:, j * PAGE_SIZE:(j + 1) * PAGE_SIZE] = kr_buf[slot, j].astype(BF16)
            return (_dot_nt(q_lat, kbuf[b * width:(b + 1) * width, :])
                    + _dot(q_rope, krbuf[:, b * width:(b + 1) * width]))

        def update(b, s):
            m_new, alpha, l_new, p_b = _online_softmax(s, m_ref[...], l_ref[...])
            l_ref[...] = l_new
            acc_ref[...] = (jnp.concatenate([alpha] * (MLA_KV_LORA // LANES), axis=1) * acc_ref[...]
                            + _dot(p_b, kbuf[b * width:(b + 1) * width, :]))
            m_ref[...] = m_new

        _pipelined(n_sub, scores, update)

    stream(consume)

    @pl.when(c == pl.num_programs(1) - 1)
    def _():
        l_tot = jnp.sum(l_ref[...], axis=1, keepdims=True)
        o_ref[...] = (acc_ref[...] / l_tot).astype(o_ref.dtype)


def _mla_paged(page_table, q_lat, q_rope, ckv_new, kr_new, cache_ckv, cache_krt, n_pg, n_sub):
    n_seq, rows, _ = q_lat.shape
    n_steps = page_table.shape[1] // (2 * n_pg)

    def seq_spec(a):
        return pl.BlockSpec((None,) + a.shape[1:], lambda b, c, pt: (b, 0, 0))

    hbm_spec = pl.BlockSpec(memory_space=pl.ANY)
    grid_spec = pltpu.PrefetchScalarGridSpec(
        num_scalar_prefetch=1,
        grid=(n_seq, n_steps),
        in_specs=[seq_spec(q_lat), seq_spec(q_rope), seq_spec(ckv_new), seq_spec(kr_new),
                  hbm_spec, hbm_spec],
        out_specs=pl.BlockSpec((None, rows, MLA_KV_LORA), lambda b, c, pt: (b, 0, 0)),
        scratch_shapes=[pltpu.VMEM((2, n_pg, PAGE_SIZE, MLA_KV_LORA), F32),
                        pltpu.VMEM((2, n_pg, MLA_ROPE, PAGE_SIZE), F32),
                        pltpu.SemaphoreType.DMA((2, 2)),
                        pltpu.VMEM((n_pg * PAGE_SIZE, MLA_KV_LORA), BF16),
                        pltpu.VMEM((MLA_ROPE, n_pg * PAGE_SIZE), BF16),
                        pltpu.VMEM((rows, LANES), F32), pltpu.VMEM((rows, LANES), F32),
                        pltpu.VMEM((rows, MLA_KV_LORA), F32)],
    )
    return pl.pallas_call(
        functools.partial(_mla_paged_kernel, n_pg=n_pg, n_sub=n_sub),
        grid_spec=grid_spec,
        out_shape=jax.ShapeDtypeStruct((n_seq, rows, MLA_KV_LORA), BF16),
        compiler_params=_params("arbitrary", "arbitrary"),
        name="mla_paged",
    )(page_table, q_lat, q_rope, ckv_new, kr_new, cache_ckv, cache_krt)


def _mla_uv_kernel(o_ref, wuv_ref, out_ref):
    for hd in range(MLA_HEADS):
        out_ref[:, hd * MLA_V:(hd + 1) * MLA_V] = _dot(
            o_ref[:, hd * MLA_KV_LORA:(hd + 1) * MLA_KV_LORA], wuv_ref[hd]).astype(BF16)


def _mla_uv(o_lat2, w_uv3):
    rows = o_lat2.shape[0]
    return pl.pallas_call(
        _mla_uv_kernel,
        out_shape=jax.ShapeDtypeStruct((rows, MLA_HEADS * MLA_V), BF16),
        compiler_params=pltpu.CompilerParams(vmem_limit_bytes=VMEM_LIMIT_BYTES),
        name="mla_uv",
    )(o_lat2, w_uv3)


def _fox_proj_kernel(*refs, prompt):
    it = iter(refs)
    x_ref, sh_ref, sc_ref, g_ref, win_ref, bf_ref = (next(it) for _ in range(6))
    if prompt:
        sel_ref, selc_ref = next(it), next(it)
        qa_ref, klo_ref, khi_ref, vlo_ref, vhi_ref, k_ref, v_ref, lf_ref, carry_ref = (
            next(it) for _ in range(9))
    else:
        q_ref, k_ref, v_ref, lf_ref = (next(it) for _ in range(4))
    nb, tt, d = x_ref.shape
    m = nb * tt
    qw = FOX_HEADS * FOX_HEAD_DIM
    kw = FOX_KV_HEADS * FOX_HEAD_DIM
    _, h = _modulated(x_ref, sh_ref, sc_ref, g_ref)
    hb = h.reshape(m, d).astype(BF16)
    proj = _dot(hb, win_ref[...])
    q = proj[:, :qw] * (FOX_SCALE * LOG2E)
    k = proj[:, qw:qw + kw]
    v = proj[:, qw + kw:qw + 2 * kw]
    z = proj[:, qw + 2 * kw:] + bf_ref[...]
    logf = jnp.minimum(z, 0.0) - jnp.log(1.0 + jnp.exp(-jnp.abs(z)))
    k_ref[...] = k
    v_ref[...] = v
    lf_ref[...] = logf[:, :FOX_HEADS]
    if not prompt:
        q_ref[...] = q.astype(BF16)
        return

    @pl.when(pl.program_id(1) == 0)
    def _():
        carry_ref[...] = jnp.zeros(carry_ref.shape, F32)

    lane = _iota((m, LANES), 1)
    logf = jnp.where(lane < FOX_HEADS, logf, 0.0)
    tri = (_iota((m, m), 0) >= _iota((m, m), 1)).astype(BF16)
    f_hi, f_mid, f_lo = _split3(logf)
    cum = _dot(tri, f_hi) + _dot(tri, f_mid) + _dot(tri, f_lo) + carry_ref[...]
    carry_ref[...] = cum[m - 1:m, :]
    hi, mid, lo = _split3(cum * LOG2E)
    comb = (hi.astype(F32) + pltpu.roll(mid.astype(F32), FOX_HEADS, 1)
            + pltpu.roll(lo.astype(F32), 2 * FOX_HEADS, 1)).astype(BF16)
    aug = _dot(comb, sel_ref[...]) + selc_ref[...]
    n_pairs = FOX_HEADS // 2
    for i in range(n_pairs):
        qa_ref[:, i * 256:i * 256 + 128] = q[:, i * 128:(i + 1) * 128].astype(BF16)
        qa_ref[:, i * 256 + 128:(i + 1) * 256] = aug[:, i * 128:(i + 1) * 128].astype(BF16)
    low_half = lane < FOX_HEAD_DIM
    a0 = n_pairs * 128
    for grp in range(FOX_KV_HEADS):
        blk = (grp // 2) * 128
        for src, lo_ref, hi_ref, stride, has_aug in ((k, klo_ref, khi_ref, 256, True),
                                                     (v, vlo_ref, vhi_ref, 128, False)):
            same = src[:, blk:blk + 128]
            swapped = pltpu.roll(same, FOX_HEAD_DIM, 1)
            lo_src, hi_src = (same, swapped) if grp % 2 == 0 else (swapped, same)
            b0 = grp * stride
            lo_ref[:, b0:b0 + 128] = jnp.where(low_half, lo_src, 0.0).astype(BF16)
            hi_ref[:, b0:b0 + 128] = jnp.where(low_half, 0.0, hi_src).astype(BF16)
            if has_aug:
                lo_ref[:, b0 + 128:b0 + 256] = aug[:, a0 + grp * 128:a0 + (grp + 1) * 128].astype(BF16)
                hi_ref[:, b0 + 128:b0 + 256] = aug[:, a0 + 512 + grp * 128:
                                                   a0 + 512 + (grp + 1) * 128].astype(BF16)


def _fox_aug_selectors():
    n_pairs = FOX_HEADS // 2
    sel = np.zeros((LANES, (n_pairs + 2 * FOX_KV_HEADS) * LANES), np.float32)
    const = np.zeros((1, sel.shape[1]), np.float32)
    klo0 = n_pairs * LANES
    khi0 = klo0 + FOX_KV_HEADS * LANES
    for part in range(3):
        src = part * FOX_HEADS
        for i in range(n_pairs):
            sel[src + 2 * i, i * LANES + part] = 1.0
            sel[src + 2 * i + 1, i * LANES + 3 + part] = 1.0
            const[0, i * LANES + 6 + 3 * (i % 2) + part] = 1.0
        for g in range(FOX_KV_HEADS):
            const[0, klo0 + g * LANES + part] = 1.0
            const[0, khi0 + g * LANES + 3 + part] = 1.0
            sel[src + 4 * g, klo0 + g * LANES + 6 + part] = -1.0
            sel[src + 4 * g + 2, klo0 + g * LANES + 9 + part] = -1.0
            sel[src + 4 * g + 1, khi0 + g * LANES + 6 + part] = -1.0
            sel[src + 4 * g + 3, khi0 + g * LANES + 9 + part] = -1.0
    return jnp.asarray(sel, BF16), jnp.asarray(const, F32)


def _fox_proj(x3, mods3, norm_g, w_in, b_f, nb, tt, prompt):
    n_b, n_t, d = x3.shape
    m = nb * tt
    n_tt = n_t // tt
    rows = n_b * n_t
    x_spec = pl.BlockSpec((nb, tt, d), lambda i, j: (i, j, 0))

    def mod_spec(col):
        return pl.BlockSpec((nb, 1, d), lambda i, j, col=col: (i, 0, col))

    def const_spec(shape):
        return pl.BlockSpec(shape, lambda i, j: (0,) * len(shape))

    def row_spec(width):
        return pl.BlockSpec((m, width), lambda i, j: (i * n_tt + j, 0))

    kw = FOX_KV_HEADS * FOX_HEAD_DIM
    ops = [x3, mods3, mods3, norm_g.reshape(1, 1, d), w_in, b_f]
    specs = [x_spec, mod_spec(3), mod_spec(4), const_spec((1, 1, d)), const_spec(w_in.shape),
             const_spec(b_f.shape)]
    scratch = []
    if prompt:
        sel, selc = _fox_aug_selectors()
        ops += [sel, selc]
        specs += [const_spec(sel.shape), const_spec(selc.shape)]
        widths = [(FOX_HEADS // 2 * 256, BF16), (FOX_KV_HEADS * 256, BF16), (FOX_KV_HEADS * 256, BF16),
                  (FOX_KV_HEADS * 128, BF16), (FOX_KV_HEADS * 128, BF16)]
        scratch = [pltpu.VMEM((1, LANES), F32)]
    else:
        widths = [(FOX_HEADS * FOX_HEAD_DIM, BF16)]
    widths += [(kw, F32), (kw, F32), (FOX_HEADS, F32)]
    return pl.pallas_call(
        functools.partial(_fox_proj_kernel, prompt=prompt),
        grid=(n_b // nb, n_tt),
        in_specs=specs,
        out_specs=[row_spec(wd) for wd, _ in widths],
        out_shape=[jax.ShapeDtypeStruct((rows, wd), dt) for wd, dt in widths],
        scratch_shapes=scratch,
        compiler_params=_params("parallel", "arbitrary"),
        name="fox_proj",
    )(*ops)


def _fox_paged_kernel(pt_ref, q_ref, lfn_ref, kn_ref, vn_ref, kt_hbm, vt_hbm, lft_hbm, o_ref,
                      k_buf, v_buf, lf_buf, sems, kbuf, vbuf, m_ref, l_ref, acc_ref, run_ref, ncol_ref,
                      *, n_pg, n_sub, dec_seq):
    c = pl.program_id(1)
    n_steps = pl.num_programs(1)
    rows, width = q_ref.shape
    q = q_ref[...]
    stream = _page_stream(pt_ref, (kt_hbm, vt_hbm, lft_hbm), (k_buf, v_buf, lf_buf), sems, n_pg,
                          lambda step, half: 2 * (n_steps - 1 - step) + 1 - half)

    def add_head_rows(s, per_head):
        return jnp.concatenate([s[hd * dec_seq:(hd + 1) * dec_seq, :] + per_head[hd:hd + 1, :]
                                for hd in range(FOX_HEADS)], axis=0)

    @pl.when(c == 0)
    def _():
        lfn = lfn_ref[...]
        n_new = lfn.shape[0]
        row = _iota((n_new, LANES), 0)
        n_cum = jnp.zeros((n_new, LANES), F32)
        for t in range(dec_seq):
            n_cum = n_cum + jnp.where(row >= t, lfn[t:t + 1, :], 0.0)
        n_cum = n_cum * LOG2E
        eye = (_iota((LANES, LANES), 0) == _iota((LANES, LANES), 1)).astype(BF16)
        hi, mid, lo = _split3(n_cum)
        n_cum_t = _dot_nt(eye, hi) + _dot_nt(eye, mid) + _dot_nt(eye, lo)
        lane = _iota((dec_seq, LANES), 1)
        for hd in range(FOX_HEADS):
            col = jnp.sum(jnp.where(lane == hd, n_cum[:dec_seq, :], 0.0), axis=1, keepdims=True)
            ncol_ref[hd * dec_seq:(hd + 1) * dec_seq, :] = jnp.broadcast_to(col, (dec_seq, LANES))
        s = add_head_rows(_dot_nt(q, kn_ref[...]), -n_cum_t)
        s = jnp.where(_iota((rows, n_new), 1) <= _iota((rows, n_new), 0) % dec_seq, s, -jnp.inf)
        m_new, _, l_new, p_b = _online_softmax(
            s, jnp.full((rows, LANES), -jnp.inf, F32), jnp.zeros((rows, LANES), F32), ncol_ref[...])
        m_ref[...] = m_new
        l_ref[...] = l_new
        acc_ref[...] = _dot(p_b, vn_ref[...])
        run_ref[...] = jnp.zeros(run_ref.shape, F32)

    sub = n_pg // n_sub
    span = sub * PAGE_SIZE
    later = (_iota((LANES, LANES), 0) > _iota((LANES, LANES), 1)).astype(BF16)
    suffix_and_total = jnp.concatenate([later, jnp.ones((LANES, LANES), BF16)], axis=1)

    def consume(slot):
        lft = lf_buf[slot].reshape(n_pg * FOX_HEADS, PAGE_SIZE)
        both = _dot3(lft, suffix_and_total) * LOG2E
        run = run_ref[...]
        bias_pages = [None] * n_pg
        for j in reversed(range(n_pg)):
            bias_pages[j] = run + both[j * FOX_HEADS:(j + 1) * FOX_HEADS, :LANES]
            run = run + both[j * FOX_HEADS:(j + 1) * FOX_HEADS, LANES:]
        run_ref[...] = run
        ncol = ncol_ref[...]

        def scores(b):
            for j in range(b * sub, (b + 1) * sub):
                kbuf[:, j * PAGE_SIZE:(j + 1) * PAGE_SIZE] = k_buf[slot, j].astype(BF16)
                vbuf[:, j * PAGE_SIZE:(j + 1) * PAGE_SIZE] = v_buf[slot, j].astype(BF16)
            return _dot(q, kbuf[:, b * span:(b + 1) * span])

        def update(b, s):
            bias = jnp.concatenate(bias_pages[b * sub:(b + 1) * sub], axis=1)
            m_new, alpha, l_new, p_b = _online_softmax(add_head_rows(s, bias), m_ref[...], l_ref[...],
                                                       ncol)
            l_ref[...] = l_new
            acc_ref[...] = (jnp.concatenate([alpha] * (width // LANES), axis=1) * acc_ref[...]
                            + _dot_nt(p_b, vbuf[:, b * span:(b + 1) * span]))
            m_ref[...] = m_new

        _pipelined(n_sub, scores, update)

    stream(consume)

    @pl.when(c == n_steps - 1)
    def _():
        o_ref[...] = acc_ref[...] / jnp.sum(l_ref[...], axis=1, keepdims=True)


def _fox_paged(page_table, q_all, lf_new, k_new, v_new, cache_kt, cache_vt, cache_lft, n_pg, n_sub, dec_seq):
    n_seq, rows, width = q_all.shape
    n_steps = page_table.shape[1] // (2 * n_pg)

    def seq_spec(a):
        return pl.BlockSpec((None,) + a.shape[1:], lambda b, c, pt: (b, 0, 0))

    hbm_spec = pl.BlockSpec(memory_space=pl.ANY)
    grid_spec = pltpu.PrefetchScalarGridSpec(
        num_scalar_prefetch=1,
        grid=(n_seq, n_steps),
        in_specs=[seq_spec(q_all), seq_spec(lf_new), seq_spec(k_new), seq_spec(v_new),
                  hbm_spec, hbm_spec, hbm_spec],
        out_specs=pl.BlockSpec((None, rows, width), lambda b, c, pt: (b, 0, 0)),
        scratch_shapes=[pltpu.VMEM((2, n_pg, width, PAGE_SIZE), F32),
                        pltpu.VMEM((2, n_pg, width, PAGE_SIZE), F32),
                        pltpu.VMEM((2, n_pg, FOX_HEADS, PAGE_SIZE), F32),
                        pltpu.SemaphoreType.DMA((3, 2)),
                        pltpu.VMEM((width, n_pg * PAGE_SIZE), BF16),
                        pltpu.VMEM((width, n_pg * PAGE_SIZE), BF16),
                        pltpu.VMEM((rows, LANES), F32), pltpu.VMEM((rows, LANES), F32),
                        pltpu.VMEM((rows, width), F32),
                        pltpu.VMEM((FOX_HEADS, LANES), F32),
                        pltpu.VMEM((rows, LANES), F32)],
    )
    return pl.pallas_call(
        functools.partial(_fox_paged_kernel, n_pg=n_pg, n_sub=n_sub, dec_seq=dec_seq),
        grid_spec=grid_spec,
        out_shape=jax.ShapeDtypeStruct((n_seq, rows, width), F32),
        compiler_params=_params("arbitrary", "arbitrary"),
        name="fox_paged",
    )(page_table, q_all, lf_new, k_new, v_new, cache_kt, cache_vt, cache_lft)


def _rope_tables(pos):
    half = MLA_ROPE // 2
    inv_freq = ROPE_THETA ** (-jnp.arange(half, dtype=F32) / half)
    ang = pos.astype(F32)[:, None] * inv_freq[None, :]
    cos, sin, zero = jnp.cos(ang), jnp.sin(ang), jnp.zeros_like(ang)
    return (jnp.concatenate([cos, cos, zero, zero], axis=1),
            jnp.concatenate([-sin, sin, zero, zero], axis=1))


def _mla_weights(w_in, g_q, w_uq, g_kv, w_uk, w_uv, w_o):
    lat = MLA_Q_LORA + MLA_KV_LORA
    kr_cols = np.concatenate([np.arange(lat, lat + MLA_ROPE)] * 2)
    in_cols = np.concatenate([np.arange(lat), kr_cols])
    uq_cols = []
    for hd in range(MLA_HEADS):
        b0 = hd * (MLA_NOPE + MLA_ROPE)
        rope_cols = np.arange(b0 + MLA_NOPE, b0 + MLA_NOPE + MLA_ROPE)
        uq_cols += [np.arange(b0, b0 + MLA_NOPE), rope_cols, rope_cols]
    return dict(
        w_in=w_in[:, in_cols].astype(BF16),
        g_q=g_q.reshape(1, -1), g_kv=g_kv.reshape(1, -1),
        w_uq=w_uq[:, np.concatenate(uq_cols)].astype(BF16),
        w_uk=w_uk.reshape(MLA_KV_LORA, MLA_HEADS * MLA_NOPE).astype(BF16),
        w_uv=w_uv.reshape(MLA_KV_LORA, MLA_HEADS * MLA_V).astype(BF16),
        w_ukt=jnp.transpose(w_uk, (1, 2, 0)).astype(BF16),
        w_uv3=jnp.transpose(w_uv, (1, 0, 2)).astype(BF16),
        w_o=w_o.astype(BF16),
    )


def _pad_rows(a, rows):
    return jnp.pad(a, ((0, 0), (0, rows - a.shape[1])) + ((0, 0),) * (a.ndim - 2))


def _pages_per_step(n_pages, want):
    n = min(want, n_pages)
    while n_pages % n:
        n -= 1
    return n


def kernel(x_prompt, x_sample, c_prompt, c_sample, cache_mla_ckv, cache_mla_krope, cache_fox_k, cache_fox_v, cache_fox_logf, page_table, ada_w, ada_b, norm_g, ffn_w_gu, ffn_w_dn, final_g, mla_w_in, mla_g_q, mla_w_uq, mla_g_kv, mla_w_uk, mla_w_uv, mla_w_o, fox_w_in, fox_b_f, fox_w_o):
    n_p, seq, d = x_prompt.shape
    n_s, dec_seq, _ = x_sample.shape
    n_pages = page_table.shape[1]
    past_len = n_pages * PAGE_SIZE
    tile = min(512, seq)
    tk = min(1024, seq)
    n_pad = LANES

    mods = _adaln(jnp.concatenate([c_prompt, c_sample], axis=0), ada_w, ada_b)
    w_gu = ffn_w_gu.astype(BF16)
    w_dn = ffn_w_dn.astype(BF16)
    xp, xs = x_prompt, x_sample

    mp = mods[0, :n_p].reshape(n_p, 1, -1)
    ms = mods[0, n_p:].reshape(n_s, 1, -1)
    w = _mla_weights(mla_w_in[0], mla_g_q[0], mla_w_uq[0], mla_g_kv[0], mla_w_uk[0], mla_w_uv[0],
                     mla_w_o[0])
    xp = _ffn(xp, mp, 0, norm_g[0, 0], w_gu[0, 0], w_dn[0, 0], 1, tile)
    xs = _ffn(xs, ms, 0, norm_g[0, 0], w_gu[0, 0], w_dn[0, 0], n_s, dec_seq)

    cos_p, sin_p = _rope_tables(jnp.arange(seq, dtype=jnp.int32))
    q, k, v, p_ckv, p_kr = _mla_proj(xp, mp, norm_g[0, 1], w, cos_p, sin_p, 1, tile, absorb=False)
    units = [(slice(h * 256, (h + 1) * 256), 0, slice(h * 256, (h + 1) * 256), 1,
              slice(h * 128, (h + 1) * 128)) for h in range(MLA_HEADS)]
    o_p = _flash(q, [k, v], units, [[h] for h in range(MLA_HEADS)], n_p, seq, tile, tk)

    cos_s, sin_s = _rope_tables(past_len + jnp.arange(dec_seq, dtype=jnp.int32))
    cos_s, sin_s = jnp.tile(cos_s, (n_s, 1)), jnp.tile(sin_s, (n_s, 1))
    q_lat, q_rope, s_ckv, s_kr = _mla_proj(xs, ms, norm_g[0, 1], w, cos_s, sin_s, n_s, dec_seq,
                                           absorb=True)
    rows = dec_seq * MLA_HEADS
    n_pg = _pages_per_step(n_pages // 2, 32)
    n_sub = 2 if n_pg % 2 == 0 else 1
    o_lat = _mla_paged(
        page_table,
        q_lat.reshape(n_s, rows, MLA_KV_LORA), q_rope.reshape(n_s, rows, LANES),
        _pad_rows(s_ckv.reshape(n_s, dec_seq, MLA_KV_LORA), n_pad).astype(BF16),
        _pad_rows(s_kr.reshape(n_s, dec_seq, MLA_ROPE), n_pad).astype(BF16),
        cache_mla_ckv[0], jnp.swapaxes(cache_mla_krope[0], 1, 2), n_pg, n_sub)
    o_s = _mla_uv(o_lat.reshape(n_s * dec_seq, MLA_HEADS * MLA_KV_LORA), w["w_uv3"])

    xp = _ffn(xp, mp, 2, norm_g[0, 2], w_gu[0, 1], w_dn[0, 1], 1, tile, mix=(o_p, w["w_o"]))
    xs = _ffn(xs, ms, 2, norm_g[0, 2], w_gu[0, 1], w_dn[0, 1], n_s, dec_seq, mix=(o_s, w["w_o"]))

    mp = mods[1, :n_p].reshape(n_p, 1, -1)
    ms = mods[1, n_p:].reshape(n_s, 1, -1)
    qw = FOX_HEADS * FOX_HEAD_DIM
    kw = FOX_KV_HEADS * FOX_HEAD_DIM
    fw_in = jnp.pad(fox_w_in[0], ((0, 0), (0, LANES - FOX_HEADS))).astype(BF16)
    fb = jnp.pad(fox_b_f[0], (0, LANES - FOX_HEADS)).reshape(1, LANES)
    fw_o = fox_w_o[0].astype(BF16)
    xp = _ffn(xp, mp, 0, norm_g[1, 0], w_gu[1, 0], w_dn[1, 0], 1, tile)
    xs = _ffn(xs, ms, 0, norm_g[1, 0], w_gu[1, 0], w_dn[1, 0], n_s, dec_seq)

    qa, klo, khi, vlo, vhi, p_k, p_v, p_lf = _fox_proj(xp, mp, norm_g[1, 1], fw_in, fb, 1, tile,
                                                       prompt=True)
    units = []
    for h in range(FOX_HEADS):
        i, par, g = h // 2, h % 2, h // (FOX_HEADS // FOX_KV_HEADS)
        units.append((slice(i * 256, (i + 1) * 256), par, slice(g * 256, (g + 1) * 256), 2 + par,
                      slice(g * 128, (g + 1) * 128)))
    o_p = _flash(qa, [klo, khi, vlo, vhi], units, [[2 * i, 2 * i + 1] for i in range(FOX_HEADS // 2)],
                 n_p, seq, tile, tk)

    q_s, s_k, s_v, s_lf = _fox_proj(xs, ms, norm_g[1, 1], fw_in, fb, n_s, dec_seq, prompt=False)
    q4 = jnp.transpose(q_s.reshape(n_s, dec_seq, FOX_HEADS, FOX_HEAD_DIM), (0, 2, 1, 3))
    grp_of_head = jnp.arange(FOX_HEADS) // (FOX_HEADS // FOX_KV_HEADS)
    onehot = (grp_of_head[:, None] == jnp.arange(FOX_KV_HEADS)[None, :]).astype(BF16)
    q_all = (q4[:, :, :, None, :] * onehot[None, :, None, :, None]).reshape(
        n_s, FOX_HEADS * dec_seq, kw)
    lf_new = jnp.pad(s_lf.reshape(n_s, dec_seq, FOX_HEADS),
                     ((0, 0), (0, n_pad - dec_seq), (0, LANES - FOX_HEADS)))
    cache_lft = jnp.swapaxes(cache_fox_logf[0], 1, 2)
    cache_kt = jnp.transpose(cache_fox_k[0], (0, 2, 3, 1)).reshape(-1, kw, PAGE_SIZE)
    cache_vt = jnp.transpose(cache_fox_v[0], (0, 2, 3, 1)).reshape(-1, kw, PAGE_SIZE)
    o_all = _fox_paged(
        page_table, q_all, lf_new,
        _pad_rows(s_k.reshape(n_s, dec_seq, kw), n_pad).astype(BF16),
        _pad_rows(s_v.reshape(n_s, dec_seq, kw), n_pad).astype(BF16),
        cache_kt, cache_vt, cache_lft, n_pg, n_sub, dec_seq)
    o5 = o_all.reshape(n_s, FOX_HEADS, dec_seq, FOX_KV_HEADS, FOX_HEAD_DIM)
    o_sel = jnp.take_along_axis(o5, grp_of_head[None, :, None, None, None], axis=3)[:, :, :, 0, :]
    o_s = jnp.transpose(o_sel, (0, 2, 1, 3)).reshape(n_s * dec_seq, qw).astype(BF16)

    xp = _ffn(xp, mp, 2, norm_g[1, 2], w_gu[1, 1], w_dn[1, 1], 1, tile, mix=(o_p, fw_o),
              final_g=final_g)
    xs = _ffn(xs, ms, 2, norm_g[1, 2], w_gu[1, 1], w_dn[1, 1], n_s, dec_seq, mix=(o_s, fw_o),
              final_g=final_g)

    def prompt_state(a, *tail):
        return a.reshape(1, n_p, seq, *tail)

    def sample_state(a, *tail):
        return a.reshape(1, n_s, dec_seq, *tail)

    return (xp, xs,
            prompt_state(p_ckv, MLA_KV_LORA), prompt_state(p_kr, MLA_ROPE),
            prompt_state(p_k, FOX_KV_HEADS, FOX_HEAD_DIM), prompt_state(p_v, FOX_KV_HEADS, FOX_HEAD_DIM),
            prompt_state(p_lf, FOX_HEADS),
            sample_state(s_ckv, MLA_KV_LORA), sample_state(s_kr, MLA_ROPE),
            sample_state(s_k, FOX_KV_HEADS, FOX_HEAD_DIM), sample_state(s_v, FOX_KV_HEADS, FOX_HEAD_DIM),
            sample_state(s_lf, FOX_HEADS))
```

```python
import functools
import math

import numpy as np
import jax
import jax.numpy as jnp
from jax import lax
from jax.experimental import pallas as pl
from jax.experimental.pallas import tpu as pltpu

F32 = jnp.float32
BF16 = jnp.bfloat16

LANES = 128
NORM_EPS = 1e-6
PAGE_SIZE = 128
D_FF = 2816
MLA_HEADS = 8
MLA_NOPE = 128
MLA_ROPE = 64
MLA_V = 128
MLA_Q_LORA = 512
MLA_KV_LORA = 256
ROPE_THETA = 10000.0
MLA_SCALE = 1.0 / math.sqrt(MLA_NOPE + MLA_ROPE)
FOX_HEADS = 16
FOX_KV_HEADS = 4
FOX_HEAD_DIM = 64
FOX_SCALE = 1.0 / math.sqrt(FOX_HEAD_DIM)
LOG2E = math.log2(math.e)
VMEM_LIMIT_BYTES = 56 * 1024 * 1024


def _params(*sem):
    return pltpu.CompilerParams(dimension_semantics=sem, vmem_limit_bytes=VMEM_LIMIT_BYTES)


def _dot(a, b):
    return jnp.dot(a, b, preferred_element_type=F32)


def _dot_nt(a, b):
    return lax.dot_general(a, b, (((1,), (1,)), ((), ())), preferred_element_type=F32)


def _sigmoid(x):
    return 1.0 / (1.0 + jnp.exp(-x))


def _rms(x, g):
    return x * lax.rsqrt(jnp.mean(x * x, axis=-1, keepdims=True) + NORM_EPS) * g


def _split3(x):
    hi = x.astype(BF16)
    r = x - hi.astype(F32)
    mid = r.astype(BF16)
    lo = (r - mid.astype(F32)).astype(BF16)
    return hi, mid, lo


def _dot3(x, m):
    hi, mid, lo = _split3(x)
    return _dot(hi, m) + _dot(mid, m) + _dot(lo, m)


def _iota(shape, dim):
    return lax.broadcasted_iota(jnp.int32, shape, dim)


def _modulated(x_ref, sh_ref, sc_ref, g_ref):
    x = x_ref[...]
    return x, _rms(x, g_ref[...]) * (1.0 + sc_ref[...]) + sh_ref[...]


def _adaln_kernel(c_ref, w_ref, b_ref, o_ref):
    c = c_ref[...]
    a = (c * _sigmoid(c)).astype(BF16)
    o_ref[0] = _dot(a, w_ref[0].astype(BF16)) + b_ref[0]


def _adaln(c_all, ada_w, ada_b):
    n_layers, d, n = ada_w.shape
    rows = c_all.shape[0]
    tn = 1536 if n % 1536 == 0 else n
    return pl.pallas_call(
        _adaln_kernel,
        grid=(n_layers, n // tn),
        in_specs=[
            pl.BlockSpec((rows, d), lambda l, j: (0, 0)),
            pl.BlockSpec((1, d, tn), lambda l, j: (l, 0, j)),
            pl.BlockSpec((1, 1, tn), lambda l, j: (l, 0, j)),
        ],
        out_specs=pl.BlockSpec((1, rows, tn), lambda l, j: (l, 0, j)),
        out_shape=jax.ShapeDtypeStruct((n_layers, rows, n), F32),
        compiler_params=_params("parallel", "parallel"),
        name="adaln",
    )(c_all, ada_w, ada_b.reshape(n_layers, 1, n))


def _ffn_kernel(*refs, has_mix, final, n_chunks, chunk):
    it = iter(refs)
    x_ref, sh_ref, sc_ref, gt_ref, g_ref, wgu_ref, wdn_ref = (next(it) for _ in range(7))
    if has_mix:
        o_ref, wo_ref, gm_ref = next(it), next(it), next(it)
    if final:
        fg_ref = next(it)
    out_ref, acc_ref = next(it), next(it)
    nb, tt, d = x_ref.shape
    m = nb * tt
    x = x_ref[...]
    if has_mix:
        x = x + gm_ref[...] * _dot(o_ref[...], wo_ref[...]).reshape(nb, tt, d)
    h = _rms(x, g_ref[...]) * (1.0 + sc_ref[...]) + sh_ref[...]
    hb = h.reshape(m, d).astype(BF16)
    d_ff = wdn_ref.shape[0]
    for c in range(n_chunks):
        lo = c * chunk
        gate = _dot(hb, wgu_ref[:, lo:lo + chunk])
        up = _dot(hb, wgu_ref[:, d_ff + lo:d_ff + lo + chunk])
        act = (gate * _sigmoid(gate) * up).astype(BF16)
        y = _dot(act, wdn_ref[lo:lo + chunk, :])
        if c == 0:
            acc_ref[...] = y
        else:
            acc_ref[...] += y
    y = x + (0.5 * gt_ref[...]) * acc_ref[...].reshape(nb, tt, d)
    if final:
        y = _rms(y, fg_ref[...])
    out_ref[...] = y


def _ffn(x3, mods3, sub, norm_g, w_gu, w_dn, nb, tt, mix=None, final_g=None):
    n_b, n_t, d = x3.shape
    d_ff = w_dn.shape[0]
    chunk = 256
    grid = (n_b // nb, n_t // tt)
    m = nb * tt
    x_spec = pl.BlockSpec((nb, tt, d), lambda i, j: (i, j, 0))

    def mod_spec(col):
        return pl.BlockSpec((nb, 1, d), lambda i, j, col=col: (i, 0, col))

    def const_spec(shape):
        return pl.BlockSpec(shape, lambda i, j: (0,) * len(shape), pipeline_mode=pl.Buffered(1))

    ops = [x3, mods3, mods3, mods3, norm_g.reshape(1, 1, d), w_gu, w_dn]
    specs = [x_spec, mod_spec(3 * sub), mod_spec(3 * sub + 1), mod_spec(3 * sub + 2),
             const_spec((1, 1, d)), const_spec(w_gu.shape), const_spec(w_dn.shape)]
    if mix is not None:
        o2, w_o = mix
        n_tt = n_t // tt
        ops += [o2, w_o, mods3]
        specs += [pl.BlockSpec((m, o2.shape[1]), lambda i, j: (i * n_tt + j, 0)),
                  const_spec(w_o.shape), mod_spec(3 * 1 + 2)]
    if final_g is not None:
        ops.append(final_g.reshape(1, 1, d))
        specs.append(const_spec((1, 1, d)))
    return pl.pallas_call(
        functools.partial(_ffn_kernel, has_mix=mix is not None, final=final_g is not None,
                          n_chunks=d_ff // chunk, chunk=chunk),
        grid=grid,
        in_specs=specs,
        out_specs=x_spec,
        out_shape=jax.ShapeDtypeStruct(x3.shape, F32),
        scratch_shapes=[pltpu.VMEM((m, d), F32)],
        compiler_params=_params("parallel", "parallel"),
        name="ffn",
    )(*ops)


def _mla_proj_kernel(*refs, absorb):
    it = iter(refs)
    x_ref, sh_ref, sc_ref, g_ref, win_ref, gq_ref, gkv_ref, wuq_ref, cos_ref, sin_ref = (
        next(it) for _ in range(10))
    if absorb:
        wukt_ref = next(it)
        qlat_ref, qrope_ref, ckv_ref, kr_ref = (next(it) for _ in range(4))
    else:
        wuk_ref, wuv_ref = next(it), next(it)
        q_ref, k_ref, v_ref, ckv_ref, kr_ref = (next(it) for _ in range(5))
    nb, tt, d = x_ref.shape
    m = nb * tt
    _, h = _modulated(x_ref, sh_ref, sc_ref, g_ref)
    hb = h.reshape(m, d).astype(BF16)
    proj = _dot(hb, win_ref[...])
    c_q = _rms(proj[:, :MLA_Q_LORA], gq_ref[...])
    c_kv = _rms(proj[:, MLA_Q_LORA:MLA_Q_LORA + MLA_KV_LORA], gkv_ref[...])
    ckv_ref[...] = c_kv
    cos_a = cos_ref[...]
    sin_b = sin_ref[...]

    def rope(z):
        return z * cos_a + pltpu.roll(z, 32, 1) * sin_b

    kr = rope(proj[:, MLA_Q_LORA + MLA_KV_LORA:])
    kr_ref[...] = kr[:, :MLA_ROPE]
    qf = _dot(c_q.astype(BF16), wuq_ref[...]) * (MLA_SCALE * LOG2E)
    ckv_b = c_kv.astype(BF16)
    if not absorb:
        kn = _dot(ckv_b, wuk_ref[...])
        v_ref[...] = _dot(ckv_b, wuv_ref[...]).astype(BF16)
        kr_b = kr.astype(BF16)
    for hd in range(MLA_HEADS):
        b0 = hd * 256
        qn = qf[:, b0:b0 + 128]
        qr = rope(qf[:, b0 + 128:b0 + 256])
        if absorb:
            qlat_ref[:, b0:b0 + 256] = _dot(qn.astype(BF16), wukt_ref[hd]).astype(BF16)
            qrope_ref[:, hd * 128:(hd + 1) * 128] = qr.astype(BF16)
        else:
            q_ref[:, b0:b0 + 128] = qn.astype(BF16)
            q_ref[:, b0 + 128:b0 + 256] = qr.astype(BF16)
            k_ref[:, b0:b0 + 128] = kn[:, hd * 128:(hd + 1) * 128].astype(BF16)
            k_ref[:, b0 + 128:b0 + 256] = kr_b


def _mla_proj(x3, mods3, norm_g, w, cos_a, sin_b, nb, tt, absorb):
    n_b, n_t, d = x3.shape
    m = nb * tt
    n_tt = n_t // tt
    rows = n_b * n_t
    grid = (n_b // nb, n_tt)
    x_spec = pl.BlockSpec((nb, tt, d), lambda i, j: (i, j, 0))

    def mod_spec(col):
        return pl.BlockSpec((nb, 1, d), lambda i, j, col=col: (i, 0, col))

    def const_spec(shape):
        return pl.BlockSpec(shape, lambda i, j: (0,) * len(shape))

    def row_spec(width):
        return pl.BlockSpec((m, width), lambda i, j: (i * n_tt + j, 0))

    tab_spec = pl.BlockSpec((m, LANES), lambda i, j: (j, 0))
    ops = [x3, mods3, mods3, norm_g.reshape(1, 1, d), w["w_in"], w["g_q"], w["g_kv"], w["w_uq"],
           cos_a, sin_b]
    specs = [x_spec, mod_spec(3), mod_spec(4), const_spec((1, 1, d)), const_spec(w["w_in"].shape),
             const_spec(w["g_q"].shape), const_spec(w["g_kv"].shape), const_spec(w["w_uq"].shape),
             tab_spec, tab_spec]
    if absorb:
        ops.append(w["w_ukt"])
        specs.append(const_spec(w["w_ukt"].shape))
        widths = [(MLA_HEADS * 256, BF16), (MLA_HEADS * 128, BF16)]
    else:
        ops += [w["w_uk"], w["w_uv"]]
        specs += [const_spec(w["w_uk"].shape), const_spec(w["w_uv"].shape)]
        widths = [(MLA_HEADS * 256, BF16), (MLA_HEADS * 256, BF16), (MLA_HEADS * MLA_V, BF16)]
    widths += [(MLA_KV_LORA, F32), (MLA_ROPE, F32)]
    return pl.pallas_call(
        functools.partial(_mla_proj_kernel, absorb=absorb),
        grid=grid,
        in_specs=specs,
        out_specs=[row_spec(wd) for wd, _ in widths],
        out_shape=[jax.ShapeDtypeStruct((rows, wd), dt) for wd, dt in widths],
        compiler_params=_params("parallel", "parallel"),
        name="mla_proj",
    )(*ops)


def _online_softmax(s, m_prev, l_prev, row_bias=None):
    blocks = [s[:, b * LANES:(b + 1) * LANES] for b in range(s.shape[1] // LANES)]
    if row_bias is not None:
        blocks = [blk + row_bias for blk in blocks]
    mx = blocks[0]
    for blk in blocks[1:]:
        mx = jnp.maximum(mx, blk)
    m_new = jnp.maximum(m_prev, jnp.max(mx, axis=1, keepdims=True))
    alpha = jnp.exp2(m_prev - m_new)
    ps = [jnp.exp2(blk - m_new) for blk in blocks]
    psum = ps[0]
    for p in ps[1:]:
        psum = psum + p
    p_b = jnp.concatenate([p.astype(BF16) for p in ps], axis=1) if len(ps) > 1 else ps[0].astype(BF16)
    return m_new, alpha, alpha * l_prev + psum, p_b


def _pipelined(n, produce, consume):
    nxt = produce(0)
    for b in range(n):
        cur = nxt
        if b + 1 < n:
            nxt = produce(b + 1)
        consume(b, cur)


def _flash_kernel(*refs, units, outs, n_kv):
    q_ref = refs[0]
    kv = refs[1:1 + n_kv]
    o_ref = refs[1 + n_kv]
    m_ref, l_ref, acc_ref = refs[2 + n_kv:]
    i = pl.program_id(1)
    j = pl.program_id(2)
    tq = q_ref.shape[0]
    tk = kv[0].shape[0]
    j_last = ((i + 1) * tq - 1) // tk

    @pl.when(j == 0)
    def _():
        m_ref[...] = jnp.full(m_ref.shape, -jnp.inf, F32)
        l_ref[...] = jnp.zeros(l_ref.shape, F32)
        acc_ref[...] = jnp.zeros(acc_ref.shape, F32)

    def step(masked, n_keys):
        if masked:
            keep = (i * tq + _iota((tq, n_keys), 0)) >= (j * tk + _iota((tq, n_keys), 1))

        def scores(u):
            qs, ki, ks, _, _ = units[u]
            s = _dot_nt(q_ref[:, qs], kv[ki][:n_keys, ks])
            return jnp.where(keep, s, -jnp.inf) if masked else s

        def update(u, s):
            _, _, _, vi, vs = units[u]
            m_new, alpha, l_new, p_b = _online_softmax(s, m_ref[u], l_ref[u])
            l_ref[u] = l_new
            acc_ref[u] = alpha * acc_ref[u] + _dot(p_b, kv[vi][:n_keys, vs])
            m_ref[u] = m_new

        _pipelined(len(units), scores, update)

    fully_visible = (j + 1) * tk <= i * tq
    partly_visible = jnp.logical_and(jnp.logical_not(fully_visible), j <= j_last)
    half = tk // 2
    first_half_only = j * tk + half > (i + 1) * tq - 1 if half % LANES == 0 else False

    @pl.when(fully_visible)
    def _():
        step(False, tk)

    if half % LANES == 0:
        @pl.when(jnp.logical_and(partly_visible, first_half_only))
        def _():
            step(True, half)

    @pl.when(jnp.logical_and(partly_visible, jnp.logical_not(first_half_only)))
    def _():
        step(True, tk)

    @pl.when(j == j_last)
    def _():
        for b, us in enumerate(outs):
            o = None
            for u in us:
                term = acc_ref[u] / jnp.sum(l_ref[u], axis=1, keepdims=True)
                o = term if o is None else o + term
            o_ref[:, b * LANES:(b + 1) * LANES] = o.astype(o_ref.dtype)


def _flash(q, kvs, units, outs, n_batch, seq, tq, tk):
    nq = seq // tq
    nk = seq // tk
    n_units = len(units)

    def q_map(b, i, j):
        return (b * nq + i, 0)

    def kv_map(b, i, j):
        return (b * nk + jnp.minimum(j, ((i + 1) * tq - 1) // tk), 0)

    out_w = LANES * len(outs)
    return pl.pallas_call(
        functools.partial(_flash_kernel, units=units, outs=outs, n_kv=len(kvs)),
        grid=(n_batch, nq, nk),
        in_specs=[pl.BlockSpec((tq, q.shape[1]), q_map)]
        + [pl.BlockSpec((tk, a.shape[1]), kv_map) for a in kvs],
        out_specs=pl.BlockSpec((tq, out_w), q_map),
        out_shape=jax.ShapeDtypeStruct((q.shape[0], out_w), BF16),
        scratch_shapes=[pltpu.VMEM((n_units, tq, LANES), F32), pltpu.VMEM((n_units, tq, LANES), F32),
                        pltpu.VMEM((n_units, tq, LANES), F32)],
        compiler_params=_params("parallel", "parallel", "arbitrary"),
        name="flash",
    )(q, *kvs)


def _page_stream(pt_ref, caches, bufs, sems, n_pg):
    def copies(s, chunk, slot):
        out = []
        for j in range(n_pg):
            page = pt_ref[s, chunk * n_pg + j]
            for a, (cache, buf) in enumerate(zip(caches, bufs)):
                out.append(pltpu.make_async_copy(cache.at[page], buf.at[slot, j], sems.at[a, slot]))
        return out

    def start(s, chunk, slot):
        for n, cp in enumerate(copies(s, chunk, slot)):
            cp.start(priority=(n // len(caches)) % 2)

    def wait(s, chunk, slot):
        for cp in copies(s, chunk, slot):
            cp.wait()

    return start, wait


def _mla_engine(ins, o_ref, scratch, n_pg, n_sub):
    qlat_ref, qrope_ref, ckn_ref, krn_ref = ins
    ckv_buf, kr_buf, kbuf, krbuf, m_ref, l_ref, acc_ref = scratch
    rows = qlat_ref.shape[1]
    sub = n_pg // n_sub
    width = sub * PAGE_SIZE

    def begin(sl):
        q_lat = qlat_ref[sl]
        ckn = ckn_ref[sl]
        n_new = ckn.shape[0]
        s = _dot_nt(q_lat, ckn) + _dot_nt(qrope_ref[sl][:, :MLA_ROPE], krn_ref[sl])
        s = jnp.where(_iota((rows, n_new), 1) <= _iota((rows, n_new), 0) // MLA_HEADS, s, -jnp.inf)
        m_new, _, l_new, p_b = _online_softmax(
            s, jnp.full((rows, LANES), -jnp.inf, F32), jnp.zeros((rows, LANES), F32))
        m_ref[...] = m_new
        l_ref[...] = l_new
        acc_ref[...] = _dot(p_b, ckn)

    def consume(sl, slot):
        q_lat = qlat_ref[sl]
        q_rope = qrope_ref[sl][:, :MLA_ROPE]

        def scores(b):
            for j in range(b * sub, (b + 1) * sub):
                kbuf[j * PAGE_SIZE:(j + 1) * PAGE_SIZE, :] = ckv_buf[slot, j].astype(BF16)
                krbuf[:, j * PAGE_SIZE:(j + 1) * PAGE_SIZE] = kr_buf[slot, j].astype(BF16)
            return (_dot_nt(q_lat, kbuf[b * width:(b + 1) * width, :])
                    + _dot(q_rope, krbuf[:, b * width:(b + 1) * width]))

        def update(b, s):
            m_new, alpha, l_new, p_b = _online_softmax(s, m_ref[...], l_ref[...])
            l_ref[...] = l_new
            acc_ref[...] = (jnp.concatenate([alpha] * (MLA_KV_LORA // LANES), axis=1) * acc_ref[...]
                            + _dot(p_b, kbuf[b * width:(b + 1) * width, :]))
            m_ref[...] = m_new

        _pipelined(n_sub, scores, update)

    def finish(sl):
        l_tot = jnp.sum(l_ref[...], axis=1, keepdims=True)
        o_ref[sl] = (acc_ref[...] / l_tot).astype(o_ref.dtype)

    return begin, consume, finish


def _mla_scratch(rows, n_pg):
    return [pltpu.VMEM((2, n_pg, PAGE_SIZE, MLA_KV_LORA), F32),
            pltpu.VMEM((2, n_pg, MLA_ROPE, PAGE_SIZE), F32),
            pltpu.VMEM((n_pg * PAGE_SIZE, MLA_KV_LORA), BF16),
            pltpu.VMEM((MLA_ROPE, n_pg * PAGE_SIZE), BF16),
            pltpu.VMEM((rows, LANES), F32), pltpu.VMEM((rows, LANES), F32),
            pltpu.VMEM((rows, MLA_KV_LORA), F32)]


def _mla_uv_kernel(o_ref, wuv_ref, out_ref):
    for hd in range(MLA_HEADS):
        out_ref[:, hd * MLA_V:(hd + 1) * MLA_V] = _dot(
            o_ref[:, hd * MLA_KV_LORA:(hd + 1) * MLA_KV_LORA], wuv_ref[hd]).astype(BF16)


def _mla_uv(o_lat2, w_uv3):
    rows = o_lat2.shape[0]
    return pl.pallas_call(
        _mla_uv_kernel,
        out_shape=jax.ShapeDtypeStruct((rows, MLA_HEADS * MLA_V), BF16),
        compiler_params=pltpu.CompilerParams(vmem_limit_bytes=VMEM_LIMIT_BYTES),
        name="mla_uv",
    )(o_lat2, w_uv3)


def _fox_proj_kernel(*refs, prompt):
    it = iter(refs)
    x_ref, sh_ref, sc_ref, g_ref, win_ref, bf_ref = (next(it) for _ in range(6))
    if prompt:
        sel_ref, selc_ref = next(it), next(it)
        qa_ref, klo_ref, khi_ref, vlo_ref, vhi_ref, k_ref, v_ref, lf_ref, carry_ref = (
            next(it) for _ in range(9))
    else:
        q_ref, k_ref, v_ref, lf_ref = (next(it) for _ in range(4))
    nb, tt, d = x_ref.shape
    m = nb * tt
    qw = FOX_HEADS * FOX_HEAD_DIM
    kw = FOX_KV_HEADS * FOX_HEAD_DIM
    _, h = _modulated(x_ref, sh_ref, sc_ref, g_ref)
    hb = h.reshape(m, d).astype(BF16)
    proj = _dot(hb, win_ref[...])
    q = proj[:, :qw] * (FOX_SCALE * LOG2E)
    k = proj[:, qw:qw + kw]
    v = proj[:, qw + kw:qw + 2 * kw]
    z = proj[:, qw + 2 * kw:] + bf_ref[...]
    logf = jnp.minimum(z, 0.0) - jnp.log(1.0 + jnp.exp(-jnp.abs(z)))
    k_ref[...] = k
    v_ref[...] = v
    lf_ref[...] = logf[:, :FOX_HEADS]
    if not prompt:
        q_ref[...] = q.astype(BF16)
        return

    @pl.when(pl.program_id(1) == 0)
    def _():
        carry_ref[...] = jnp.zeros(carry_ref.shape, F32)

    lane = _iota((m, LANES), 1)
    logf = jnp.where(lane < FOX_HEADS, logf, 0.0)
    tri = (_iota((m, m), 0) >= _iota((m, m), 1)).astype(BF16)
    f_hi, f_mid, f_lo = _split3(logf)
    cum = _dot(tri, f_hi) + _dot(tri, f_mid) + _dot(tri, f_lo) + carry_ref[...]
    carry_ref[...] = cum[m - 1:m, :]
    hi, mid, lo = _split3(cum * LOG2E)
    comb = (hi.astype(F32) + pltpu.roll(mid.astype(F32), FOX_HEADS, 1)
            + pltpu.roll(lo.astype(F32), 2 * FOX_HEADS, 1)).astype(BF16)
    aug = _dot(comb, sel_ref[...]) + selc_ref[...]
    n_pairs = FOX_HEADS // 2
    for i in range(n_pairs):
        qa_ref[:, i * 256:i * 256 + 128] = q[:, i * 128:(i + 1) * 128].astype(BF16)
        qa_ref[:, i * 256 + 128:(i + 1) * 256] = aug[:, i * 128:(i + 1) * 128].astype(BF16)
    low_half = lane < FOX_HEAD_DIM
    a0 = n_pairs * 128
    for grp in range(FOX_KV_HEADS):
        blk = (grp // 2) * 128
        for src, lo_ref, hi_ref, stride, has_aug in ((k, klo_ref, khi_ref, 256, True),
                                                     (v, vlo_ref, vhi_ref, 128, False)):
            same = src[:, blk:blk + 128]
            swapped = pltpu.roll(same, FOX_HEAD_DIM, 1)
            lo_src, hi_src = (same, swapped) if grp % 2 == 0 else (swapped, same)
            b0 = grp * stride
            lo_ref[:, b0:b0 + 128] = jnp.where(low_half, lo_src, 0.0).astype(BF16)
            hi_ref[:, b0:b0 + 128] = jnp.where(low_half, 0.0, hi_src).astype(BF16)
            if has_aug:
                lo_ref[:, b0 + 128:b0 + 256] = aug[:, a0 + grp * 128:a0 + (grp + 1) * 128].astype(BF16)
                hi_ref[:, b0 + 128:b0 + 256] = aug[:, a0 + 512 + grp * 128:
                                                   a0 + 512 + (grp + 1) * 128].astype(BF16)


def _fox_aug_selectors():
    n_pairs = FOX_HEADS // 2
    sel = np.zeros((LANES, (n_pairs + 2 * FOX_KV_HEADS) * LANES), np.float32)
    const = np.zeros((1, sel.shape[1]), np.float32)
    klo0 = n_pairs * LANES
    khi0 = klo0 + FOX_KV_HEADS * LANES
    for part in range(3):
        src = part * FOX_HEADS
        for i in range(n_pairs):
            sel[src + 2 * i, i * LANES + part] = 1.0
            sel[src + 2 * i + 1, i * LANES + 3 + part] = 1.0
            const[0, i * LANES + 6 + 3 * (i % 2) + part] = 1.0
        for g in range(FOX_KV_HEADS):
            const[0, klo0 + g * LANES + part] = 1.0
            const[0, khi0 + g * LANES + 3 + part] = 1.0
            sel[src + 4 * g, klo0 + g * LANES + 6 + part] = -1.0
            sel[src + 4 * g + 2, klo0 + g * LANES + 9 + part] = -1.0
            sel[src + 4 * g + 1, khi0 + g * LANES + 6 + part] = -1.0
            sel[src + 4 * g + 3, khi0 + g * LANES + 9 + part] = -1.0
    return jnp.asarray(sel, BF16), jnp.asarray(const, F32)


def _fox_proj(x3, mods3, norm_g, w_in, b_f, nb, tt, prompt):
    n_b, n_t, d = x3.shape
    m = nb * tt
    n_tt = n_t // tt
    rows = n_b * n_t
    x_spec = pl.BlockSpec((nb, tt, d), lambda i, j: (i, j, 0))

    def mod_spec(col):
        return pl.BlockSpec((nb, 1, d), lambda i, j, col=col: (i, 0, col))

    def const_spec(shape):
        return pl.BlockSpec(shape, lambda i, j: (0,) * len(shape))

    def row_spec(width):
        return pl.BlockSpec((m, width), lambda i, j: (i * n_tt + j, 0))

    kw = FOX_KV_HEADS * FOX_HEAD_DIM
    ops = [x3, mods3, mods3, norm_g.reshape(1, 1, d), w_in, b_f]
    specs = [x_spec, mod_spec(3), mod_spec(4), const_spec((1, 1, d)), const_spec(w_in.shape),
             const_spec(b_f.shape)]
    scratch = []
    if prompt:
        sel, selc = _fox_aug_selectors()
        ops += [sel, selc]
        specs += [const_spec(sel.shape), const_spec(selc.shape)]
        widths = [(FOX_HEADS // 2 * 256, BF16), (FOX_KV_HEADS * 256, BF16), (FOX_KV_HEADS * 256, BF16),
                  (FOX_KV_HEADS * 128, BF16), (FOX_KV_HEADS * 128, BF16)]
        scratch = [pltpu.VMEM((1, LANES), F32)]
    else:
        widths = [(FOX_HEADS * FOX_HEAD_DIM, BF16)]
    widths += [(kw, F32), (kw, F32), (FOX_HEADS, F32)]
    return pl.pallas_call(
        functools.partial(_fox_proj_kernel, prompt=prompt),
        grid=(n_b // nb, n_tt),
        in_specs=specs,
        out_specs=[row_spec(wd) for wd, _ in widths],
        out_shape=[jax.ShapeDtypeStruct((rows, wd), dt) for wd, dt in widths],
        scratch_shapes=scratch,
        compiler_params=_params("parallel", "arbitrary"),
        name="fox_proj",
    )(*ops)


def _fox_engine(ins, o_ref, scratch, n_pg, n_sub, dec_seq):
    q_ref, lfn_ref, kn_ref, vn_ref = ins
    k_buf, v_buf, lf_buf, kbuf, vbuf, m_ref, l_ref, acc_ref, run_ref, ncol_ref = scratch
    _, rows, width = q_ref.shape

    def add_head_rows(s, per_head):
        return jnp.concatenate([s[hd * dec_seq:(hd + 1) * dec_seq, :] + per_head[hd:hd + 1, :]
                                for hd in range(FOX_HEADS)], axis=0)

    def begin(sl):
        q = q_ref[sl]
        lfn = lfn_ref[sl]
        n_new = lfn.shape[0]
        row = _iota((n_new, LANES), 0)
        n_cum = jnp.zeros((n_new, LANES), F32)
        for t in range(dec_seq):
            n_cum = n_cum + jnp.where(row >= t, lfn[t:t + 1, :], 0.0)
        n_cum = n_cum * LOG2E
        eye = (_iota((LANES, LANES), 0) == _iota((LANES, LANES), 1)).astype(BF16)
        hi, mid, lo = _split3(n_cum)
        n_cum_t = _dot_nt(eye, hi) + _dot_nt(eye, mid) + _dot_nt(eye, lo)
        lane = _iota((dec_seq, LANES), 1)
        for hd in range(FOX_HEADS):
            col = jnp.sum(jnp.where(lane == hd, n_cum[:dec_seq, :], 0.0), axis=1, keepdims=True)
            ncol_ref[hd * dec_seq:(hd + 1) * dec_seq, :] = jnp.broadcast_to(col, (dec_seq, LANES))
        s = add_head_rows(_dot_nt(q, kn_ref[sl]), -n_cum_t)
        s = jnp.where(_iota((rows, n_new), 1) <= _iota((rows, n_new), 0) % dec_seq, s, -jnp.inf)
        m_new, _, l_new, p_b = _online_softmax(
            s, jnp.full((rows, LANES), -jnp.inf, F32), jnp.zeros((rows, LANES), F32), ncol_ref[...])
        m_ref[...] = m_new
        l_ref[...] = l_new
        acc_ref[...] = _dot(p_b, vn_ref[sl])
        run_ref[...] = jnp.zeros(run_ref.shape, F32)

    sub = n_pg // n_sub
    span = sub * PAGE_SIZE

    def consume(sl, slot):
        q = q_ref[sl]
        later = (_iota((LANES, LANES), 0) > _iota((LANES, LANES), 1)).astype(BF16)
        suffix_and_total = jnp.concatenate([later, jnp.ones((LANES, LANES), BF16)], axis=1)
        lft = lf_buf[slot].reshape(n_pg * FOX_HEADS, PAGE_SIZE)
        both = _dot3(lft, suffix_and_total) * LOG2E
        run = run_ref[...]
        bias_pages = [None] * n_pg
        for j in reversed(range(n_pg)):
            bias_pages[j] = run + both[j * FOX_HEADS:(j + 1) * FOX_HEADS, :LANES]
            run = run + both[j * FOX_HEADS:(j + 1) * FOX_HEADS, LANES:]
        run_ref[...] = run
        ncol = ncol_ref[...]

        def scores(b):
            for j in range(b * sub, (b + 1) * sub):
                kbuf[:, j * PAGE_SIZE:(j + 1) * PAGE_SIZE] = k_buf[slot, j].astype(BF16)
                vbuf[:, j * PAGE_SIZE:(j + 1) * PAGE_SIZE] = v_buf[slot, j].astype(BF16)
            return _dot(q, kbuf[:, b * span:(b + 1) * span])

        def update(b, s):
            bias = jnp.concatenate(bias_pages[b * sub:(b + 1) * sub], axis=1)
            m_new, alpha, l_new, p_b = _online_softmax(add_head_rows(s, bias), m_ref[...], l_ref[...],
                                                       ncol)
            l_ref[...] = l_new
            acc_ref[...] = (jnp.concatenate([alpha] * (width // LANES), axis=1) * acc_ref[...]
                            + _dot_nt(p_b, vbuf[:, b * span:(b + 1) * span]))
            m_ref[...] = m_new

        _pipelined(n_sub, scores, update)

    def finish(sl):
        o_ref[sl] = acc_ref[...] / jnp.sum(l_ref[...], axis=1, keepdims=True)

    return begin, consume, finish


def _fox_scratch(rows, width, n_pg):
    return [pltpu.VMEM((2, n_pg, width, PAGE_SIZE), F32),
            pltpu.VMEM((2, n_pg, width, PAGE_SIZE), F32),
            pltpu.VMEM((2, n_pg, FOX_HEADS, PAGE_SIZE), F32),
            pltpu.VMEM((width, n_pg * PAGE_SIZE), BF16),
            pltpu.VMEM((width, n_pg * PAGE_SIZE), BF16),
            pltpu.VMEM((rows, LANES), F32), pltpu.VMEM((rows, LANES), F32),
            pltpu.VMEM((rows, width), F32),
            pltpu.VMEM((FOX_HEADS, LANES), F32),
            pltpu.VMEM((rows, LANES), F32)]


def _ffn_paged_kernel(pt_ref, *refs, fox, n_pg, n_sub, dec_seq, n_ff, chunk, n_chunks):
    x_ref, sh_ref, sc_ref, gt_ref, g_ref, wgu_ref, wdn_ref = refs[:7]
    n_att = 4
    att_in = refs[7:7 + n_att]
    n_cache = 3 if fox else 2
    caches = refs[7 + n_att:7 + n_att + n_cache]
    out_ref, o_ref, acc_ref, hb_ref, sems = refs[7 + n_att + n_cache:12 + n_att + n_cache]
    scratch = refs[12 + n_att + n_cache:]
    if fox:
        begin, consume, finish = _fox_engine(att_in, o_ref, scratch, n_pg, n_sub, dec_seq)
    else:
        begin, consume, finish = _mla_engine(att_in, o_ref, scratch, n_pg, n_sub)
    start, wait = _page_stream(pt_ref, caches, scratch[:n_cache], sems, n_pg)

    nb, tt, d = x_ref.shape
    m = nb * tt
    d_ff = wdn_ref.shape[0]
    h = _rms(x_ref[...], g_ref[...]) * (1.0 + sc_ref[...]) + sh_ref[...]
    hb_ref[...] = h.reshape(m, d).astype(BF16)

    def ffn_chunk(c):
        lo = c * chunk
        hb = hb_ref[...]
        gate = _dot(hb, wgu_ref[:, lo:lo + chunk])
        up = _dot(hb, wgu_ref[:, d_ff + lo:d_ff + lo + chunk])
        y = _dot((gate * _sigmoid(gate) * up).astype(BF16), wdn_ref[lo:lo + chunk, :])
        if c == 0:
            acc_ref[...] = y
        else:
            acc_ref[...] += y

    seq_per_step = att_in[0].shape[0]
    g = pl.program_id(0) * pl.num_programs(1) + pl.program_id(1)
    n_g = pl.num_programs(0) * pl.num_programs(1)
    order = [(sl, k) for sl in range(seq_per_step) for k in range(n_chunks)]

    def chunk_no(k):
        return n_chunks - 1 - k if fox else k

    @pl.when(g == 0)
    def _():
        start(0, chunk_no(0), 0)

    for idx, (sl, k) in enumerate(order):
        slot = idx % 2
        if idx + 1 < len(order):
            sl2, k2 = order[idx + 1]
            start(g * seq_per_step + sl2, chunk_no(k2), 1 - slot)
        else:
            @pl.when(g + 1 < n_g)
            def _():
                start((g + 1) * seq_per_step, chunk_no(0), 1 - slot)
        wait(g * seq_per_step + sl, chunk_no(k), slot)
        if k == 0:
            begin(sl)
        consume(sl, slot)
        if k == n_chunks - 1:
            finish(sl)
        for c in range(n_ff):
            if c * len(order) // n_ff == idx:
                ffn_chunk(c)

    out_ref[...] = x_ref[...] + (0.5 * gt_ref[...]) * acc_ref[...].reshape(nb, tt, d)


def _ffn_paged(x3, mods3, norm_g, w_gu, w_dn, tt, page_table, att_in, caches, fox, n_pg, n_sub, dec_seq):
    n_b, n_t, d = x3.shape
    d_ff = w_dn.shape[0]
    chunk = 256
    n_steps = n_b * (n_t // tt)
    n_seq, rows, _ = att_in[0].shape
    assert n_seq % n_steps == 0, "sample sequences must split evenly over the prompt tiles"
    seq_per_step = n_seq // n_steps
    n_chunks = page_table.shape[1] // n_pg
    assert (seq_per_step * n_chunks) % 2 == 0
    n_j = n_t // tt
    x_spec = pl.BlockSpec((1, tt, d), lambda i, j, pt: (i, j, 0))

    def mod_spec(col):
        return pl.BlockSpec((1, 1, d), lambda i, j, pt, col=col: (i, 0, col))

    def const_spec(shape):
        return pl.BlockSpec(shape, lambda i, j, pt: (0,) * len(shape), pipeline_mode=pl.Buffered(1))

    def seq_spec(a):
        return pl.BlockSpec((seq_per_step,) + a.shape[1:], lambda i, j, pt: (i * n_j + j, 0, 0))

    hbm_spec = pl.BlockSpec(memory_space=pl.ANY)
    if fox:
        out_w, out_dt = att_in[0].shape[2], F32
        att_scratch = _fox_scratch(rows, out_w, n_pg)
    else:
        out_w, out_dt = MLA_KV_LORA, BF16
        att_scratch = _mla_scratch(rows, n_pg)
    grid_spec = pltpu.PrefetchScalarGridSpec(
        num_scalar_prefetch=1,
        grid=(n_b, n_j),
        in_specs=[x_spec, mod_spec(0), mod_spec(1), mod_spec(2), const_spec((1, 1, d)),
                  const_spec(w_gu.shape), const_spec(w_dn.shape)]
        + [seq_spec(a) for a in att_in] + [hbm_spec] * len(caches),
        out_specs=[x_spec, pl.BlockSpec((seq_per_step, rows, out_w), lambda i, j, pt: (i * n_j + j, 0, 0))],
        scratch_shapes=[pltpu.VMEM((tt, d), F32), pltpu.VMEM((tt, d), BF16),
                        pltpu.SemaphoreType.DMA((len(caches), 2))] + att_scratch,
    )
    return pl.pallas_call(
        functools.partial(_ffn_paged_kernel, fox=fox, n_pg=n_pg, n_sub=n_sub, dec_seq=dec_seq,
                          n_ff=d_ff // chunk, chunk=chunk, n_chunks=n_chunks),
        grid_spec=grid_spec,
        out_shape=[jax.ShapeDtypeStruct(x3.shape, F32),
                   jax.ShapeDtypeStruct((n_seq, rows, out_w), out_dt)],
        compiler_params=_params("arbitrary", "arbitrary"),
        name="ffn_paged",
    )(page_table, x3, mods3, mods3, mods3, norm_g.reshape(1, 1, d), w_gu, w_dn, *att_in, *caches)


def _rope_tables(pos):
    half = MLA_ROPE // 2
    inv_freq = ROPE_THETA ** (-jnp.arange(half, dtype=F32) / half)
    ang = pos.astype(F32)[:, None] * inv_freq[None, :]
    cos, sin, zero = jnp.cos(ang), jnp.sin(ang), jnp.zeros_like(ang)
    return (jnp.concatenate([cos, cos, zero, zero], axis=1),
            jnp.concatenate([-sin, sin, zero, zero], axis=1))


def _mla_weights(w_in, g_q, w_uq, g_kv, w_uk, w_uv, w_o):
    lat = MLA_Q_LORA + MLA_KV_LORA
    kr_cols = np.concatenate([np.arange(lat, lat + MLA_ROPE)] * 2)
    in_cols = np.concatenate([np.arange(lat), kr_cols])
    uq_cols = []
    for hd in range(MLA_HEADS):
        b0 = hd * (MLA_NOPE + MLA_ROPE)
        rope_cols = np.arange(b0 + MLA_NOPE, b0 + MLA_NOPE + MLA_ROPE)
        uq_cols += [np.arange(b0, b0 + MLA_NOPE), rope_cols, rope_cols]
    return dict(
        w_in=w_in[:, in_cols].astype(BF16),
        g_q=g_q.reshape(1, -1), g_kv=g_kv.reshape(1, -1),
        w_uq=w_uq[:, np.concatenate(uq_cols)].astype(BF16),
        w_uk=w_uk.reshape(MLA_KV_LORA, MLA_HEADS * MLA_NOPE).astype(BF16),
        w_uv=w_uv.reshape(MLA_KV_LORA, MLA_HEADS * MLA_V).astype(BF16),
        w_ukt=jnp.transpose(w_uk, (1, 2, 0)).astype(BF16),
        w_uv3=jnp.transpose(w_uv, (1, 0, 2)).astype(BF16),
        w_o=w_o.astype(BF16),
    )


def _pad_rows(a, rows):
    return jnp.pad(a, ((0, 0), (0, rows - a.shape[1])) + ((0, 0),) * (a.ndim - 2))


def _pages_per_step(n_pages, want):
    n = min(want, n_pages)
    while n_pages % n:
        n -= 1
    return n


def kernel(x_prompt, x_sample, c_prompt, c_sample, cache_mla_ckv, cache_mla_krope, cache_fox_k, cache_fox_v, cache_fox_logf, page_table, ada_w, ada_b, norm_g, ffn_w_gu, ffn_w_dn, final_g, mla_w_in, mla_g_q, mla_w_uq, mla_g_kv, mla_w_uk, mla_w_uv, mla_w_o, fox_w_in, fox_b_f, fox_w_o):
    n_p, seq, d = x_prompt.shape
    n_s, dec_seq, _ = x_sample.shape
    n_pages = page_table.shape[1]
    past_len = n_pages * PAGE_SIZE
    tile = min(512, seq)
    tk = min(1024, seq)
    n_pad = LANES

    mods = _adaln(jnp.concatenate([c_prompt, c_sample], axis=0), ada_w, ada_b)
    w_gu = ffn_w_gu.astype(BF16)
    w_dn = ffn_w_dn.astype(BF16)
    xp, xs = x_prompt, x_sample

    mp = mods[0, :n_p].reshape(n_p, 1, -1)
    ms = mods[0, n_p:].reshape(n_s, 1, -1)
    w = _mla_weights(mla_w_in[0], mla_g_q[0], mla_w_uq[0], mla_g_kv[0], mla_w_uk[0], mla_w_uv[0],
                     mla_w_o[0])
    xs = _ffn(xs, ms, 0, norm_g[0, 0], w_gu[0, 0], w_dn[0, 0], n_s, dec_seq)
    cos_s, sin_s = _rope_tables(past_len + jnp.arange(dec_seq, dtype=jnp.int32))
    cos_s, sin_s = jnp.tile(cos_s, (n_s, 1)), jnp.tile(sin_s, (n_s, 1))
    q_lat, q_rope, s_ckv, s_kr = _mla_proj(xs, ms, norm_g[0, 1], w, cos_s, sin_s, n_s, dec_seq,
                                           absorb=True)
    rows = dec_seq * MLA_HEADS
    n_pg = _pages_per_step(n_pages, 16)
    n_sub = 2 if n_pg % 2 == 0 else 1
    xp, o_lat = _ffn_paged(
        xp, mp, norm_g[0, 0], w_gu[0, 0], w_dn[0, 0], tile, page_table,
        [q_lat.reshape(n_s, rows, MLA_KV_LORA), q_rope.reshape(n_s, rows, LANES),
         _pad_rows(s_ckv.reshape(n_s, dec_seq, MLA_KV_LORA), n_pad).astype(BF16),
         _pad_rows(s_kr.reshape(n_s, dec_seq, MLA_ROPE), n_pad).astype(BF16)],
        [cache_mla_ckv[0], jnp.swapaxes(cache_mla_krope[0], 1, 2)],
        False, n_pg, n_sub, dec_seq)
    o_s = _mla_uv(o_lat.reshape(n_s * dec_seq, MLA_HEADS * MLA_KV_LORA), w["w_uv3"])

    cos_p, sin_p = _rope_tables(jnp.arange(seq, dtype=jnp.int32))
    q, k, v, p_ckv, p_kr = _mla_proj(xp, mp, norm_g[0, 1], w, cos_p, sin_p, 1, tile, absorb=False)
    units = [(slice(h * 256, (h + 1) * 256), 0, slice(h * 256, (h + 1) * 256), 1,
              slice(h * 128, (h + 1) * 128)) for h in range(MLA_HEADS)]
    o_p = _flash(q, [k, v], units, [[h] for h in range(MLA_HEADS)], n_p, seq, tile, tk)

    xp = _ffn(xp, mp, 2, norm_g[0, 2], w_gu[0, 1], w_dn[0, 1], 1, tile, mix=(o_p, w["w_o"]))
    xs = _ffn(xs, ms, 2, norm_g[0, 2], w_gu[0, 1], w_dn[0, 1], n_s, dec_seq, mix=(o_s, w["w_o"]))

    mp = mods[1, :n_p].reshape(n_p, 1, -1)
    ms = mods[1, n_p:].reshape(n_s, 1, -1)
    qw = FOX_HEADS * FOX_HEAD_DIM
    kw = FOX_KV_HEADS * FOX_HEAD_DIM
    fw_in = jnp.pad(fox_w_in[0], ((0, 0), (0, LANES - FOX_HEADS))).astype(BF16)
    fb = jnp.pad(fox_b_f[0], (0, LANES - FOX_HEADS)).reshape(1, LANES)
    fw_o = fox_w_o[0].astype(BF16)
    xs = _ffn(xs, ms, 0, norm_g[1, 0], w_gu[1, 0], w_dn[1, 0], n_s, dec_seq)
    q_s, s_k, s_v, s_lf = _fox_proj(xs, ms, norm_g[1, 1], fw_in, fb, n_s, dec_seq, prompt=False)
    q4 = jnp.transpose(q_s.reshape(n_s, dec_seq, FOX_HEADS, FOX_HEAD_DIM), (0, 2, 1, 3))
    grp_of_head = jnp.arange(FOX_HEADS) // (FOX_HEADS // FOX_KV_HEADS)
    onehot = (grp_of_head[:, None] == jnp.arange(FOX_KV_HEADS)[None, :]).astype(BF16)
    q_all = (q4[:, :, :, None, :] * onehot[None, :, None, :, None]).reshape(
        n_s, FOX_HEADS * dec_seq, kw)
    lf_new = jnp.pad(s_lf.reshape(n_s, dec_seq, FOX_HEADS),
                     ((0, 0), (0, n_pad - dec_seq), (0, LANES - FOX_HEADS)))
    cache_lft = jnp.swapaxes(cache_fox_logf[0], 1, 2)
    cache_kt = jnp.transpose(cache_fox_k[0], (0, 2, 3, 1)).reshape(-1, kw, PAGE_SIZE)
    cache_vt = jnp.transpose(cache_fox_v[0], (0, 2, 3, 1)).reshape(-1, kw, PAGE_SIZE)
    xp, o_all = _ffn_paged(
        xp, mp, norm_g[1, 0], w_gu[1, 0], w_dn[1, 0], tile, page_table,
        [q_all, lf_new,
         _pad_rows(s_k.reshape(n_s, dec_seq, kw), n_pad).astype(BF16),
         _pad_rows(s_v.reshape(n_s, dec_seq, kw), n_pad).astype(BF16)],
        [cache_kt, cache_vt, cache_lft], True, n_pg, n_sub, dec_seq)
    o5 = o_all.reshape(n_s, FOX_HEADS, dec_seq, FOX_KV_HEADS, FOX_HEAD_DIM)
    o_sel = jnp.take_along_axis(o5, grp_of_head[None, :, None, None, None], axis=3)[:, :, :, 0, :]
    o_s = jnp.transpose(o_sel, (0, 2, 1, 3)).reshape(n_s * dec_seq, qw).astype(BF16)

    qa, klo, khi, vlo, vhi, p_k, p_v, p_lf = _fox_proj(xp, mp, norm_g[1, 1], fw_in, fb, 1, tile,
                                                       prompt=True)
    units = []
    for h in range(FOX_HEADS):
        i, par, g = h // 2, h % 2, h // (FOX_HEADS // FOX_KV_HEADS)
        units.append((slice(i * 256, (i + 1) * 256), par, slice(g * 256, (g + 1) * 256), 2 + par,
                      slice(g * 128, (g + 1) * 128)))
    o_p = _flash(qa, [klo, khi, vlo, vhi], units, [[2 * i, 2 * i + 1] for i in range(FOX_HEADS // 2)],
                 n_p, seq, tile, tk)

    xp = _ffn(xp, mp, 2, norm_g[1, 2], w_gu[1, 1], w_dn[1, 1], 1, tile, mix=(o_p, fw_o),
              final_g=final_g)
    xs = _ffn(xs, ms, 2, norm_g[1, 2], w_gu[1, 1], w_dn[1, 1], n_s, dec_seq, mix=(o_s, fw_o),
              final_g=final_g)

    def prompt_state(a, *tail):
        return a.reshape(1, n_p, seq, *tail)

    def sample_state(a, *tail):
        return a.reshape(1, n_s, dec_seq, *tail)

    return (xp, xs,
            prompt_state(p_ckv, MLA_KV_LORA), prompt_state(p_kr, MLA_ROPE),
            prompt_state(p_k, FOX_KV_HEADS, FOX_HEAD_DIM), prompt_state(p_v, FOX_KV_HEADS, FOX_HEAD_DIM),
            prompt_state(p_lf, FOX_HEADS),
            sample_state(s_ckv, MLA_KV_LORA), sample_state(s_kr, MLA_ROPE),
            sample_state(s_k, FOX_KV_HEADS, FOX_HEAD_DIM), sample_state(s_v, FOX_KV_HEADS, FOX_HEAD_DIM),
            sample_state(s_lf, FOX_HEADS))
```

```python
import functools
import math

import numpy as np
import jax
import jax.numpy as jnp
from jax import lax
from jax.experimental import pallas as pl
from jax.experimental.pallas import tpu as pltpu

F32 = jnp.float32
BF16 = jnp.bfloat16

LANES = 128
NORM_EPS = 1e-6
PAGE_SIZE = 128
D_FF = 2816
MLA_HEADS = 8
MLA_NOPE = 128
MLA_ROPE = 64
MLA_V = 128
MLA_Q_LORA = 512
MLA_KV_LORA = 256
ROPE_THETA = 10000.0
MLA_SCALE = 1.0 / math.sqrt(MLA_NOPE + MLA_ROPE)
FOX_HEADS = 16
FOX_KV_HEADS = 4
FOX_HEAD_DIM = 64
FOX_SCALE = 1.0 / math.sqrt(FOX_HEAD_DIM)
LOG2E = math.log2(math.e)
VMEM_LIMIT_BYTES = 60 * 1024 * 1024
PAGE_SLOTS = 4


def _params(*sem):
    return pltpu.CompilerParams(dimension_semantics=sem, vmem_limit_bytes=VMEM_LIMIT_BYTES)


def _dot(a, b):
    return jnp.dot(a, b, preferred_element_type=F32)


def _dot_nt(a, b):
    return lax.dot_general(a, b, (((1,), (1,)), ((), ())), preferred_element_type=F32)


def _sigmoid(x):
    return 1.0 / (1.0 + jnp.exp(-x))


def _rms(x, g):
    return x * lax.rsqrt(jnp.mean(x * x, axis=-1, keepdims=True) + NORM_EPS) * g


def _split3(x):
    hi = x.astype(BF16)
    r = x - hi.astype(F32)
    mid = r.astype(BF16)
    lo = (r - mid.astype(F32)).astype(BF16)
    return hi, mid, lo


def _dot3(x, m):
    hi, mid, lo = _split3(x)
    return _dot(hi, m) + _dot(mid, m) + _dot(lo, m)


def _iota(shape, dim):
    return lax.broadcasted_iota(jnp.int32, shape, dim)


def _modulated(x_ref, sh_ref, sc_ref, g_ref):
    x = x_ref[...]
    return x, _rms(x, g_ref[...]) * (1.0 + sc_ref[...]) + sh_ref[...]


def _adaln_kernel(c_ref, w_ref, b_ref, o_ref):
    c = c_ref[...]
    a = (c * _sigmoid(c)).astype(BF16)
    o_ref[0] = _dot(a, w_ref[0].astype(BF16)) + b_ref[0]


def _adaln(c_all, ada_w, ada_b):
    n_layers, d, n = ada_w.shape
    rows = c_all.shape[0]
    tn = 1536 if n % 1536 == 0 else n
    return pl.pallas_call(
        _adaln_kernel,
        grid=(n_layers, n // tn),
        in_specs=[
            pl.BlockSpec((rows, d), lambda l, j: (0, 0)),
            pl.BlockSpec((1, d, tn), lambda l, j: (l, 0, j)),
            pl.BlockSpec((1, 1, tn), lambda l, j: (l, 0, j)),
        ],
        out_specs=pl.BlockSpec((1, rows, tn), lambda l, j: (l, 0, j)),
        out_shape=jax.ShapeDtypeStruct((n_layers, rows, n), F32),
        compiler_params=_params("parallel", "parallel"),
        name="adaln",
    )(c_all, ada_w, ada_b.reshape(n_layers, 1, n))


def _ffn_kernel(*refs, has_mix, final, n_chunks, chunk):
    it = iter(refs)
    x_ref, sh_ref, sc_ref, gt_ref, g_ref, wgu_ref, wdn_ref = (next(it) for _ in range(7))
    if has_mix:
        o_ref, wo_ref, gm_ref = next(it), next(it), next(it)
    if final:
        fg_ref = next(it)
    out_ref, acc_ref = next(it), next(it)
    nb, tt, d = x_ref.shape
    m = nb * tt
    x = x_ref[...]
    if has_mix:
        x = x + gm_ref[...] * _dot(o_ref[...], wo_ref[...]).reshape(nb, tt, d)
    h = _rms(x, g_ref[...]) * (1.0 + sc_ref[...]) + sh_ref[...]
    hb = h.reshape(m, d).astype(BF16)
    d_ff = wdn_ref.shape[0]
    for c in range(n_chunks):
        lo = c * chunk
        gate = _dot(hb, wgu_ref[:, lo:lo + chunk])
        up = _dot(hb, wgu_ref[:, d_ff + lo:d_ff + lo + chunk])
        act = (gate * _sigmoid(gate) * up).astype(BF16)
        y = _dot(act, wdn_ref[lo:lo + chunk, :])
        if c == 0:
            acc_ref[...] = y
        else:
            acc_ref[...] += y
    y = x + (0.5 * gt_ref[...]) * acc_ref[...].reshape(nb, tt, d)
    if final:
        y = _rms(y, fg_ref[...])
    out_ref[...] = y


def _ffn(x3, mods3, sub, norm_g, w_gu, w_dn, nb, tt, mix=None, final_g=None):
    n_b, n_t, d = x3.shape
    d_ff = w_dn.shape[0]
    chunk = 256
    grid = (n_b // nb, n_t // tt)
    m = nb * tt
    x_spec = pl.BlockSpec((nb, tt, d), lambda i, j: (i, j, 0))

    def mod_spec(col):
        return pl.BlockSpec((nb, 1, d), lambda i, j, col=col: (i, 0, col))

    def const_spec(shape):
        return pl.BlockSpec(shape, lambda i, j: (0,) * len(shape), pipeline_mode=pl.Buffered(1))

    ops = [x3, mods3, mods3, mods3, norm_g.reshape(1, 1, d), w_gu, w_dn]
    specs = [x_spec, mod_spec(3 * sub), mod_spec(3 * sub + 1), mod_spec(3 * sub + 2),
             const_spec((1, 1, d)), const_spec(w_gu.shape), const_spec(w_dn.shape)]
    if mix is not None:
        o2, w_o = mix
        n_tt = n_t // tt
        ops += [o2, w_o, mods3]
        specs += [pl.BlockSpec((m, o2.shape[1]), lambda i, j: (i * n_tt + j, 0)),
                  const_spec(w_o.shape), mod_spec(3 * 1 + 2)]
    if final_g is not None:
        ops.append(final_g.reshape(1, 1, d))
        specs.append(const_spec((1, 1, d)))
    return pl.pallas_call(
        functools.partial(_ffn_kernel, has_mix=mix is not None, final=final_g is not None,
                          n_chunks=d_ff // chunk, chunk=chunk),
        grid=grid,
        in_specs=specs,
        out_specs=x_spec,
        out_shape=jax.ShapeDtypeStruct(x3.shape, F32),
        scratch_shapes=[pltpu.VMEM((m, d), F32)],
        compiler_params=_params("parallel", "parallel"),
        name="ffn",
    )(*ops)


def _mla_proj_kernel(*refs, absorb):
    it = iter(refs)
    x_ref, sh_ref, sc_ref, g_ref, win_ref, gq_ref, gkv_ref, wuq_ref, cos_ref, sin_ref = (
        next(it) for _ in range(10))
    if absorb:
        wukt_ref = next(it)
        qlat_ref, qrope_ref, ckv_ref, kr_ref = (next(it) for _ in range(4))
    else:
        wuk_ref, wuv_ref = next(it), next(it)
        q_ref, k_ref, v_ref, ckv_ref, kr_ref = (next(it) for _ in range(5))
    nb, tt, d = x_ref.shape
    m = nb * tt
    _, h = _modulated(x_ref, sh_ref, sc_ref, g_ref)
    hb = h.reshape(m, d).astype(BF16)
    proj = _dot(hb, win_ref[...])
    c_q = _rms(proj[:, :MLA_Q_LORA], gq_ref[...])
    c_kv = _rms(proj[:, MLA_Q_LORA:MLA_Q_LORA + MLA_KV_LORA], gkv_ref[...])
    ckv_ref[...] = c_kv
    cos_a = cos_ref[...]
    sin_b = sin_ref[...]

    def rope(z):
        return z * cos_a + pltpu.roll(z, 32, 1) * sin_b

    kr = rope(proj[:, MLA_Q_LORA + MLA_KV_LORA:])
    kr_ref[...] = kr[:, :MLA_ROPE]
    qf = _dot(c_q.astype(BF16), wuq_ref[...]) * (MLA_SCALE * LOG2E)
    ckv_b = c_kv.astype(BF16)
    if not absorb:
        kn = _dot(ckv_b, wuk_ref[...])
        v_ref[...] = _dot(ckv_b, wuv_ref[...]).astype(BF16)
        kr_b = kr.astype(BF16)
    for hd in range(MLA_HEADS):
        b0 = hd * 256
        qn = qf[:, b0:b0 + 128]
        qr = rope(qf[:, b0 + 128:b0 + 256])
        if absorb:
            qlat_ref[:, b0:b0 + 256] = _dot(qn.astype(BF16), wukt_ref[hd]).astype(BF16)
            qrope_ref[:, hd * 128:(hd + 1) * 128] = qr.astype(BF16)
        else:
            q_ref[:, b0:b0 + 128] = qn.astype(BF16)
            q_ref[:, b0 + 128:b0 + 256] = qr.astype(BF16)
            k_ref[:, b0:b0 + 128] = kn[:, hd * 128:(hd + 1) * 128].astype(BF16)
            k_ref[:, b0 + 128:b0 + 256] = kr_b


def _mla_proj(x3, mods3, norm_g, w, cos_a, sin_b, nb, tt, absorb):
    n_b, n_t, d = x3.shape
    m = nb * tt
    n_tt = n_t // tt
    rows = n_b * n_t
    grid = (n_b // nb, n_tt)
    x_spec = pl.BlockSpec((nb, tt, d), lambda i, j: (i, j, 0))

    def mod_spec(col):
        return pl.BlockSpec((nb, 1, d), lambda i, j, col=col: (i, 0, col))

    def const_spec(shape):
        return pl.BlockSpec(shape, lambda i, j: (0,) * len(shape))

    def row_spec(width):
        return pl.BlockSpec((m, width), lambda i, j: (i * n_tt + j, 0))

    tab_spec = pl.BlockSpec((m, LANES), lambda i, j: (j, 0))
    ops = [x3, mods3, mods3, norm_g.reshape(1, 1, d), w["w_in"], w["g_q"], w["g_kv"], w["w_uq"],
           cos_a, sin_b]
    specs = [x_spec, mod_spec(3), mod_spec(4), const_spec((1, 1, d)), const_spec(w["w_in"].shape),
             const_spec(w["g_q"].shape), const_spec(w["g_kv"].shape), const_spec(w["w_uq"].shape),
             tab_spec, tab_spec]
    if absorb:
        ops.append(w["w_ukt"])
        specs.append(const_spec(w["w_ukt"].shape))
        widths = [(MLA_HEADS * 256, BF16), (MLA_HEADS * 128, BF16)]
    else:
        ops += [w["w_uk"], w["w_uv"]]
        specs += [const_spec(w["w_uk"].shape), const_spec(w["w_uv"].shape)]
        widths = [(MLA_HEADS * 256, BF16), (MLA_HEADS * 256, BF16), (MLA_HEADS * MLA_V, BF16)]
    widths += [(MLA_KV_LORA, F32), (MLA_ROPE, F32)]
    return pl.pallas_call(
        functools.partial(_mla_proj_kernel, absorb=absorb),
        grid=grid,
        in_specs=specs,
        out_specs=[row_spec(wd) for wd, _ in widths],
        out_shape=[jax.ShapeDtypeStruct((rows, wd), dt) for wd, dt in widths],
        compiler_params=_params("parallel", "parallel"),
        name="mla_proj",
    )(*ops)


def _online_softmax(s, m_prev, l_prev, row_bias=None):
    blocks = [s[:, b * LANES:(b + 1) * LANES] for b in range(s.shape[1] // LANES)]
    if row_bias is not None:
        blocks = [blk + row_bias for blk in blocks]
    mx = blocks[0]
    for blk in blocks[1:]:
        mx = jnp.maximum(mx, blk)
    m_new = jnp.maximum(m_prev, jnp.max(mx, axis=1, keepdims=True))
    alpha = jnp.exp2(m_prev - m_new)
    ps = [jnp.exp2(blk - m_new) for blk in blocks]
    psum = ps[0]
    for p in ps[1:]:
        psum = psum + p
    p_b = jnp.concatenate([p.astype(BF16) for p in ps], axis=1) if len(ps) > 1 else ps[0].astype(BF16)
    return m_new, alpha, alpha * l_prev + psum, p_b


def _pipelined(n, produce, consume):
    nxt = produce(0)
    for b in range(n):
        cur = nxt
        if b + 1 < n:
            nxt = produce(b + 1)
        consume(b, cur)


def _flash_kernel(*refs, units, outs, n_kv):
    q_ref = refs[0]
    kv = refs[1:1 + n_kv]
    o_ref = refs[1 + n_kv]
    m_ref, l_ref, acc_ref = refs[2 + n_kv:]
    i = pl.program_id(1)
    j = pl.program_id(2)
    tq = q_ref.shape[0]
    tk = kv[0].shape[0]
    j_last = ((i + 1) * tq - 1) // tk

    @pl.when(j == 0)
    def _():
        m_ref[...] = jnp.full(m_ref.shape, -jnp.inf, F32)
        l_ref[...] = jnp.zeros(l_ref.shape, F32)
        acc_ref[...] = jnp.zeros(acc_ref.shape, F32)

    def step(masked, n_keys):
        if masked:
            keep = (i * tq + _iota((tq, n_keys), 0)) >= (j * tk + _iota((tq, n_keys), 1))

        def scores(u):
            qs, ki, ks, _, _ = units[u]
            s = _dot_nt(q_ref[:, qs], kv[ki][:n_keys, ks])
            return jnp.where(keep, s, -jnp.inf) if masked else s

        def update(u, s):
            _, _, _, vi, vs = units[u]
            m_new, alpha, l_new, p_b = _online_softmax(s, m_ref[u], l_ref[u])
            l_ref[u] = l_new
            acc_ref[u] = alpha * acc_ref[u] + _dot(p_b, kv[vi][:n_keys, vs])
            m_ref[u] = m_new

        _pipelined(len(units), scores, update)

    fully_visible = (j + 1) * tk <= i * tq
    partly_visible = jnp.logical_and(jnp.logical_not(fully_visible), j <= j_last)
    half = tk // 2
    first_half_only = j * tk + half > (i + 1) * tq - 1 if half % LANES == 0 else False

    @pl.when(fully_visible)
    def _():
        step(False, tk)

    if half % LANES == 0:
        @pl.when(jnp.logical_and(partly_visible, first_half_only))
        def _():
            step(True, half)

    @pl.when(jnp.logical_and(partly_visible, jnp.logical_not(first_half_only)))
    def _():
        step(True, tk)

    @pl.when(j == j_last)
    def _():
        for b, us in enumerate(outs):
            o = None
            for u in us:
                term = acc_ref[u] / jnp.sum(l_ref[u], axis=1, keepdims=True)
                o = term if o is None else o + term
            o_ref[:, b * LANES:(b + 1) * LANES] = o.astype(o_ref.dtype)


def _flash(q, kvs, units, outs, n_batch, seq, tq, tk):
    nq = seq // tq
    nk = seq // tk
    n_units = len(units)

    def q_map(b, i, j):
        return (b * nq + i, 0)

    def kv_map(b, i, j):
        return (b * nk + jnp.minimum(j, ((i + 1) * tq - 1) // tk), 0)

    out_w = LANES * len(outs)
    return pl.pallas_call(
        functools.partial(_flash_kernel, units=units, outs=outs, n_kv=len(kvs)),
        grid=(n_batch, nq, nk),
        in_specs=[pl.BlockSpec((tq, q.shape[1]), q_map)]
        + [pl.BlockSpec((tk, a.shape[1]), kv_map) for a in kvs],
        out_specs=pl.BlockSpec((tq, out_w), q_map),
        out_shape=jax.ShapeDtypeStruct((q.shape[0], out_w), BF16),
        scratch_shapes=[pltpu.VMEM((n_units, tq, LANES), F32), pltpu.VMEM((n_units, tq, LANES), F32),
                        pltpu.VMEM((n_units, tq, LANES), F32)],
        compiler_params=_params("parallel", "parallel", "arbitrary"),
        name="flash",
    )(q, *kvs)


def _page_stream(pt_ref, caches, bufs, sems, n_pg):
    def copies(s, chunk, slot):
        out = []
        for j in range(n_pg):
            page = pt_ref[s, chunk * n_pg + j]
            for a, (cache, buf) in enumerate(zip(caches, bufs)):
                out.append(pltpu.make_async_copy(cache.at[page], buf.at[slot, j], sems.at[a, slot]))
        return out

    def start(s, chunk, slot):
        for n, cp in enumerate(copies(s, chunk, slot)):
            cp.start(priority=(n // len(caches)) % 2)

    def wait(s, chunk, slot):
        for cp in copies(s, chunk, slot):
            cp.wait()

    return start, wait


def _mla_engine(ins, o_ref, scratch, n_pg, n_sub):
    qlat_ref, qrope_ref, ckn_ref, krn_ref = ins
    ckv_buf, kr_buf, kbuf, krbuf, m_ref, l_ref, acc_ref = scratch
    rows = qlat_ref.shape[1]
    sub = n_pg // n_sub
    width = sub * PAGE_SIZE

    def begin(sl):
        q_lat = qlat_ref[sl]
        ckn = ckn_ref[sl]
        n_new = ckn.shape[0]
        s = _dot_nt(q_lat, ckn) + _dot_nt(qrope_ref[sl][:, :MLA_ROPE], krn_ref[sl])
        s = jnp.where(_iota((rows, n_new), 1) <= _iota((rows, n_new), 0) // MLA_HEADS, s, -jnp.inf)
        m_new, _, l_new, p_b = _online_softmax(
            s, jnp.full((rows, LANES), -jnp.inf, F32), jnp.zeros((rows, LANES), F32))
        m_ref[...] = m_new
        l_ref[...] = l_new
        acc_ref[...] = _dot(p_b, ckn)

    def consume(sl, slot):
        q_lat = qlat_ref[sl]
        q_rope = qrope_ref[sl][:, :MLA_ROPE]

        def scores(b):
            for j in range(b * sub, (b + 1) * sub):
                kbuf[j * PAGE_SIZE:(j + 1) * PAGE_SIZE, :] = ckv_buf[slot, j].astype(BF16)
                krbuf[:, j * PAGE_SIZE:(j + 1) * PAGE_SIZE] = kr_buf[slot, j].astype(BF16)
            return (_dot_nt(q_lat, kbuf[b * width:(b + 1) * width, :])
                    + _dot(q_rope, krbuf[:, b * width:(b + 1) * width]))

        def update(b, s):
            m_new, alpha, l_new, p_b = _online_softmax(s, m_ref[...], l_ref[...])
            l_ref[...] = l_new
            acc_ref[...] = (jnp.concatenate([alpha] * (MLA_KV_LORA // LANES), axis=1) * acc_ref[...]
                            + _dot(p_b, kbuf[b * width:(b + 1) * width, :]))
            m_ref[...] = m_new

        _pipelined(n_sub, scores, update)

    def finish(sl):
        l_tot = jnp.sum(l_ref[...], axis=1, keepdims=True)
        o_ref[sl] = (acc_ref[...] / l_tot).astype(o_ref.dtype)

    return begin, consume, finish


def _mla_scratch(rows, n_pg):
    return [pltpu.VMEM((PAGE_SLOTS, n_pg, PAGE_SIZE, MLA_KV_LORA), F32),
            pltpu.VMEM((PAGE_SLOTS, n_pg, MLA_ROPE, PAGE_SIZE), F32),
            pltpu.VMEM((n_pg * PAGE_SIZE, MLA_KV_LORA), BF16),
            pltpu.VMEM((MLA_ROPE, n_pg * PAGE_SIZE), BF16),
            pltpu.VMEM((rows, LANES), F32), pltpu.VMEM((rows, LANES), F32),
            pltpu.VMEM((rows, MLA_KV_LORA), F32)]


def _mla_uv_kernel(o_ref, wuv_ref, out_ref):
    for hd in range(MLA_HEADS):
        out_ref[:, hd * MLA_V:(hd + 1) * MLA_V] = _dot(
            o_ref[:, hd * MLA_KV_LORA:(hd + 1) * MLA_KV_LORA], wuv_ref[hd]).astype(BF16)


def _mla_uv(o_lat2, w_uv3):
    rows = o_lat2.shape[0]
    return pl.pallas_call(
        _mla_uv_kernel,
        out_shape=jax.ShapeDtypeStruct((rows, MLA_HEADS * MLA_V), BF16),
        compiler_params=pltpu.CompilerParams(vmem_limit_bytes=VMEM_LIMIT_BYTES),
        name="mla_uv",
    )(o_lat2, w_uv3)


def _fox_proj_kernel(*refs, prompt):
    it = iter(refs)
    x_ref, sh_ref, sc_ref, g_ref, win_ref, bf_ref = (next(it) for _ in range(6))
    if prompt:
        sel_ref, selc_ref = next(it), next(it)
        qa_ref, klo_ref, khi_ref, vlo_ref, vhi_ref, k_ref, v_ref, lf_ref, carry_ref = (
            next(it) for _ in range(9))
    else:
        q_ref, k_ref, v_ref, lf_ref = (next(it) for _ in range(4))
    nb, tt, d = x_ref.shape
    m = nb * tt
    qw = FOX_HEADS * FOX_HEAD_DIM
    kw = FOX_KV_HEADS * FOX_HEAD_DIM
    _, h = _modulated(x_ref, sh_ref, sc_ref, g_ref)
    hb = h.reshape(m, d).astype(BF16)
    proj = _dot(hb, win_ref[...])
    q = proj[:, :qw] * (FOX_SCALE * LOG2E)
    k = proj[:, qw:qw + kw]
    v = proj[:, qw + kw:qw + 2 * kw]
    z = proj[:, qw + 2 * kw:] + bf_ref[...]
    logf = jnp.minimum(z, 0.0) - jnp.log(1.0 + jnp.exp(-jnp.abs(z)))
    k_ref[...] = k
    v_ref[...] = v
    lf_ref[...] = logf[:, :FOX_HEADS]
    if not prompt:
        q_ref[...] = q.astype(BF16)
        return

    @pl.when(pl.program_id(1) == 0)
    def _():
        carry_ref[...] = jnp.zeros(carry_ref.shape, F32)

    lane = _iota((m, LANES), 1)
    logf = jnp.where(lane < FOX_HEADS, logf, 0.0)
    tri = (_iota((m, m), 0) >= _iota((m, m), 1)).astype(BF16)
    f_hi, f_mid, f_lo = _split3(logf)
    cum = _dot(tri, f_hi) + _dot(tri, f_mid) + _dot(tri, f_lo) + carry_ref[...]
    carry_ref[...] = cum[m - 1:m, :]
    hi, mid, lo = _split3(cum * LOG2E)
    comb = (hi.astype(F32) + pltpu.roll(mid.astype(F32), FOX_HEADS, 1)
            + pltpu.roll(lo.astype(F32), 2 * FOX_HEADS, 1)).astype(BF16)
    aug = _dot(comb, sel_ref[...]) + selc_ref[...]
    n_pairs = FOX_HEADS // 2
    for i in range(n_pairs):
        qa_ref[:, i * 256:i * 256 + 128] = q[:, i * 128:(i + 1) * 128].astype(BF16)
        qa_ref[:, i * 256 + 128:(i + 1) * 256] = aug[:, i * 128:(i + 1) * 128].astype(BF16)
    low_half = lane < FOX_HEAD_DIM
    a0 = n_pairs * 128
    for grp in range(FOX_KV_HEADS):
        blk = (grp // 2) * 128
        for src, lo_ref, hi_ref, stride, has_aug in ((k, klo_ref, khi_ref, 256, True),
                                                     (v, vlo_ref, vhi_ref, 128, False)):
            same = src[:, blk:blk + 128]
            swapped = pltpu.roll(same, FOX_HEAD_DIM, 1)
            lo_src, hi_src = (same, swapped) if grp % 2 == 0 else (swapped, same)
            b0 = grp * stride
            lo_ref[:, b0:b0 + 128] = jnp.where(low_half, lo_src, 0.0).astype(BF16)
            hi_ref[:, b0:b0 + 128] = jnp.where(low_half, 0.0, hi_src).astype(BF16)
            if has_aug:
                lo_ref[:, b0 + 128:b0 + 256] = aug[:, a0 + grp * 128:a0 + (grp + 1) * 128].astype(BF16)
                hi_ref[:, b0 + 128:b0 + 256] = aug[:, a0 + 512 + grp * 128:
                                                   a0 + 512 + (grp + 1) * 128].astype(BF16)


def _fox_aug_selectors():
    n_pairs = FOX_HEADS // 2
    sel = np.zeros((LANES, (n_pairs + 2 * FOX_KV_HEADS) * LANES), np.float32)
    const = np.zeros((1, sel.shape[1]), np.float32)
    klo0 = n_pairs * LANES
    khi0 = klo0 + FOX_KV_HEADS * LANES
    for part in range(3):
        src = part * FOX_HEADS
        for i in range(n_pairs):
            sel[src + 2 * i, i * LANES + part] = 1.0
            sel[src + 2 * i + 1, i * LANES + 3 + part] = 1.0
            const[0, i * LANES + 6 + 3 * (i % 2) + part] = 1.0
        for g in range(FOX_KV_HEADS):
            const[0, klo0 + g * LANES + part] = 1.0
            const[0, khi0 + g * LANES + 3 + part] = 1.0
            sel[src + 4 * g, klo0 + g * LANES + 6 + part] = -1.0
            sel[src + 4 * g + 2, klo0 + g * LANES + 9 + part] = -1.0
            sel[src + 4 * g + 1, khi0 + g * LANES + 6 + part] = -1.0
            sel[src + 4 * g + 3, khi0 + g * LANES + 9 + part] = -1.0
    return jnp.asarray(sel, BF16), jnp.asarray(const, F32)


def _fox_proj(x3, mods3, norm_g, w_in, b_f, nb, tt, prompt):
    n_b, n_t, d = x3.shape
    m = nb * tt
    n_tt = n_t // tt
    rows = n_b * n_t
    x_spec = pl.BlockSpec((nb, tt, d), lambda i, j: (i, j, 0))

    def mod_spec(col):
        return pl.BlockSpec((nb, 1, d), lambda i, j, col=col: (i, 0, col))

    def const_spec(shape):
        return pl.BlockSpec(shape, lambda i, j: (0,) * len(shape))

    def row_spec(width):
        return pl.BlockSpec((m, width), lambda i, j: (i * n_tt + j, 0))

    kw = FOX_KV_HEADS * FOX_HEAD_DIM
    ops = [x3, mods3, mods3, norm_g.reshape(1, 1, d), w_in, b_f]
    specs = [x_spec, mod_spec(3), mod_spec(4), const_spec((1, 1, d)), const_spec(w_in.shape),
             const_spec(b_f.shape)]
    scratch = []
    if prompt:
        sel, selc = _fox_aug_selectors()
        ops += [sel, selc]
        specs += [const_spec(sel.shape), const_spec(selc.shape)]
        widths = [(FOX_HEADS // 2 * 256, BF16), (FOX_KV_HEADS * 256, BF16), (FOX_KV_HEADS * 256, BF16),
                  (FOX_KV_HEADS * 128, BF16), (FOX_KV_HEADS * 128, BF16)]
        scratch = [pltpu.VMEM((1, LANES), F32)]
    else:
        widths = [(FOX_HEADS * FOX_HEAD_DIM, BF16)]
    widths += [(kw, F32), (kw, F32), (FOX_HEADS, F32)]
    return pl.pallas_call(
        functools.partial(_fox_proj_kernel, prompt=prompt),
        grid=(n_b // nb, n_tt),
        in_specs=specs,
        out_specs=[row_spec(wd) for wd, _ in widths],
        out_shape=[jax.ShapeDtypeStruct((rows, wd), dt) for wd, dt in widths],
        scratch_shapes=scratch,
        compiler_params=_params("parallel", "arbitrary"),
        name="fox_proj",
    )(*ops)


def _fox_engine(ins, o_ref, scratch, n_pg, n_sub, dec_seq):
    q_ref, lfn_ref, kn_ref, vn_ref = ins
    k_buf, v_buf, lf_buf, kbuf, vbuf, m_ref, l_ref, acc_ref, run_ref, ncol_ref = scratch
    _, rows, width = q_ref.shape

    def add_head_rows(s, per_head):
        return jnp.concatenate([s[hd * dec_seq:(hd + 1) * dec_seq, :] + per_head[hd:hd + 1, :]
                                for hd in range(FOX_HEADS)], axis=0)

    def begin(sl):
        q = q_ref[sl]
        lfn = lfn_ref[sl]
        n_new = lfn.shape[0]
        row = _iota((n_new, LANES), 0)
        n_cum = jnp.zeros((n_new, LANES), F32)
        for t in range(dec_seq):
            n_cum = n_cum + jnp.where(row >= t, lfn[t:t + 1, :], 0.0)
        n_cum = n_cum * LOG2E
        eye = (_iota((LANES, LANES), 0) == _iota((LANES, LANES), 1)).astype(BF16)
        hi, mid, lo = _split3(n_cum)
        n_cum_t = _dot_nt(eye, hi) + _dot_nt(eye, mid) + _dot_nt(eye, lo)
        lane = _iota((dec_seq, LANES), 1)
        for hd in range(FOX_HEADS):
            col = jnp.sum(jnp.where(lane == hd, n_cum[:dec_seq, :], 0.0), axis=1, keepdims=True)
            ncol_ref[hd * dec_seq:(hd + 1) * dec_seq, :] = jnp.broadcast_to(col, (dec_seq, LANES))
        s = add_head_rows(_dot_nt(q, kn_ref[sl]), -n_cum_t)
        s = jnp.where(_iota((rows, n_new), 1) <= _iota((rows, n_new), 0) % dec_seq, s, -jnp.inf)
        m_new, _, l_new, p_b = _online_softmax(
            s, jnp.full((rows, LANES), -jnp.inf, F32), jnp.zeros((rows, LANES), F32), ncol_ref[...])
        m_ref[...] = m_new
        l_ref[...] = l_new
        acc_ref[...] = _dot(p_b, vn_ref[sl])
        run_ref[...] = jnp.zeros(run_ref.shape, F32)

    sub = n_pg // n_sub
    span = sub * PAGE_SIZE

    def consume(sl, slot):
        q = q_ref[sl]
        later = (_iota((LANES, LANES), 0) > _iota((LANES, LANES), 1)).astype(BF16)
        suffix_and_total = jnp.concatenate([later, jnp.ones((LANES, LANES), BF16)], axis=1)
        lft = lf_buf[slot].reshape(n_pg * FOX_HEADS, PAGE_SIZE)
        both = _dot3(lft, suffix_and_total) * LOG2E
        run = run_ref[...]
        bias_pages = [None] * n_pg
        for j in reversed(range(n_pg)):
            bias_pages[j] = run + both[j * FOX_HEADS:(j + 1) * FOX_HEADS, :LANES]
            run = run + both[j * FOX_HEADS:(j + 1) * FOX_HEADS, LANES:]
        run_ref[...] = run
        ncol = ncol_ref[...]

        def scores(b):
            for j in range(b * sub, (b + 1) * sub):
                kbuf[:, j * PAGE_SIZE:(j + 1) * PAGE_SIZE] = k_buf[slot, j].astype(BF16)
                vbuf[:, j * PAGE_SIZE:(j + 1) * PAGE_SIZE] = v_buf[slot, j].astype(BF16)
            return _dot(q, kbuf[:, b * span:(b + 1) * span])

        def update(b, s):
            bias = jnp.concatenate(bias_pages[b * sub:(b + 1) * sub], axis=1)
            m_new, alpha, l_new, p_b = _online_softmax(add_head_rows(s, bias), m_ref[...], l_ref[...],
                                                       ncol)
            l_ref[...] = l_new
            acc_ref[...] = (jnp.concatenate([alpha] * (width // LANES), axis=1) * acc_ref[...]
                            + _dot_nt(p_b, vbuf[:, b * span:(b + 1) * span]))
            m_ref[...] = m_new

        _pipelined(n_sub, scores, update)

    def finish(sl):
        o_ref[sl] = acc_ref[...] / jnp.sum(l_ref[...], axis=1, keepdims=True)

    return begin, consume, finish


def _fox_scratch(rows, width, n_pg):
    return [pltpu.VMEM((PAGE_SLOTS, n_pg, width, PAGE_SIZE), F32),
            pltpu.VMEM((PAGE_SLOTS, n_pg, width, PAGE_SIZE), F32),
            pltpu.VMEM((PAGE_SLOTS, n_pg, FOX_HEADS, PAGE_SIZE), F32),
            pltpu.VMEM((width, n_pg * PAGE_SIZE), BF16),
            pltpu.VMEM((width, n_pg * PAGE_SIZE), BF16),
            pltpu.VMEM((rows, LANES), F32), pltpu.VMEM((rows, LANES), F32),
            pltpu.VMEM((rows, width), F32),
            pltpu.VMEM((FOX_HEADS, LANES), F32),
            pltpu.VMEM((rows, LANES), F32)]


def _ffn_paged_kernel(pt_ref, *refs, fox, n_pg, n_sub, dec_seq, n_ff, chunk, n_chunks):
    x_ref, sh_ref, sc_ref, gt_ref, g_ref, wgu_ref, wdn_ref = refs[:7]
    n_att = 4
    att_in = refs[7:7 + n_att]
    n_cache = 3 if fox else 2
    caches = refs[7 + n_att:7 + n_att + n_cache]
    out_ref, o_ref, acc_ref, hb_ref, sems = refs[7 + n_att + n_cache:12 + n_att + n_cache]
    scratch = refs[12 + n_att + n_cache:]
    if fox:
        begin, consume, finish = _fox_engine(att_in, o_ref, scratch, n_pg, n_sub, dec_seq)
    else:
        begin, consume, finish = _mla_engine(att_in, o_ref, scratch, n_pg, n_sub)
    start, wait = _page_stream(pt_ref, caches, scratch[:n_cache], sems, n_pg)

    nb, tt, d = x_ref.shape
    m = nb * tt
    d_ff = wdn_ref.shape[0]
    h = _rms(x_ref[...], g_ref[...]) * (1.0 + sc_ref[...]) + sh_ref[...]
    hb_ref[...] = h.reshape(m, d).astype(BF16)

    def ffn_chunk(c):
        lo = c * chunk
        hb = hb_ref[...]
        gate = _dot(hb, wgu_ref[:, lo:lo + chunk])
        up = _dot(hb, wgu_ref[:, d_ff + lo:d_ff + lo + chunk])
        y = _dot((gate * _sigmoid(gate) * up).astype(BF16), wdn_ref[lo:lo + chunk, :])
        if c == 0:
            acc_ref[...] = y
        else:
            acc_ref[...] += y

    seq_per_step = att_in[0].shape[0]
    g = pl.program_id(0) * pl.num_programs(1) + pl.program_id(1)
    n_g = pl.num_programs(0) * pl.num_programs(1)
    order = [(sl, k) for sl in range(seq_per_step) for k in range(n_chunks)]

    def chunk_no(k):
        return n_chunks - 1 - k if fox else k

    n_slots = scratch[0].shape[0]
    ahead = n_slots - 1

    def start_ahead(idx):
        step, pos = divmod(idx, len(order))
        sl2, k2 = order[pos]
        if step == 0:
            start(g * seq_per_step + sl2, chunk_no(k2), idx % n_slots)
        else:
            @pl.when(g + 1 < n_g)
            def _():
                start((g + 1) * seq_per_step + sl2, chunk_no(k2), idx % n_slots)

    @pl.when(g == 0)
    def _():
        for idx in range(ahead):
            start(order[idx][0], chunk_no(order[idx][1]), idx)

    for idx, (sl, k) in enumerate(order):
        slot = idx % n_slots
        start_ahead(idx + ahead)
        wait(g * seq_per_step + sl, chunk_no(k), slot)
        if k == 0:
            begin(sl)
        consume(sl, slot)
        if k == n_chunks - 1:
            finish(sl)
        for c in range(n_ff):
            if c * len(order) // n_ff == idx:
                ffn_chunk(c)

    out_ref[...] = x_ref[...] + (0.5 * gt_ref[...]) * acc_ref[...].reshape(nb, tt, d)


def _ffn_paged(x3, mods3, norm_g, w_gu, w_dn, tt, page_table, att_in, caches, fox, n_pg, n_sub, dec_seq):
    n_b, n_t, d = x3.shape
    d_ff = w_dn.shape[0]
    chunk = 256
    n_steps = n_b * (n_t // tt)
    n_seq, rows, _ = att_in[0].shape
    assert n_seq % n_steps == 0, "sample sequences must split evenly over the prompt tiles"
    seq_per_step = n_seq // n_steps
    n_chunks = page_table.shape[1] // n_pg
    assert (seq_per_step * n_chunks) % PAGE_SLOTS == 0
    n_j = n_t // tt
    x_spec = pl.BlockSpec((1, tt, d), lambda i, j, pt: (i, j, 0))

    def mod_spec(col):
        return pl.BlockSpec((1, 1, d), lambda i, j, pt, col=col: (i, 0, col))

    def const_spec(shape):
        return pl.BlockSpec(shape, lambda i, j, pt: (0,) * len(shape), pipeline_mode=pl.Buffered(1))

    def seq_spec(a):
        return pl.BlockSpec((seq_per_step,) + a.shape[1:], lambda i, j, pt: (i * n_j + j, 0, 0))

    hbm_spec = pl.BlockSpec(memory_space=pl.ANY)
    if fox:
        out_w, out_dt = att_in[0].shape[2], F32
        att_scratch = _fox_scratch(rows, out_w, n_pg)
    else:
        out_w, out_dt = MLA_KV_LORA, BF16
        att_scratch = _mla_scratch(rows, n_pg)
    grid_spec = pltpu.PrefetchScalarGridSpec(
        num_scalar_prefetch=1,
        grid=(n_b, n_j),
        in_specs=[x_spec, mod_spec(0), mod_spec(1), mod_spec(2), const_spec((1, 1, d)),
                  const_spec(w_gu.shape), const_spec(w_dn.shape)]
        + [seq_spec(a) for a in att_in] + [hbm_spec] * len(caches),
        out_specs=[x_spec, pl.BlockSpec((seq_per_step, rows, out_w), lambda i, j, pt: (i * n_j + j, 0, 0))],
        scratch_shapes=[pltpu.VMEM((tt, d), F32), pltpu.VMEM((tt, d), BF16),
                        pltpu.SemaphoreType.DMA((len(caches), PAGE_SLOTS))] + att_scratch,
    )
    return pl.pallas_call(
        functools.partial(_ffn_paged_kernel, fox=fox, n_pg=n_pg, n_sub=n_sub, dec_seq=dec_seq,
                          n_ff=d_ff // chunk, chunk=chunk, n_chunks=n_chunks),
        grid_spec=grid_spec,
        out_shape=[jax.ShapeDtypeStruct(x3.shape, F32),
                   jax.ShapeDtypeStruct((n_seq, rows, out_w), out_dt)],
        compiler_params=_params("arbitrary", "arbitrary"),
        name="ffn_paged",
    )(page_table, x3, mods3, mods3, mods3, norm_g.reshape(1, 1, d), w_gu, w_dn, *att_in, *caches)


def _rope_tables(pos):
    half = MLA_ROPE // 2
    inv_freq = ROPE_THETA ** (-jnp.arange(half, dtype=F32) / half)
    ang = pos.astype(F32)[:, None] * inv_freq[None, :]
    cos, sin, zero = jnp.cos(ang), jnp.sin(ang), jnp.zeros_like(ang)
    return (jnp.concatenate([cos, cos, zero, zero], axis=1),
            jnp.concatenate([-sin, sin, zero, zero], axis=1))


def _mla_weights(w_in, g_q, w_uq, g_kv, w_uk, w_uv, w_o):
    lat = MLA_Q_LORA + MLA_KV_LORA
    kr_cols = np.concatenate([np.arange(lat, lat + MLA_ROPE)] * 2)
    in_cols = np.concatenate([np.arange(lat), kr_cols])
    uq_cols = []
    for hd in range(MLA_HEADS):
        b0 = hd * (MLA_NOPE + MLA_ROPE)
        rope_cols = np.arange(b0 + MLA_NOPE, b0 + MLA_NOPE + MLA_ROPE)
        uq_cols += [np.arange(b0, b0 + MLA_NOPE), rope_cols, rope_cols]
    return dict(
        w_in=w_in[:, in_cols].astype(BF16),
        g_q=g_q.reshape(1, -1), g_kv=g_kv.reshape(1, -1),
        w_uq=w_uq[:, np.concatenate(uq_cols)].astype(BF16),
        w_uk=w_uk.reshape(MLA_KV_LORA, MLA_HEADS * MLA_NOPE).astype(BF16),
        w_uv=w_uv.reshape(MLA_KV_LORA, MLA_HEADS * MLA_V).astype(BF16),
        w_ukt=jnp.transpose(w_uk, (1, 2, 0)).astype(BF16),
        w_uv3=jnp.transpose(w_uv, (1, 0, 2)).astype(BF16),
        w_o=w_o.astype(BF16),
    )


def _pad_rows(a, rows):
    return jnp.pad(a, ((0, 0), (0, rows - a.shape[1])) + ((0, 0),) * (a.ndim - 2))


def _pages_per_step(n_pages, want):
    n = min(want, n_pages)
    while n_pages % n:
        n -= 1
    return n


def kernel(x_prompt, x_sample, c_prompt, c_sample, cache_mla_ckv, cache_mla_krope, cache_fox_k, cache_fox_v, cache_fox_logf, page_table, ada_w, ada_b, norm_g, ffn_w_gu, ffn_w_dn, final_g, mla_w_in, mla_g_q, mla_w_uq, mla_g_kv, mla_w_uk, mla_w_uv, mla_w_o, fox_w_in, fox_b_f, fox_w_o):
    n_p, seq, d = x_prompt.shape
    n_s, dec_seq, _ = x_sample.shape
    n_pages = page_table.shape[1]
    past_len = n_pages * PAGE_SIZE
    tile = min(512, seq)
    tk = min(1024, seq)
    n_pad = LANES

    mods = _adaln(jnp.concatenate([c_prompt, c_sample], axis=0), ada_w, ada_b)
    w_gu = ffn_w_gu.astype(BF16)
    w_dn = ffn_w_dn.astype(BF16)
    xp, xs = x_prompt, x_sample

    mp = mods[0, :n_p].reshape(n_p, 1, -1)
    ms = mods[0, n_p:].reshape(n_s, 1, -1)
    w = _mla_weights(mla_w_in[0], mla_g_q[0], mla_w_uq[0], mla_g_kv[0], mla_w_uk[0], mla_w_uv[0],
                     mla_w_o[0])
    xs = _ffn(xs, ms, 0, norm_g[0, 0], w_gu[0, 0], w_dn[0, 0], n_s, dec_seq)
    cos_s, sin_s = _rope_tables(past_len + jnp.arange(dec_seq, dtype=jnp.int32))
    cos_s, sin_s = jnp.tile(cos_s, (n_s, 1)), jnp.tile(sin_s, (n_s, 1))
    q_lat, q_rope, s_ckv, s_kr = _mla_proj(xs, ms, norm_g[0, 1], w, cos_s, sin_s, n_s, dec_seq,
                                           absorb=True)
    rows = dec_seq * MLA_HEADS
    n_pg = _pages_per_step(n_pages, 16)
    n_sub = 2 if n_pg % 2 == 0 else 1
    xp, o_lat = _ffn_paged(
        xp, mp, norm_g[0, 0], w_gu[0, 0], w_dn[0, 0], tile, page_table,
        [q_lat.reshape(n_s, rows, MLA_KV_LORA), q_rope.reshape(n_s, rows, LANES),
         _pad_rows(s_ckv.reshape(n_s, dec_seq, MLA_KV_LORA), n_pad).astype(BF16),
         _pad_rows(s_kr.reshape(n_s, dec_seq, MLA_ROPE), n_pad).astype(BF16)],
        [cache_mla_ckv[0], jnp.swapaxes(cache_mla_krope[0], 1, 2)],
        False, n_pg, n_sub, dec_seq)
    o_s = _mla_uv(o_lat.reshape(n_s * dec_seq, MLA_HEADS * MLA_KV_LORA), w["w_uv3"])

    cos_p, sin_p = _rope_tables(jnp.arange(seq, dtype=jnp.int32))
    q, k, v, p_ckv, p_kr = _mla_proj(xp, mp, norm_g[0, 1], w, cos_p, sin_p, 1, tile, absorb=False)
    units = [(slice(h * 256, (h + 1) * 256), 0, slice(h * 256, (h + 1) * 256), 1,
              slice(h * 128, (h + 1) * 128)) for h in range(MLA_HEADS)]
    o_p = _flash(q, [k, v], units, [[h] for h in range(MLA_HEADS)], n_p, seq, tile, tk)

    xp = _ffn(xp, mp, 2, norm_g[0, 2], w_gu[0, 1], w_dn[0, 1], 1, tile, mix=(o_p, w["w_o"]))
    xs = _ffn(xs, ms, 2, norm_g[0, 2], w_gu[0, 1], w_dn[0, 1], n_s, dec_seq, mix=(o_s, w["w_o"]))

    mp = mods[1, :n_p].reshape(n_p, 1, -1)
    ms = mods[1, n_p:].reshape(n_s, 1, -1)
    qw = FOX_HEADS * FOX_HEAD_DIM
    kw = FOX_KV_HEADS * FOX_HEAD_DIM
    fw_in = jnp.pad(fox_w_in[0], ((0, 0), (0, LANES - FOX_HEADS))).astype(BF16)
    fb = jnp.pad(fox_b_f[0], (0, LANES - FOX_HEADS)).reshape(1, LANES)
    fw_o = fox_w_o[0].astype(BF16)
    xs = _ffn(xs, ms, 0, norm_g[1, 0], w_gu[1, 0], w_dn[1, 0], n_s, dec_seq)
    q_s, s_k, s_v, s_lf = _fox_proj(xs, ms, norm_g[1, 1], fw_in, fb, n_s, dec_seq, prompt=False)
    q4 = jnp.transpose(q_s.reshape(n_s, dec_seq, FOX_HEADS, FOX_HEAD_DIM), (0, 2, 1, 3))
    grp_of_head = jnp.arange(FOX_HEADS) // (FOX_HEADS // FOX_KV_HEADS)
    onehot = (grp_of_head[:, None] == jnp.arange(FOX_KV_HEADS)[None, :]).astype(BF16)
    q_all = (q4[:, :, :, None, :] * onehot[None, :, None, :, None]).reshape(
        n_s, FOX_HEADS * dec_seq, kw)
    lf_new = jnp.pad(s_lf.reshape(n_s, dec_seq, FOX_HEADS),
                     ((0, 0), (0, n_pad - dec_seq), (0, LANES - FOX_HEADS)))
    cache_lft = jnp.swapaxes(cache_fox_logf[0], 1, 2)
    cache_kt = jnp.transpose(cache_fox_k[0], (0, 2, 3, 1)).reshape(-1, kw, PAGE_SIZE)
    cache_vt = jnp.transpose(cache_fox_v[0], (0, 2, 3, 1)).reshape(-1, kw, PAGE_SIZE)
    xp, o_all = _ffn_paged(
        xp, mp, norm_g[1, 0], w_gu[1, 0], w_dn[1, 0], tile, page_table,
        [q_all, lf_new,
         _pad_rows(s_k.reshape(n_s, dec_seq, kw), n_pad).astype(BF16),
         _pad_rows(s_v.reshape(n_s, dec_seq, kw), n_pad).astype(BF16)],
        [cache_kt, cache_vt, cache_lft], True, n_pg, n_sub, dec_seq)
    o5 = o_all.reshape(n_s, FOX_HEADS, dec_seq, FOX_KV_HEADS, FOX_HEAD_DIM)
    o_sel = jnp.take_along_axis(o5, grp_of_head[None, :, None, None, None], axis=3)[:, :, :, 0, :]
    o_s = jnp.transpose(o_sel, (0, 2, 1, 3)).reshape(n_s * dec_seq, qw).astype(BF16)

    qa, klo, khi, vlo, vhi, p_k, p_v, p_lf = _fox_proj(xp, mp, norm_g[1, 1], fw_in, fb, 1, tile,
                                                       prompt=True)
    units = []
    for h in range(FOX_HEADS):
        i, par, g = h // 2, h % 2, h // (FOX_HEADS // FOX_KV_HEADS)
        units.append((slice(i * 256, (i + 1) * 256), par, slice(g * 256, (g + 1) * 256), 2 + par,
                      slice(g * 128, (g + 1) * 128)))
    o_p = _flash(qa, [klo, khi, vlo, vhi], units, [[2 * i, 2 * i + 1] for i in range(FOX_HEADS // 2)],
                 n_p, seq, tile, tk)

    xp = _ffn(xp, mp, 2, norm_g[1, 2], w_gu[1, 1], w_dn[1, 1], 1, tile, mix=(o_p, fw_o),
              final_g=final_g)
    xs = _ffn(xs, ms, 2, norm_g[1, 2], w_gu[1, 1], w_dn[1, 1], n_s, dec_seq, mix=(o_s, fw_o),
              final_g=final_g)

    def prompt_state(a, *tail):
        return a.reshape(1, n_p, seq, *tail)

    def sample_state(a, *tail):
        return a.reshape(1, n_s, dec_seq, *tail)

    return (xp, xs,
            prompt_state(p_ckv, MLA_KV_LORA), prompt_state(p_kr, MLA_ROPE),
            prompt_state(p_k, FOX_KV_HEADS, FOX_HEAD_DIM), prompt_state(p_v, FOX_KV_HEADS, FOX_HEAD_DIM),
            prompt_state(p_lf, FOX_HEADS),
            sample_state(s_ckv, MLA_KV_LORA), sample_state(s_kr, MLA_ROPE),
            sample_state(s_k, FOX_KV_HEADS, FOX_HEAD_DIM), sample_state(s_v, FOX_KV_HEADS, FOX_HEAD_DIM),
            sample_state(s_lf, FOX_HEADS))
```

```python
import functools
import math

import numpy as np
import jax
import jax.numpy as jnp
from jax import lax
from jax.experimental import pallas as pl
from jax.experimental.pallas import tpu as pltpu

F32 = jnp.float32
BF16 = jnp.bfloat16

LANES = 128
NORM_EPS = 1e-6
PAGE_SIZE = 128
D_FF = 2816
MLA_HEADS = 8
MLA_NOPE = 128
MLA_ROPE = 64
MLA_V = 128
MLA_Q_LORA = 512
MLA_KV_LORA = 256
ROPE_THETA = 10000.0
MLA_SCALE = 1.0 / math.sqrt(MLA_NOPE + MLA_ROPE)
FOX_HEADS = 16
FOX_KV_HEADS = 4
FOX_HEAD_DIM = 64
FOX_SCALE = 1.0 / math.sqrt(FOX_HEAD_DIM)
LOG2E = math.log2(math.e)
VMEM_LIMIT_BYTES = 60 * 1024 * 1024
PAGE_SLOTS = 4


def _params(*sem):
    return pltpu.CompilerParams(dimension_semantics=sem, vmem_limit_bytes=VMEM_LIMIT_BYTES)


def _dot(a, b):
    return jnp.dot(a, b, preferred_element_type=F32)


def _dot_nt(a, b):
    return lax.dot_general(a, b, (((1,), (1,)), ((), ())), preferred_element_type=F32)


def _sigmoid(x):
    return 1.0 / (1.0 + jnp.exp(-x))


def _rms(x, g):
    return x * lax.rsqrt(jnp.mean(x * x, axis=-1, keepdims=True) + NORM_EPS) * g


def _split3(x):
    hi = x.astype(BF16)
    r = x - hi.astype(F32)
    mid = r.astype(BF16)
    lo = (r - mid.astype(F32)).astype(BF16)
    return hi, mid, lo


def _dot3(x, m):
    hi, mid, lo = _split3(x)
    return _dot(hi, m) + _dot(mid, m) + _dot(lo, m)


def _iota(shape, dim):
    return lax.broadcasted_iota(jnp.int32, shape, dim)


def _modulated(x_ref, sh_ref, sc_ref, g_ref):
    x = x_ref[...]
    return x, _rms(x, g_ref[...]) * (1.0 + sc_ref[...]) + sh_ref[...]


def _adaln_kernel(c_ref, w_ref, b_ref, o_ref):
    c = c_ref[...]
    a = (c * _sigmoid(c)).astype(BF16)
    o_ref[0] = _dot(a, w_ref[0].astype(BF16)) + b_ref[0]


def _adaln(c_all, ada_w, ada_b):
    n_layers, d, n = ada_w.shape
    rows = c_all.shape[0]
    tn = 1536 if n % 1536 == 0 else n
    return pl.pallas_call(
        _adaln_kernel,
        grid=(n_layers, n // tn),
        in_specs=[
            pl.BlockSpec((rows, d), lambda l, j: (0, 0)),
            pl.BlockSpec((1, d, tn), lambda l, j: (l, 0, j)),
            pl.BlockSpec((1, 1, tn), lambda l, j: (l, 0, j)),
        ],
        out_specs=pl.BlockSpec((1, rows, tn), lambda l, j: (l, 0, j)),
        out_shape=jax.ShapeDtypeStruct((n_layers, rows, n), F32),
        compiler_params=_params("parallel", "parallel"),
        name="adaln",
    )(c_all, ada_w, ada_b.reshape(n_layers, 1, n))


def _ffn_kernel(*refs, has_mix, final, n_chunks, chunk):
    it = iter(refs)
    x_ref, sh_ref, sc_ref, gt_ref, g_ref, wgu_ref, wdn_ref = (next(it) for _ in range(7))
    if has_mix:
        o_ref, wo_ref, gm_ref = next(it), next(it), next(it)
    if final:
        fg_ref = next(it)
    out_ref, acc_ref = next(it), next(it)
    nb, tt, d = x_ref.shape
    m = nb * tt
    x = x_ref[...]
    if has_mix:
        x = x + gm_ref[...] * _dot(o_ref[...], wo_ref[...]).reshape(nb, tt, d)
    h = _rms(x, g_ref[...]) * (1.0 + sc_ref[...]) + sh_ref[...]
    hb = h.reshape(m, d).astype(BF16)
    d_ff = wdn_ref.shape[0]
    for c in range(n_chunks):
        lo = c * chunk
        gate = _dot(hb, wgu_ref[:, lo:lo + chunk])
        up = _dot(hb, wgu_ref[:, d_ff + lo:d_ff + lo + chunk])
        act = (gate * _sigmoid(gate) * up).astype(BF16)
        y = _dot(act, wdn_ref[lo:lo + chunk, :])
        if c == 0:
            acc_ref[...] = y
        else:
            acc_ref[...] += y
    y = x + (0.5 * gt_ref[...]) * acc_ref[...].reshape(nb, tt, d)
    if final:
        y = _rms(y, fg_ref[...])
    out_ref[...] = y


def _ffn(x3, mods3, sub, norm_g, w_gu, w_dn, nb, tt, mix=None, final_g=None):
    n_b, n_t, d = x3.shape
    d_ff = w_dn.shape[0]
    chunk = 256
    grid = (n_b // nb, n_t // tt)
    m = nb * tt
    x_spec = pl.BlockSpec((nb, tt, d), lambda i, j: (i, j, 0))

    def mod_spec(col):
        return pl.BlockSpec((nb, 1, d), lambda i, j, col=col: (i, 0, col))

    def const_spec(shape):
        return pl.BlockSpec(shape, lambda i, j: (0,) * len(shape), pipeline_mode=pl.Buffered(1))

    ops = [x3, mods3, mods3, mods3, norm_g.reshape(1, 1, d), w_gu, w_dn]
    specs = [x_spec, mod_spec(3 * sub), mod_spec(3 * sub + 1), mod_spec(3 * sub + 2),
             const_spec((1, 1, d)), const_spec(w_gu.shape), const_spec(w_dn.shape)]
    if mix is not None:
        o2, w_o = mix
        n_tt = n_t // tt
        ops += [o2, w_o, mods3]
        specs += [pl.BlockSpec((m, o2.shape[1]), lambda i, j: (i * n_tt + j, 0)),
                  const_spec(w_o.shape), mod_spec(3 * 1 + 2)]
    if final_g is not None:
        ops.append(final_g.reshape(1, 1, d))
        specs.append(const_spec((1, 1, d)))
    return pl.pallas_call(
        functools.partial(_ffn_kernel, has_mix=mix is not None, final=final_g is not None,
                          n_chunks=d_ff // chunk, chunk=chunk),
        grid=grid,
        in_specs=specs,
        out_specs=x_spec,
        out_shape=jax.ShapeDtypeStruct(x3.shape, F32),
        scratch_shapes=[pltpu.VMEM((m, d), F32)],
        compiler_params=_params("parallel", "parallel"),
        name="ffn",
    )(*ops)


def _mla_proj_kernel(*refs, absorb):
    it = iter(refs)
    x_ref, sh_ref, sc_ref, g_ref, win_ref, gq_ref, gkv_ref, wuq_ref, cos_ref, sin_ref = (
        next(it) for _ in range(10))
    if absorb:
        wukt_ref = next(it)
        qlat_ref, qrope_ref, ckv_ref, kr_ref = (next(it) for _ in range(4))
    else:
        wuk_ref, wuv_ref = next(it), next(it)
        q_ref, k_ref, v_ref, ckv_ref, kr_ref = (next(it) for _ in range(5))
    nb, tt, d = x_ref.shape
    m = nb * tt
    _, h = _modulated(x_ref, sh_ref, sc_ref, g_ref)
    hb = h.reshape(m, d).astype(BF16)
    proj = _dot(hb, win_ref[...])
    c_q = _rms(proj[:, :MLA_Q_LORA], gq_ref[...])
    c_kv = _rms(proj[:, MLA_Q_LORA:MLA_Q_LORA + MLA_KV_LORA], gkv_ref[...])
    ckv_ref[...] = c_kv
    cos_a = cos_ref[...]
    sin_b = sin_ref[...]

    def rope(z):
        return z * cos_a + pltpu.roll(z, 32, 1) * sin_b

    kr = rope(proj[:, MLA_Q_LORA + MLA_KV_LORA:])
    kr_ref[...] = kr[:, :MLA_ROPE]
    qf = _dot(c_q.astype(BF16), wuq_ref[...]) * (MLA_SCALE * LOG2E)
    ckv_b = c_kv.astype(BF16)
    if not absorb:
        kn = _dot(ckv_b, wuk_ref[...])
        v_ref[...] = _dot(ckv_b, wuv_ref[...]).astype(BF16)
        kr_b = kr.astype(BF16)
    for hd in range(MLA_HEADS):
        b0 = hd * 256
        qn = qf[:, b0:b0 + 128]
        qr = rope(qf[:, b0 + 128:b0 + 256])
        if absorb:
            qlat_ref[:, b0:b0 + 256] = _dot(qn.astype(BF16), wukt_ref[hd]).astype(BF16)
            qrope_ref[:, hd * 128:(hd + 1) * 128] = qr.astype(BF16)
        else:
            q_ref[:, b0:b0 + 128] = qn.astype(BF16)
            q_ref[:, b0 + 128:b0 + 256] = qr.astype(BF16)
            k_ref[:, b0:b0 + 128] = kn[:, hd * 128:(hd + 1) * 128].astype(BF16)
            k_ref[:, b0 + 128:b0 + 256] = kr_b


def _mla_proj(x3, mods3, norm_g, w, cos_a, sin_b, nb, tt, absorb):
    n_b, n_t, d = x3.shape
    m = nb * tt
    n_tt = n_t // tt
    rows = n_b * n_t
    grid = (n_b // nb, n_tt)
    x_spec = pl.BlockSpec((nb, tt, d), lambda i, j: (i, j, 0))

    def mod_spec(col):
        return pl.BlockSpec((nb, 1, d), lambda i, j, col=col: (i, 0, col))

    def const_spec(shape):
        return pl.BlockSpec(shape, lambda i, j: (0,) * len(shape))

    def row_spec(width):
        return pl.BlockSpec((m, width), lambda i, j: (i * n_tt + j, 0))

    tab_spec = pl.BlockSpec((m, LANES), lambda i, j: (j, 0))
    ops = [x3, mods3, mods3, norm_g.reshape(1, 1, d), w["w_in"], w["g_q"], w["g_kv"], w["w_uq"],
           cos_a, sin_b]
    specs = [x_spec, mod_spec(3), mod_spec(4), const_spec((1, 1, d)), const_spec(w["w_in"].shape),
             const_spec(w["g_q"].shape), const_spec(w["g_kv"].shape), const_spec(w["w_uq"].shape),
             tab_spec, tab_spec]
    if absorb:
        ops.append(w["w_ukt"])
        specs.append(const_spec(w["w_ukt"].shape))
        widths = [(MLA_HEADS * 256, BF16), (MLA_HEADS * 128, BF16)]
    else:
        ops += [w["w_uk"], w["w_uv"]]
        specs += [const_spec(w["w_uk"].shape), const_spec(w["w_uv"].shape)]
        widths = [(MLA_HEADS * 256, BF16), (MLA_HEADS * 256, BF16), (MLA_HEADS * MLA_V, BF16)]
    widths += [(MLA_KV_LORA, F32), (MLA_ROPE, F32)]
    return pl.pallas_call(
        functools.partial(_mla_proj_kernel, absorb=absorb),
        grid=grid,
        in_specs=specs,
        out_specs=[row_spec(wd) for wd, _ in widths],
        out_shape=[jax.ShapeDtypeStruct((rows, wd), dt) for wd, dt in widths],
        compiler_params=_params("parallel", "parallel"),
        name="mla_proj",
    )(*ops)


def _online_softmax(s, m_prev, l_prev, row_bias=None):
    blocks = [s[:, b * LANES:(b + 1) * LANES] for b in range(s.shape[1] // LANES)]
    if row_bias is not None:
        blocks = [blk + row_bias for blk in blocks]
    mx = blocks[0]
    for blk in blocks[1:]:
        mx = jnp.maximum(mx, blk)
    m_new = jnp.maximum(m_prev, jnp.max(mx, axis=1, keepdims=True))
    alpha = jnp.exp2(m_prev - m_new)
    ps = [jnp.exp2(blk - m_new) for blk in blocks]
    psum = ps[0]
    for p in ps[1:]:
        psum = psum + p
    p_b = jnp.concatenate([p.astype(BF16) for p in ps], axis=1) if len(ps) > 1 else ps[0].astype(BF16)
    return m_new, alpha, alpha * l_prev + psum, p_b


def _pipelined(n, produce, consume):
    nxt = produce(0)
    for b in range(n):
        cur = nxt
        if b + 1 < n:
            nxt = produce(b + 1)
        consume(b, cur)


def _flash_kernel(*refs, units, outs, n_kv):
    q_ref = refs[0]
    kv = refs[1:1 + n_kv]
    o_ref = refs[1 + n_kv]
    m_ref, l_ref, acc_ref = refs[2 + n_kv:]
    i = pl.program_id(1)
    j = pl.program_id(2)
    tq = q_ref.shape[0]
    tk = kv[0].shape[0]
    j_last = ((i + 1) * tq - 1) // tk

    @pl.when(j == 0)
    def _():
        m_ref[...] = jnp.full(m_ref.shape, -jnp.inf, F32)
        l_ref[...] = jnp.zeros(l_ref.shape, F32)
        acc_ref[...] = jnp.zeros(acc_ref.shape, F32)

    def step(masked, n_keys):
        if masked:
            keep = (i * tq + _iota((tq, n_keys), 0)) >= (j * tk + _iota((tq, n_keys), 1))

        def scores(u):
            qs, ki, ks, _, _ = units[u]
            s = _dot_nt(q_ref[:, qs], kv[ki][:n_keys, ks])
            return jnp.where(keep, s, -jnp.inf) if masked else s

        def update(u, s):
            _, _, _, vi, vs = units[u]
            m_new, alpha, l_new, p_b = _online_softmax(s, m_ref[u], l_ref[u])
            l_ref[u] = l_new
            acc_ref[u] = alpha * acc_ref[u] + _dot(p_b, kv[vi][:n_keys, vs])
            m_ref[u] = m_new

        _pipelined(len(units), scores, update)

    fully_visible = (j + 1) * tk <= i * tq
    partly_visible = jnp.logical_and(jnp.logical_not(fully_visible), j <= j_last)
    half = tk // 2
    first_half_only = j * tk + half > (i + 1) * tq - 1 if half % LANES == 0 else False

    @pl.when(fully_visible)
    def _():
        step(False, tk)

    if half % LANES == 0:
        @pl.when(jnp.logical_and(partly_visible, first_half_only))
        def _():
            step(True, half)

    @pl.when(jnp.logical_and(partly_visible, jnp.logical_not(first_half_only)))
    def _():
        step(True, tk)

    @pl.when(j == j_last)
    def _():
        for b, us in enumerate(outs):
            o = None
            for u in us:
                term = acc_ref[u] / jnp.sum(l_ref[u], axis=1, keepdims=True)
                o = term if o is None else o + term
            o_ref[:, b * LANES:(b + 1) * LANES] = o.astype(o_ref.dtype)


def _flash(q, kvs, units, outs, n_batch, seq, tq, tk):
    nq = seq // tq
    nk = seq // tk
    n_units = len(units)

    def q_map(b, i, j):
        return (b * nq + i, 0)

    def kv_map(b, i, j):
        return (b * nk + jnp.minimum(j, ((i + 1) * tq - 1) // tk), 0)

    out_w = LANES * len(outs)
    return pl.pallas_call(
        functools.partial(_flash_kernel, units=units, outs=outs, n_kv=len(kvs)),
        grid=(n_batch, nq, nk),
        in_specs=[pl.BlockSpec((tq, q.shape[1]), q_map)]
        + [pl.BlockSpec((tk, a.shape[1]), kv_map) for a in kvs],
        out_specs=pl.BlockSpec((tq, out_w), q_map),
        out_shape=jax.ShapeDtypeStruct((q.shape[0], out_w), BF16),
        scratch_shapes=[pltpu.VMEM((n_units, tq, LANES), F32), pltpu.VMEM((n_units, tq, LANES), F32),
                        pltpu.VMEM((n_units, tq, LANES), F32)],
        compiler_params=_params("parallel", "parallel", "arbitrary"),
        name="flash",
    )(q, *kvs)


def _page_stream(pt_ref, caches, bufs, sems, n_pg):
    def copies(s, chunk, slot):
        out = []
        for j in range(n_pg):
            page = pt_ref[s, chunk * n_pg + j]
            for a, (cache, buf) in enumerate(zip(caches, bufs)):
                out.append(pltpu.make_async_copy(cache.at[page], buf.at[slot, j], sems.at[a, slot]))
        return out

    def start(s, chunk, slot):
        for n, cp in enumerate(copies(s, chunk, slot)):
            cp.start(priority=(n // len(caches)) % 2)

    def wait(s, chunk, slot):
        for cp in copies(s, chunk, slot):
            cp.wait()

    return start, wait


def _mla_engine(ins, o_ref, scratch, n_pg, n_sub):
    qlat_ref, qrope_ref, ckn_ref, krn_ref = ins
    ckv_buf, kr_buf, kbuf, krbuf, m_ref, l_ref, acc_ref = scratch
    rows = qlat_ref.shape[1]
    sub = n_pg // n_sub
    width = sub * PAGE_SIZE

    def begin(sl):
        q_lat = qlat_ref[sl]
        ckn = ckn_ref[sl]
        n_new = ckn.shape[0]
        s = _dot_nt(q_lat, ckn) + _dot_nt(qrope_ref[sl][:, :MLA_ROPE], krn_ref[sl])
        s = jnp.where(_iota((rows, n_new), 1) <= _iota((rows, n_new), 0) // MLA_HEADS, s, -jnp.inf)
        m_new, _, l_new, p_b = _online_softmax(
            s, jnp.full((rows, LANES), -jnp.inf, F32), jnp.zeros((rows, LANES), F32))
        m_ref[...] = m_new
        l_ref[...] = l_new
        acc_ref[...] = _dot(p_b, ckn)

    def consume(sl, slot):
        q_lat = qlat_ref[sl]
        q_rope = qrope_ref[sl][:, :MLA_ROPE]

        def scores(b):
            for j in range(b * sub, (b + 1) * sub):
                kbuf[j * PAGE_SIZE:(j + 1) * PAGE_SIZE, :] = ckv_buf[slot, j].astype(BF16)
                krbuf[:, j * PAGE_SIZE:(j + 1) * PAGE_SIZE] = kr_buf[slot, j].astype(BF16)
            return (_dot_nt(q_lat, kbuf[b * width:(b + 1) * width, :])
                    + _dot(q_rope, krbuf[:, b * width:(b + 1) * width]))

        def update(b, s):
            m_new, alpha, l_new, p_b = _online_softmax(s, m_ref[...], l_ref[...])
            l_ref[...] = l_new
            acc_ref[...] = (jnp.concatenate([alpha] * (MLA_KV_LORA // LANES), axis=1) * acc_ref[...]
                            + _dot(p_b, kbuf[b * width:(b + 1) * width, :]))
            m_ref[...] = m_new

        _pipelined(n_sub, scores, update)

    def finish(sl):
        l_tot = jnp.sum(l_ref[...], axis=1, keepdims=True)
        o_ref[sl] = (acc_ref[...] / l_tot).astype(o_ref.dtype)

    return begin, consume, finish


def _mla_scratch(rows, n_pg):
    return [pltpu.VMEM((PAGE_SLOTS, n_pg, PAGE_SIZE, MLA_KV_LORA), F32),
            pltpu.VMEM((PAGE_SLOTS, n_pg, MLA_ROPE, PAGE_SIZE), F32),
            pltpu.VMEM((n_pg * PAGE_SIZE, MLA_KV_LORA), BF16),
            pltpu.VMEM((MLA_ROPE, n_pg * PAGE_SIZE), BF16),
            pltpu.VMEM((rows, LANES), F32), pltpu.VMEM((rows, LANES), F32),
            pltpu.VMEM((rows, MLA_KV_LORA), F32)]


def _mla_uv_kernel(o_ref, wuv_ref, out_ref):
    for hd in range(MLA_HEADS):
        out_ref[:, hd * MLA_V:(hd + 1) * MLA_V] = _dot(
            o_ref[:, hd * MLA_KV_LORA:(hd + 1) * MLA_KV_LORA], wuv_ref[hd]).astype(BF16)


def _mla_uv(o_lat2, w_uv3):
    rows = o_lat2.shape[0]
    return pl.pallas_call(
        _mla_uv_kernel,
        out_shape=jax.ShapeDtypeStruct((rows, MLA_HEADS * MLA_V), BF16),
        compiler_params=pltpu.CompilerParams(vmem_limit_bytes=VMEM_LIMIT_BYTES),
        name="mla_uv",
    )(o_lat2, w_uv3)


def _fox_proj_kernel(*refs, prompt):
    it = iter(refs)
    x_ref, sh_ref, sc_ref, g_ref, win_ref, bf_ref = (next(it) for _ in range(6))
    if prompt:
        sel_ref, selc_ref = next(it), next(it)
        qa_ref, klo_ref, khi_ref, vlo_ref, vhi_ref, k_ref, v_ref, lf_ref, carry_ref = (
            next(it) for _ in range(9))
    else:
        q_ref, k_ref, v_ref, lf_ref = (next(it) for _ in range(4))
    nb, tt, d = x_ref.shape
    m = nb * tt
    qw = FOX_HEADS * FOX_HEAD_DIM
    kw = FOX_KV_HEADS * FOX_HEAD_DIM
    _, h = _modulated(x_ref, sh_ref, sc_ref, g_ref)
    hb = h.reshape(m, d).astype(BF16)
    proj = _dot(hb, win_ref[...])
    q = proj[:, :qw] * (FOX_SCALE * LOG2E)
    k = proj[:, qw:qw + kw]
    v = proj[:, qw + kw:qw + 2 * kw]
    z = proj[:, qw + 2 * kw:] + bf_ref[...]
    logf = jnp.minimum(z, 0.0) - jnp.log(1.0 + jnp.exp(-jnp.abs(z)))
    if prompt:
        k_ref[...] = k.T
        v_ref[...] = v.T
    else:
        k_ref[...] = k
        v_ref[...] = v
    lf_ref[...] = logf[:, :FOX_HEADS]
    if not prompt:
        q_ref[...] = q.astype(BF16)
        return

    @pl.when(pl.program_id(1) == 0)
    def _():
        carry_ref[...] = jnp.zeros(carry_ref.shape, F32)

    lane = _iota((m, LANES), 1)
    logf = jnp.where(lane < FOX_HEADS, logf, 0.0)
    tri = (_iota((m, m), 0) >= _iota((m, m), 1)).astype(BF16)
    f_hi, f_mid, f_lo = _split3(logf)
    cum = _dot(tri, f_hi) + _dot(tri, f_mid) + _dot(tri, f_lo) + carry_ref[...]
    carry_ref[...] = cum[m - 1:m, :]
    hi, mid, lo = _split3(cum * LOG2E)
    comb = (hi.astype(F32) + pltpu.roll(mid.astype(F32), FOX_HEADS, 1)
            + pltpu.roll(lo.astype(F32), 2 * FOX_HEADS, 1)).astype(BF16)
    aug = _dot(comb, sel_ref[...]) + selc_ref[...]
    n_pairs = FOX_HEADS // 2
    for i in range(n_pairs):
        qa_ref[:, i * 256:i * 256 + 128] = q[:, i * 128:(i + 1) * 128].astype(BF16)
        qa_ref[:, i * 256 + 128:(i + 1) * 256] = aug[:, i * 128:(i + 1) * 128].astype(BF16)
    low_half = lane < FOX_HEAD_DIM
    a0 = n_pairs * 128
    for grp in range(FOX_KV_HEADS):
        blk = (grp // 2) * 128
        for src, lo_ref, hi_ref, stride, has_aug in ((k, klo_ref, khi_ref, 256, True),
                                                     (v, vlo_ref, vhi_ref, 128, False)):
            same = src[:, blk:blk + 128]
            swapped = pltpu.roll(same, FOX_HEAD_DIM, 1)
            lo_src, hi_src = (same, swapped) if grp % 2 == 0 else (swapped, same)
            b0 = grp * stride
            lo_ref[:, b0:b0 + 128] = jnp.where(low_half, lo_src, 0.0).astype(BF16)
            hi_ref[:, b0:b0 + 128] = jnp.where(low_half, 0.0, hi_src).astype(BF16)
            if has_aug:
                lo_ref[:, b0 + 128:b0 + 256] = aug[:, a0 + grp * 128:a0 + (grp + 1) * 128].astype(BF16)
                hi_ref[:, b0 + 128:b0 + 256] = aug[:, a0 + 512 + grp * 128:
                                                   a0 + 512 + (grp + 1) * 128].astype(BF16)


def _fox_aug_selectors():
    n_pairs = FOX_HEADS // 2
    sel = np.zeros((LANES, (n_pairs + 2 * FOX_KV_HEADS) * LANES), np.float32)
    const = np.zeros((1, sel.shape[1]), np.float32)
    klo0 = n_pairs * LANES
    khi0 = klo0 + FOX_KV_HEADS * LANES
    for part in range(3):
        src = part * FOX_HEADS
        for i in range(n_pairs):
            sel[src + 2 * i, i * LANES + part] = 1.0
            sel[src + 2 * i + 1, i * LANES + 3 + part] = 1.0
            const[0, i * LANES + 6 + 3 * (i % 2) + part] = 1.0
        for g in range(FOX_KV_HEADS):
            const[0, klo0 + g * LANES + part] = 1.0
            const[0, khi0 + g * LANES + 3 + part] = 1.0
            sel[src + 4 * g, klo0 + g * LANES + 6 + part] = -1.0
            sel[src + 4 * g + 2, klo0 + g * LANES + 9 + part] = -1.0
            sel[src + 4 * g + 1, khi0 + g * LANES + 6 + part] = -1.0
            sel[src + 4 * g + 3, khi0 + g * LANES + 9 + part] = -1.0
    return jnp.asarray(sel, BF16), jnp.asarray(const, F32)


def _fox_proj(x3, mods3, norm_g, w_in, b_f, nb, tt, prompt):
    n_b, n_t, d = x3.shape
    m = nb * tt
    n_tt = n_t // tt
    rows = n_b * n_t
    x_spec = pl.BlockSpec((nb, tt, d), lambda i, j: (i, j, 0))

    def mod_spec(col):
        return pl.BlockSpec((nb, 1, d), lambda i, j, col=col: (i, 0, col))

    def const_spec(shape):
        return pl.BlockSpec(shape, lambda i, j: (0,) * len(shape))

    def row_spec(width):
        return pl.BlockSpec((m, width), lambda i, j: (i * n_tt + j, 0))

    kw = FOX_KV_HEADS * FOX_HEAD_DIM
    ops = [x3, mods3, mods3, norm_g.reshape(1, 1, d), w_in, b_f]
    specs = [x_spec, mod_spec(3), mod_spec(4), const_spec((1, 1, d)), const_spec(w_in.shape),
             const_spec(b_f.shape)]
    scratch = []
    if prompt:
        sel, selc = _fox_aug_selectors()
        ops += [sel, selc]
        specs += [const_spec(sel.shape), const_spec(selc.shape)]
        widths = [(FOX_HEADS // 2 * 256, BF16), (FOX_KV_HEADS * 256, BF16), (FOX_KV_HEADS * 256, BF16),
                  (FOX_KV_HEADS * 128, BF16), (FOX_KV_HEADS * 128, BF16)]
        scratch = [pltpu.VMEM((1, LANES), F32)]
    else:
        widths = [(FOX_HEADS * FOX_HEAD_DIM, BF16)]
    out_specs = [row_spec(wd) for wd, _ in widths]
    out_shape = [jax.ShapeDtypeStruct((rows, wd), dt) for wd, dt in widths]
    if prompt:
        out_specs += [pl.BlockSpec((None, kw, tt), lambda i, j: (i, 0, j))] * 2
        out_shape += [jax.ShapeDtypeStruct((n_b, kw, n_t), F32)] * 2
    else:
        out_specs += [row_spec(kw)] * 2
        out_shape += [jax.ShapeDtypeStruct((rows, kw), F32)] * 2
    out_specs.append(row_spec(FOX_HEADS))
    out_shape.append(jax.ShapeDtypeStruct((rows, FOX_HEADS), F32))
    return pl.pallas_call(
        functools.partial(_fox_proj_kernel, prompt=prompt),
        grid=(n_b // nb, n_tt),
        in_specs=specs,
        out_specs=out_specs,
        out_shape=out_shape,
        scratch_shapes=scratch,
        compiler_params=_params("parallel", "arbitrary"),
        name="fox_proj",
    )(*ops)


def _fox_engine(ins, o_ref, scratch, n_pg, n_sub, dec_seq):
    q_ref, lfn_ref, kn_ref, vn_ref = ins
    k_buf, v_buf, lf_buf, kbuf, vbuf, m_ref, l_ref, acc_ref, run_ref, ncol_ref = scratch
    _, rows, width = q_ref.shape

    def add_head_rows(s, per_head):
        return jnp.concatenate([s[hd * dec_seq:(hd + 1) * dec_seq, :] + per_head[hd:hd + 1, :]
                                for hd in range(FOX_HEADS)], axis=0)

    def begin(sl):
        q = q_ref[sl]
        lfn = lfn_ref[sl]
        n_new = lfn.shape[0]
        row = _iota((n_new, LANES), 0)
        n_cum = jnp.zeros((n_new, LANES), F32)
        for t in range(dec_seq):
            n_cum = n_cum + jnp.where(row >= t, lfn[t:t + 1, :], 0.0)
        n_cum = n_cum * LOG2E
        eye = (_iota((LANES, LANES), 0) == _iota((LANES, LANES), 1)).astype(BF16)
        hi, mid, lo = _split3(n_cum)
        n_cum_t = _dot_nt(eye, hi) + _dot_nt(eye, mid) + _dot_nt(eye, lo)
        lane = _iota((dec_seq, LANES), 1)
        for hd in range(FOX_HEADS):
            col = jnp.sum(jnp.where(lane == hd, n_cum[:dec_seq, :], 0.0), axis=1, keepdims=True)
            ncol_ref[hd * dec_seq:(hd + 1) * dec_seq, :] = jnp.broadcast_to(col, (dec_seq, LANES))
        s = add_head_rows(_dot_nt(q, kn_ref[sl]), -n_cum_t)
        s = jnp.where(_iota((rows, n_new), 1) <= _iota((rows, n_new), 0) % dec_seq, s, -jnp.inf)
        m_new, _, l_new, p_b = _online_softmax(
            s, jnp.full((rows, LANES), -jnp.inf, F32), jnp.zeros((rows, LANES), F32), ncol_ref[...])
        m_ref[...] = m_new
        l_ref[...] = l_new
        acc_ref[...] = _dot(p_b, vn_ref[sl])
        run_ref[...] = jnp.zeros(run_ref.shape, F32)

    sub = n_pg // n_sub
    span = sub * PAGE_SIZE

    def consume(sl, slot):
        q = q_ref[sl]
        later = (_iota((LANES, LANES), 0) > _iota((LANES, LANES), 1)).astype(BF16)
        suffix_and_total = jnp.concatenate([later, jnp.ones((LANES, LANES), BF16)], axis=1)
        lft = lf_buf[slot].reshape(n_pg * FOX_HEADS, PAGE_SIZE)
        both = _dot3(lft, suffix_and_total) * LOG2E
        run = run_ref[...]
        bias_pages = [None] * n_pg
        for j in reversed(range(n_pg)):
            bias_pages[j] = run + both[j * FOX_HEADS:(j + 1) * FOX_HEADS, :LANES]
            run = run + both[j * FOX_HEADS:(j + 1) * FOX_HEADS, LANES:]
        run_ref[...] = run
        ncol = ncol_ref[...]

        def scores(b):
            for j in range(b * sub, (b + 1) * sub):
                kbuf[:, j * PAGE_SIZE:(j + 1) * PAGE_SIZE] = k_buf[slot, j].astype(BF16)
                vbuf[:, j * PAGE_SIZE:(j + 1) * PAGE_SIZE] = v_buf[slot, j].astype(BF16)
            return _dot(q, kbuf[:, b * span:(b + 1) * span])

        def update(b, s):
            bias = jnp.concatenate(bias_pages[b * sub:(b + 1) * sub], axis=1)
            m_new, alpha, l_new, p_b = _online_softmax(add_head_rows(s, bias), m_ref[...], l_ref[...],
                                                       ncol)
            l_ref[...] = l_new
            acc_ref[...] = (jnp.concatenate([alpha] * (width // LANES), axis=1) * acc_ref[...]
                            + _dot_nt(p_b, vbuf[:, b * span:(b + 1) * span]))
            m_ref[...] = m_new

        _pipelined(n_sub, scores, update)

    def finish(sl):
        o_ref[sl] = acc_ref[...] / jnp.sum(l_ref[...], axis=1, keepdims=True)

    return begin, consume, finish


def _fox_scratch(rows, width, n_pg):
    return [pltpu.VMEM((PAGE_SLOTS, n_pg, width, PAGE_SIZE), F32),
            pltpu.VMEM((PAGE_SLOTS, n_pg, width, PAGE_SIZE), F32),
            pltpu.VMEM((PAGE_SLOTS, n_pg, FOX_HEADS, PAGE_SIZE), F32),
            pltpu.VMEM((width, n_pg * PAGE_SIZE), BF16),
            pltpu.VMEM((width, n_pg * PAGE_SIZE), BF16),
            pltpu.VMEM((rows, LANES), F32), pltpu.VMEM((rows, LANES), F32),
            pltpu.VMEM((rows, width), F32),
            pltpu.VMEM((FOX_HEADS, LANES), F32),
            pltpu.VMEM((rows, LANES), F32)]


def _ffn_paged_kernel(pt_ref, *refs, fox, n_pg, n_sub, dec_seq, n_ff, chunk, n_chunks):
    x_ref, sh_ref, sc_ref, gt_ref, g_ref, wgu_ref, wdn_ref = refs[:7]
    n_att = 4
    att_in = refs[7:7 + n_att]
    n_cache = 3 if fox else 2
    caches = refs[7 + n_att:7 + n_att + n_cache]
    out_ref, o_ref, acc_ref, hb_ref, sems = refs[7 + n_att + n_cache:12 + n_att + n_cache]
    scratch = refs[12 + n_att + n_cache:]
    if fox:
        begin, consume, finish = _fox_engine(att_in, o_ref, scratch, n_pg, n_sub, dec_seq)
    else:
        begin, consume, finish = _mla_engine(att_in, o_ref, scratch, n_pg, n_sub)
    start, wait = _page_stream(pt_ref, caches, scratch[:n_cache], sems, n_pg)

    nb, tt, d = x_ref.shape
    m = nb * tt
    d_ff = wdn_ref.shape[0]
    h = _rms(x_ref[...], g_ref[...]) * (1.0 + sc_ref[...]) + sh_ref[...]
    hb_ref[...] = h.reshape(m, d).astype(BF16)

    def ffn_chunk(c):
        lo = c * chunk
        hb = hb_ref[...]
        gate = _dot(hb, wgu_ref[:, lo:lo + chunk])
        up = _dot(hb, wgu_ref[:, d_ff + lo:d_ff + lo + chunk])
        y = _dot((gate * _sigmoid(gate) * up).astype(BF16), wdn_ref[lo:lo + chunk, :])
        if c == 0:
            acc_ref[...] = y
        else:
            acc_ref[...] += y

    seq_per_step = att_in[0].shape[0]
    g = pl.program_id(0) * pl.num_programs(1) + pl.program_id(1)
    n_g = pl.num_programs(0) * pl.num_programs(1)
    order = [(sl, k) for sl in range(seq_per_step) for k in range(n_chunks)]

    def chunk_no(k):
        return n_chunks - 1 - k if fox else k

    n_slots = scratch[0].shape[0]
    ahead = n_slots - 1

    def start_ahead(idx):
        step, pos = divmod(idx, len(order))
        sl2, k2 = order[pos]
        if step == 0:
            start(g * seq_per_step + sl2, chunk_no(k2), idx % n_slots)
        else:
            @pl.when(g + 1 < n_g)
            def _():
                start((g + 1) * seq_per_step + sl2, chunk_no(k2), idx % n_slots)

    @pl.when(g == 0)
    def _():
        for idx in range(ahead):
            start(order[idx][0], chunk_no(order[idx][1]), idx)

    for idx, (sl, k) in enumerate(order):
        slot = idx % n_slots
        start_ahead(idx + ahead)
        wait(g * seq_per_step + sl, chunk_no(k), slot)
        if k == 0:
            begin(sl)
        consume(sl, slot)
        if k == n_chunks - 1:
            finish(sl)
        for c in range(n_ff):
            if c * len(order) // n_ff == idx:
                ffn_chunk(c)

    out_ref[...] = x_ref[...] + (0.5 * gt_ref[...]) * acc_ref[...].reshape(nb, tt, d)


def _ffn_paged(x3, mods3, norm_g, w_gu, w_dn, tt, page_table, att_in, caches, fox, n_pg, n_sub, dec_seq):
    n_b, n_t, d = x3.shape
    d_ff = w_dn.shape[0]
    chunk = 256
    n_steps = n_b * (n_t // tt)
    n_seq, rows, _ = att_in[0].shape
    assert n_seq % n_steps == 0, "sample sequences must split evenly over the prompt tiles"
    seq_per_step = n_seq // n_steps
    n_chunks = page_table.shape[1] // n_pg
    assert (seq_per_step * n_chunks) % PAGE_SLOTS == 0
    n_j = n_t // tt
    x_spec = pl.BlockSpec((1, tt, d), lambda i, j, pt: (i, j, 0))

    def mod_spec(col):
        return pl.BlockSpec((1, 1, d), lambda i, j, pt, col=col: (i, 0, col))

    def const_spec(shape):
        return pl.BlockSpec(shape, lambda i, j, pt: (0,) * len(shape), pipeline_mode=pl.Buffered(1))

    def seq_spec(a):
        return pl.BlockSpec((seq_per_step,) + a.shape[1:], lambda i, j, pt: (i * n_j + j, 0, 0))

    hbm_spec = pl.BlockSpec(memory_space=pl.ANY)
    if fox:
        out_w, out_dt = att_in[0].shape[2], F32
        att_scratch = _fox_scratch(rows, out_w, n_pg)
    else:
        out_w, out_dt = MLA_KV_LORA, BF16
        att_scratch = _mla_scratch(rows, n_pg)
    grid_spec = pltpu.PrefetchScalarGridSpec(
        num_scalar_prefetch=1,
        grid=(n_b, n_j),
        in_specs=[x_spec, mod_spec(0), mod_spec(1), mod_spec(2), const_spec((1, 1, d)),
                  const_spec(w_gu.shape), const_spec(w_dn.shape)]
        + [seq_spec(a) for a in att_in] + [hbm_spec] * len(caches),
        out_specs=[x_spec, pl.BlockSpec((seq_per_step, rows, out_w), lambda i, j, pt: (i * n_j + j, 0, 0))],
        scratch_shapes=[pltpu.VMEM((tt, d), F32), pltpu.VMEM((tt, d), BF16),
                        pltpu.SemaphoreType.DMA((len(caches), PAGE_SLOTS))] + att_scratch,
    )
    return pl.pallas_call(
        functools.partial(_ffn_paged_kernel, fox=fox, n_pg=n_pg, n_sub=n_sub, dec_seq=dec_seq,
                          n_ff=d_ff // chunk, chunk=chunk, n_chunks=n_chunks),
        grid_spec=grid_spec,
        out_shape=[jax.ShapeDtypeStruct(x3.shape, F32),
                   jax.ShapeDtypeStruct((n_seq, rows, out_w), out_dt)],
        compiler_params=_params("arbitrary", "arbitrary"),
        name="ffn_paged",
    )(page_table, x3, mods3, mods3, mods3, norm_g.reshape(1, 1, d), w_gu, w_dn, *att_in, *caches)


def _rope_tables(pos):
    half = MLA_ROPE // 2
    inv_freq = ROPE_THETA ** (-jnp.arange(half, dtype=F32) / half)
    ang = pos.astype(F32)[:, None] * inv_freq[None, :]
    cos, sin, zero = jnp.cos(ang), jnp.sin(ang), jnp.zeros_like(ang)
    return (jnp.concatenate([cos, cos, zero, zero], axis=1),
            jnp.concatenate([-sin, sin, zero, zero], axis=1))


def _mla_weights(w_in, g_q, w_uq, g_kv, w_uk, w_uv, w_o):
    lat = MLA_Q_LORA + MLA_KV_LORA
    kr_cols = np.concatenate([np.arange(lat, lat + MLA_ROPE)] * 2)
    in_cols = np.concatenate([np.arange(lat), kr_cols])
    uq_cols = []
    for hd in range(MLA_HEADS):
        b0 = hd * (MLA_NOPE + MLA_ROPE)
        rope_cols = np.arange(b0 + MLA_NOPE, b0 + MLA_NOPE + MLA_ROPE)
        uq_cols += [np.arange(b0, b0 + MLA_NOPE), rope_cols, rope_cols]
    return dict(
        w_in=w_in[:, in_cols].astype(BF16),
        g_q=g_q.reshape(1, -1), g_kv=g_kv.reshape(1, -1),
        w_uq=w_uq[:, np.concatenate(uq_cols)].astype(BF16),
        w_uk=w_uk.reshape(MLA_KV_LORA, MLA_HEADS * MLA_NOPE).astype(BF16),
        w_uv=w_uv.reshape(MLA_KV_LORA, MLA_HEADS * MLA_V).astype(BF16),
        w_ukt=jnp.transpose(w_uk, (1, 2, 0)).astype(BF16),
        w_uv3=jnp.transpose(w_uv, (1, 0, 2)).astype(BF16),
        w_o=w_o.astype(BF16),
    )


def _pad_rows(a, rows):
    return jnp.pad(a, ((0, 0), (0, rows - a.shape[1])) + ((0, 0),) * (a.ndim - 2))


def _pages_per_step(n_pages, want):
    n = min(want, n_pages)
    while n_pages % n:
        n -= 1
    return n


def kernel(x_prompt, x_sample, c_prompt, c_sample, cache_mla_ckv, cache_mla_krope, cache_fox_k, cache_fox_v, cache_fox_logf, page_table, ada_w, ada_b, norm_g, ffn_w_gu, ffn_w_dn, final_g, mla_w_in, mla_g_q, mla_w_uq, mla_g_kv, mla_w_uk, mla_w_uv, mla_w_o, fox_w_in, fox_b_f, fox_w_o):
    n_p, seq, d = x_prompt.shape
    n_s, dec_seq, _ = x_sample.shape
    n_pages = page_table.shape[1]
    past_len = n_pages * PAGE_SIZE
    tile = min(512, seq)
    tk = min(1024, seq)
    n_pad = LANES

    mods = _adaln(jnp.concatenate([c_prompt, c_sample], axis=0), ada_w, ada_b)
    w_gu = ffn_w_gu.astype(BF16)
    w_dn = ffn_w_dn.astype(BF16)
    xp, xs = x_prompt, x_sample

    mp = mods[0, :n_p].reshape(n_p, 1, -1)
    ms = mods[0, n_p:].reshape(n_s, 1, -1)
    w = _mla_weights(mla_w_in[0], mla_g_q[0], mla_w_uq[0], mla_g_kv[0], mla_w_uk[0], mla_w_uv[0],
                     mla_w_o[0])
    xs = _ffn(xs, ms, 0, norm_g[0, 0], w_gu[0, 0], w_dn[0, 0], n_s, dec_seq)
    cos_s, sin_s = _rope_tables(past_len + jnp.arange(dec_seq, dtype=jnp.int32))
    cos_s, sin_s = jnp.tile(cos_s, (n_s, 1)), jnp.tile(sin_s, (n_s, 1))
    q_lat, q_rope, s_ckv, s_kr = _mla_proj(xs, ms, norm_g[0, 1], w, cos_s, sin_s, n_s, dec_seq,
                                           absorb=True)
    rows = dec_seq * MLA_HEADS
    n_pg = _pages_per_step(n_pages, 16)
    n_sub = 2 if n_pg % 2 == 0 else 1
    xp, o_lat = _ffn_paged(
        xp, mp, norm_g[0, 0], w_gu[0, 0], w_dn[0, 0], tile, page_table,
        [q_lat.reshape(n_s, rows, MLA_KV_LORA), q_rope.reshape(n_s, rows, LANES),
         _pad_rows(s_ckv.reshape(n_s, dec_seq, MLA_KV_LORA), n_pad).astype(BF16),
         _pad_rows(s_kr.reshape(n_s, dec_seq, MLA_ROPE), n_pad).astype(BF16)],
        [cache_mla_ckv[0], jnp.swapaxes(cache_mla_krope[0], 1, 2)],
        False, n_pg, n_sub, dec_seq)
    o_s = _mla_uv(o_lat.reshape(n_s * dec_seq, MLA_HEADS * MLA_KV_LORA), w["w_uv3"])

    cos_p, sin_p = _rope_tables(jnp.arange(seq, dtype=jnp.int32))
    q, k, v, p_ckv, p_kr = _mla_proj(xp, mp, norm_g[0, 1], w, cos_p, sin_p, 1, tile, absorb=False)
    units = [(slice(h * 256, (h + 1) * 256), 0, slice(h * 256, (h + 1) * 256), 1,
              slice(h * 128, (h + 1) * 128)) for h in range(MLA_HEADS)]
    o_p = _flash(q, [k, v], units, [[h] for h in range(MLA_HEADS)], n_p, seq, tile, tk)

    xp = _ffn(xp, mp, 2, norm_g[0, 2], w_gu[0, 1], w_dn[0, 1], 1, tile, mix=(o_p, w["w_o"]))
    xs = _ffn(xs, ms, 2, norm_g[0, 2], w_gu[0, 1], w_dn[0, 1], n_s, dec_seq, mix=(o_s, w["w_o"]))

    mp = mods[1, :n_p].reshape(n_p, 1, -1)
    ms = mods[1, n_p:].reshape(n_s, 1, -1)
    qw = FOX_HEADS * FOX_HEAD_DIM
    kw = FOX_KV_HEADS * FOX_HEAD_DIM
    fw_in = jnp.pad(fox_w_in[0], ((0, 0), (0, LANES - FOX_HEADS))).astype(BF16)
    fb = jnp.pad(fox_b_f[0], (0, LANES - FOX_HEADS)).reshape(1, LANES)
    fw_o = fox_w_o[0].astype(BF16)
    xs = _ffn(xs, ms, 0, norm_g[1, 0], w_gu[1, 0], w_dn[1, 0], n_s, dec_seq)
    q_s, s_k, s_v, s_lf = _fox_proj(xs, ms, norm_g[1, 1], fw_in, fb, n_s, dec_seq, prompt=False)
    q4 = jnp.transpose(q_s.reshape(n_s, dec_seq, FOX_HEADS, FOX_HEAD_DIM), (0, 2, 1, 3))
    grp_of_head = jnp.arange(FOX_HEADS) // (FOX_HEADS // FOX_KV_HEADS)
    onehot = (grp_of_head[:, None] == jnp.arange(FOX_KV_HEADS)[None, :]).astype(BF16)
    q_all = (q4[:, :, :, None, :] * onehot[None, :, None, :, None]).reshape(
        n_s, FOX_HEADS * dec_seq, kw)
    lf_new = jnp.pad(s_lf.reshape(n_s, dec_seq, FOX_HEADS),
                     ((0, 0), (0, n_pad - dec_seq), (0, LANES - FOX_HEADS)))
    cache_lft = jnp.swapaxes(cache_fox_logf[0], 1, 2)
    cache_kt = jnp.transpose(cache_fox_k[0], (0, 2, 3, 1)).reshape(-1, kw, PAGE_SIZE)
    cache_vt = jnp.transpose(cache_fox_v[0], (0, 2, 3, 1)).reshape(-1, kw, PAGE_SIZE)
    xp, o_all = _ffn_paged(
        xp, mp, norm_g[1, 0], w_gu[1, 0], w_dn[1, 0], tile, page_table,
        [q_all, lf_new,
         _pad_rows(s_k.reshape(n_s, dec_seq, kw), n_pad).astype(BF16),
         _pad_rows(s_v.reshape(n_s, dec_seq, kw), n_pad).astype(BF16)],
        [cache_kt, cache_vt, cache_lft], True, n_pg, n_sub, dec_seq)
    o5 = o_all.reshape(n_s, FOX_HEADS, dec_seq, FOX_KV_HEADS, FOX_HEAD_DIM)
    o_sel = jnp.take_along_axis(o5, grp_of_head[None, :, None, None, None], axis=3)[:, :, :, 0, :]
    o_s = jnp.transpose(o_sel, (0, 2, 1, 3)).reshape(n_s * dec_seq, qw).astype(BF16)

    qa, klo, khi, vlo, vhi, p_k, p_v, p_lf = _fox_proj(xp, mp, norm_g[1, 1], fw_in, fb, 1, tile,
                                                       prompt=True)
    units = []
    for h in range(FOX_HEADS):
        i, par, g = h // 2, h % 2, h // (FOX_HEADS // FOX_KV_HEADS)
        units.append((slice(i * 256, (i + 1) * 256), par, slice(g * 256, (g + 1) * 256), 2 + par,
                      slice(g * 128, (g + 1) * 128)))
    o_p = _flash(qa, [klo, khi, vlo, vhi], units, [[2 * i, 2 * i + 1] for i in range(FOX_HEADS // 2)],
                 n_p, seq, tile, tk)

    xp = _ffn(xp, mp, 2, norm_g[1, 2], w_gu[1, 1], w_dn[1, 1], 1, tile, mix=(o_p, fw_o),
              final_g=final_g)
    xs = _ffn(xs, ms, 2, norm_g[1, 2], w_gu[1, 1], w_dn[1, 1], n_s, dec_seq, mix=(o_s, fw_o),
              final_g=final_g)

    def prompt_state(a, *tail):
        return a.reshape(1, n_p, seq, *tail)

    def prompt_kv_state(a_t):
        a4 = a_t.reshape(n_p, FOX_KV_HEADS, FOX_HEAD_DIM, seq)
        return jnp.transpose(a4, (0, 3, 1, 2))[None]

    def sample_state(a, *tail):
        return a.reshape(1, n_s, dec_seq, *tail)

    return (xp, xs,
            prompt_state(p_ckv, MLA_KV_LORA), prompt_state(p_kr, MLA_ROPE),
            prompt_kv_state(p_k), prompt_kv_state(p_v),
            prompt_state(p_lf, FOX_HEADS),
            sample_state(s_ckv, MLA_KV_LORA), sample_state(s_kr, MLA_ROPE),
            sample_state(s_k, FOX_KV_HEADS, FOX_HEAD_DIM), sample_state(s_v, FOX_KV_HEADS, FOX_HEAD_DIM),
            sample_state(s_lf, FOX_HEADS))
```

```python
import functools
import math

import numpy as np
import jax
import jax.numpy as jnp
from jax import lax
from jax.experimental import pallas as pl
from jax.experimental.pallas import tpu as pltpu

F32 = jnp.float32
BF16 = jnp.bfloat16

LANES = 128
NORM_EPS = 1e-6
PAGE_SIZE = 128
D_FF = 2816
MLA_HEADS = 8
MLA_NOPE = 128
MLA_ROPE = 64
MLA_V = 128
MLA_Q_LORA = 512
MLA_KV_LORA = 256
ROPE_THETA = 10000.0
MLA_SCALE = 1.0 / math.sqrt(MLA_NOPE + MLA_ROPE)
FOX_HEADS = 16
FOX_KV_HEADS = 4
FOX_HEAD_DIM = 64
FOX_SCALE = 1.0 / math.sqrt(FOX_HEAD_DIM)
LOG2E = math.log2(math.e)
VMEM_LIMIT_BYTES = 60 * 1024 * 1024
PAGE_SLOTS = 4


def _params(*sem):
    return pltpu.CompilerParams(dimension_semantics=sem, vmem_limit_bytes=VMEM_LIMIT_BYTES)


def _dot(a, b):
    return jnp.dot(a, b, preferred_element_type=F32)


def _dot_nt(a, b):
    return lax.dot_general(a, b, (((1,), (1,)), ((), ())), preferred_element_type=F32)


def _sigmoid(x):
    return 1.0 / (1.0 + jnp.exp(-x))


def _rms(x, g):
    return x * lax.rsqrt(jnp.mean(x * x, axis=-1, keepdims=True) + NORM_EPS) * g


def _split3(x):
    hi = x.astype(BF16)
    r = x - hi.astype(F32)
    mid = r.astype(BF16)
    lo = (r - mid.astype(F32)).astype(BF16)
    return hi, mid, lo


def _dot3(x, m):
    hi, mid, lo = _split3(x)
    return _dot(hi, m) + _dot(mid, m) + _dot(lo, m)


def _iota(shape, dim):
    return lax.broadcasted_iota(jnp.int32, shape, dim)


def _modulated(x_ref, sh_ref, sc_ref, g_ref):
    x = x_ref[...]
    return x, _rms(x, g_ref[...]) * (1.0 + sc_ref[...]) + sh_ref[...]


def _adaln_kernel(c_ref, w_ref, b_ref, o_ref):
    c = c_ref[...]
    a = (c * _sigmoid(c)).astype(BF16)
    o_ref[0] = _dot(a, w_ref[0].astype(BF16)) + b_ref[0]


def _adaln(c_all, ada_w, ada_b):
    n_layers, d, n = ada_w.shape
    rows = c_all.shape[0]
    tn = 1536 if n % 1536 == 0 else n
    return pl.pallas_call(
        _adaln_kernel,
        grid=(n_layers, n // tn),
        in_specs=[
            pl.BlockSpec((rows, d), lambda l, j: (0, 0)),
            pl.BlockSpec((1, d, tn), lambda l, j: (l, 0, j)),
            pl.BlockSpec((1, 1, tn), lambda l, j: (l, 0, j)),
        ],
        out_specs=pl.BlockSpec((1, rows, tn), lambda l, j: (l, 0, j)),
        out_shape=jax.ShapeDtypeStruct((n_layers, rows, n), F32),
        compiler_params=_params("parallel", "parallel"),
        name="adaln",
    )(c_all, ada_w, ada_b.reshape(n_layers, 1, n))


def _ffn_kernel(*refs, has_mix, final, n_chunks, chunk):
    it = iter(refs)
    x_ref, sh_ref, sc_ref, gt_ref, g_ref, wgu_ref, wdn_ref = (next(it) for _ in range(7))
    if has_mix:
        o_ref, wo_ref, gm_ref = next(it), next(it), next(it)
    if final:
        fg_ref = next(it)
    out_ref, acc_ref = next(it), next(it)
    nb, tt, d = x_ref.shape
    m = nb * tt
    x = x_ref[...]
    if has_mix:
        x = x + gm_ref[...] * _dot(o_ref[...], wo_ref[...]).reshape(nb, tt, d)
    h = _rms(x, g_ref[...]) * (1.0 + sc_ref[...]) + sh_ref[...]
    hb = h.reshape(m, d).astype(BF16)
    d_ff = wdn_ref.shape[0]
    for c in range(n_chunks):
        lo = c * chunk
        gate = _dot(hb, wgu_ref[:, lo:lo + chunk])
        up = _dot(hb, wgu_ref[:, d_ff + lo:d_ff + lo + chunk])
        act = (gate * _sigmoid(gate) * up).astype(BF16)
        y = _dot(act, wdn_ref[lo:lo + chunk, :])
        if c == 0:
            acc_ref[...] = y
        else:
            acc_ref[...] += y
    y = x + (0.5 * gt_ref[...]) * acc_ref[...].reshape(nb, tt, d)
    if final:
        y = _rms(y, fg_ref[...])
    out_ref[...] = y


def _ffn(x3, mods3, sub, norm_g, w_gu, w_dn, nb, tt, mix=None, final_g=None):
    n_b, n_t, d = x3.shape
    d_ff = w_dn.shape[0]
    chunk = 256
    grid = (n_b // nb, n_t // tt)
    m = nb * tt
    x_spec = pl.BlockSpec((nb, tt, d), lambda i, j: (i, j, 0))

    def mod_spec(col):
        return pl.BlockSpec((nb, 1, d), lambda i, j, col=col: (i, 0, col))

    def const_spec(shape):
        return pl.BlockSpec(shape, lambda i, j: (0,) * len(shape), pipeline_mode=pl.Buffered(1))

    ops = [x3, mods3, mods3, mods3, norm_g.reshape(1, 1, d), w_gu, w_dn]
    specs = [x_spec, mod_spec(3 * sub), mod_spec(3 * sub + 1), mod_spec(3 * sub + 2),
             const_spec((1, 1, d)), const_spec(w_gu.shape), const_spec(w_dn.shape)]
    if mix is not None:
        o2, w_o = mix
        n_tt = n_t // tt
        ops += [o2, w_o, mods3]
        specs += [pl.BlockSpec((m, o2.shape[1]), lambda i, j: (i * n_tt + j, 0)),
                  const_spec(w_o.shape), mod_spec(3 * 1 + 2)]
    if final_g is not None:
        ops.append(final_g.reshape(1, 1, d))
        specs.append(const_spec((1, 1, d)))
    return pl.pallas_call(
        functools.partial(_ffn_kernel, has_mix=mix is not None, final=final_g is not None,
                          n_chunks=d_ff // chunk, chunk=chunk),
        grid=grid,
        in_specs=specs,
        out_specs=x_spec,
        out_shape=jax.ShapeDtypeStruct(x3.shape, F32),
        scratch_shapes=[pltpu.VMEM((m, d), F32)],
        compiler_params=_params("parallel", "parallel"),
        name="ffn",
    )(*ops)


def _mla_proj_kernel(*refs, absorb):
    it = iter(refs)
    x_ref, sh_ref, sc_ref, g_ref, win_ref, gq_ref, gkv_ref, wuq_ref, cos_ref, sin_ref = (
        next(it) for _ in range(10))
    if absorb:
        wukt_ref = next(it)
        qlat_ref, qrope_ref, ckv_ref, kr_ref = (next(it) for _ in range(4))
    else:
        wuk_ref, wuv_ref = next(it), next(it)
        q_ref, k_ref, v_ref, ckv_ref, kr_ref = (next(it) for _ in range(5))
    nb, tt, d = x_ref.shape
    m = nb * tt
    _, h = _modulated(x_ref, sh_ref, sc_ref, g_ref)
    hb = h.reshape(m, d).astype(BF16)
    proj = _dot(hb, win_ref[...])
    c_q = _rms(proj[:, :MLA_Q_LORA], gq_ref[...])
    c_kv = _rms(proj[:, MLA_Q_LORA:MLA_Q_LORA + MLA_KV_LORA], gkv_ref[...])
    ckv_ref[...] = c_kv
    cos_a = cos_ref[...]
    sin_b = sin_ref[...]

    def rope(z):
        return z * cos_a + pltpu.roll(z, 32, 1) * sin_b

    kr = rope(proj[:, MLA_Q_LORA + MLA_KV_LORA:])
    kr_ref[...] = kr[:, :MLA_ROPE]
    qf = _dot(c_q.astype(BF16), wuq_ref[...]) * (MLA_SCALE * LOG2E)
    ckv_b = c_kv.astype(BF16)
    if not absorb:
        kn = _dot(ckv_b, wuk_ref[...])
        v_ref[...] = _dot(ckv_b, wuv_ref[...]).astype(BF16)
        kr_b = kr.astype(BF16)
    for hd in range(MLA_HEADS):
        b0 = hd * 256
        qn = qf[:, b0:b0 + 128]
        qr = rope(qf[:, b0 + 128:b0 + 256])
        if absorb:
            qlat_ref[:, b0:b0 + 256] = _dot(qn.astype(BF16), wukt_ref[hd]).astype(BF16)
            qrope_ref[:, hd * 128:(hd + 1) * 128] = qr.astype(BF16)
        else:
            q_ref[:, b0:b0 + 128] = qn.astype(BF16)
            q_ref[:, b0 + 128:b0 + 256] = qr.astype(BF16)
            k_ref[:, b0:b0 + 128] = kn[:, hd * 128:(hd + 1) * 128].astype(BF16)
            k_ref[:, b0 + 128:b0 + 256] = kr_b


def _mla_proj(x3, mods3, norm_g, w, cos_a, sin_b, nb, tt, absorb):
    n_b, n_t, d = x3.shape
    m = nb * tt
    n_tt = n_t // tt
    rows = n_b * n_t
    grid = (n_b // nb, n_tt)
    x_spec = pl.BlockSpec((nb, tt, d), lambda i, j: (i, j, 0))

    def mod_spec(col):
        return pl.BlockSpec((nb, 1, d), lambda i, j, col=col: (i, 0, col))

    def const_spec(shape):
        return pl.BlockSpec(shape, lambda i, j: (0,) * len(shape))

    def row_spec(width):
        return pl.BlockSpec((m, width), lambda i, j: (i * n_tt + j, 0))

    tab_spec = pl.BlockSpec((m, LANES), lambda i, j: (j, 0))
    ops = [x3, mods3, mods3, norm_g.reshape(1, 1, d), w["w_in"], w["g_q"], w["g_kv"], w["w_uq"],
           cos_a, sin_b]
    specs = [x_spec, mod_spec(3), mod_spec(4), const_spec((1, 1, d)), const_spec(w["w_in"].shape),
             const_spec(w["g_q"].shape), const_spec(w["g_kv"].shape), const_spec(w["w_uq"].shape),
             tab_spec, tab_spec]
    if absorb:
        ops.append(w["w_ukt"])
        specs.append(const_spec(w["w_ukt"].shape))
        widths = [(MLA_HEADS * 256, BF16), (MLA_HEADS * 128, BF16)]
    else:
        ops += [w["w_uk"], w["w_uv"]]
        specs += [const_spec(w["w_uk"].shape), const_spec(w["w_uv"].shape)]
        widths = [(MLA_HEADS * 256, BF16), (MLA_HEADS * 256, BF16), (MLA_HEADS * MLA_V, BF16)]
    widths += [(MLA_KV_LORA, F32), (MLA_ROPE, F32)]
    return pl.pallas_call(
        functools.partial(_mla_proj_kernel, absorb=absorb),
        grid=grid,
        in_specs=specs,
        out_specs=[row_spec(wd) for wd, _ in widths],
        out_shape=[jax.ShapeDtypeStruct((rows, wd), dt) for wd, dt in widths],
        compiler_params=_params("parallel", "parallel"),
        name="mla_proj",
    )(*ops)


def _online_softmax(s, m_prev, l_prev, row_bias=None):
    blocks = [s[:, b * LANES:(b + 1) * LANES] for b in range(s.shape[1] // LANES)]
    if row_bias is not None:
        blocks = [blk + row_bias for blk in blocks]
    mx = blocks[0]
    for blk in blocks[1:]:
        mx = jnp.maximum(mx, blk)
    m_new = jnp.maximum(m_prev, jnp.max(mx, axis=1, keepdims=True))
    alpha = jnp.exp2(m_prev - m_new)
    ps = [jnp.exp2(blk - m_new) for blk in blocks]
    psum = ps[0]
    for p in ps[1:]:
        psum = psum + p
    p_b = jnp.concatenate([p.astype(BF16) for p in ps], axis=1) if len(ps) > 1 else ps[0].astype(BF16)
    return m_new, alpha, alpha * l_prev + psum, p_b


def _pipelined(n, produce, consume):
    nxt = produce(0)
    for b in range(n):
        cur = nxt
        if b + 1 < n:
            nxt = produce(b + 1)
        consume(b, cur)


def _flash_kernel(*refs, units, outs, n_kv):
    q_ref = refs[0]
    kv = refs[1:1 + n_kv]
    o_ref = refs[1 + n_kv]
    m_ref, l_ref, acc_ref = refs[2 + n_kv:]
    i = pl.program_id(1)
    j = pl.program_id(2)
    tq = q_ref.shape[0]
    tk = kv[0].shape[0]
    j_last = ((i + 1) * tq - 1) // tk

    @pl.when(j == 0)
    def _():
        m_ref[...] = jnp.full(m_ref.shape, -jnp.inf, F32)
        l_ref[...] = jnp.zeros(l_ref.shape, F32)
        acc_ref[...] = jnp.zeros(acc_ref.shape, F32)

    def step(masked, n_keys):
        if masked:
            keep = (i * tq + _iota((tq, n_keys), 0)) >= (j * tk + _iota((tq, n_keys), 1))

        def scores(u):
            qs, ki, ks, _, _ = units[u]
            s = _dot_nt(q_ref[:, qs], kv[ki][:n_keys, ks])
            return jnp.where(keep, s, -jnp.inf) if masked else s

        def update(u, s):
            _, _, _, vi, vs = units[u]
            m_new, alpha, l_new, p_b = _online_softmax(s, m_ref[u], l_ref[u])
            l_ref[u] = l_new
            acc_ref[u] = alpha * acc_ref[u] + _dot(p_b, kv[vi][:n_keys, vs])
            m_ref[u] = m_new

        _pipelined(len(units), scores, update)

    fully_visible = (j + 1) * tk <= i * tq
    partly_visible = jnp.logical_and(jnp.logical_not(fully_visible), j <= j_last)
    half = tk // 2
    first_half_only = j * tk + half > (i + 1) * tq - 1 if half % LANES == 0 else False

    @pl.when(fully_visible)
    def _():
        step(False, tk)

    if half % LANES == 0:
        @pl.when(jnp.logical_and(partly_visible, first_half_only))
        def _():
            step(True, half)

    @pl.when(jnp.logical_and(partly_visible, jnp.logical_not(first_half_only)))
    def _():
        step(True, tk)

    @pl.when(j == j_last)
    def _():
        for b, us in enumerate(outs):
            o = None
            for u in us:
                term = acc_ref[u] / jnp.sum(l_ref[u], axis=1, keepdims=True)
                o = term if o is None else o + term
            o_ref[:, b * LANES:(b + 1) * LANES] = o.astype(o_ref.dtype)


def _flash(q, kvs, units, outs, n_batch, seq, tq, tk):
    nq = seq // tq
    nk = seq // tk
    n_units = len(units)

    def q_map(b, i, j):
        return (b * nq + i, 0)

    def kv_map(b, i, j):
        return (b * nk + jnp.minimum(j, ((i + 1) * tq - 1) // tk), 0)

    out_w = LANES * len(outs)
    return pl.pallas_call(
        functools.partial(_flash_kernel, units=units, outs=outs, n_kv=len(kvs)),
        grid=(n_batch, nq, nk),
        in_specs=[pl.BlockSpec((tq, q.shape[1]), q_map)]
        + [pl.BlockSpec((tk, a.shape[1]), kv_map) for a in kvs],
        out_specs=pl.BlockSpec((tq, out_w), q_map),
        out_shape=jax.ShapeDtypeStruct((q.shape[0], out_w), BF16),
        scratch_shapes=[pltpu.VMEM((n_units, tq, LANES), F32), pltpu.VMEM((n_units, tq, LANES), F32),
                        pltpu.VMEM((n_units, tq, LANES), F32)],
        compiler_params=_params("parallel", "parallel", "arbitrary"),
        name="flash",
    )(q, *kvs)


def _page_stream(pt_ref, caches, bufs, sems, n_pg):
    def copies(s, chunk, slot):
        out = []
        for j in range(n_pg):
            page = pt_ref[s, chunk * n_pg + j]
            for a, (cache, buf) in enumerate(zip(caches, bufs)):
                out.append(pltpu.make_async_copy(cache.at[page], buf.at[slot, j], sems.at[a, slot]))
        return out

    def start(s, chunk, slot):
        for n, cp in enumerate(copies(s, chunk, slot)):
            cp.start(priority=(n // len(caches)) % 2)

    def wait(s, chunk, slot):
        for cp in copies(s, chunk, slot):
            cp.wait()

    return start, wait


def _mla_engine(ins, o_ref, scratch, n_pg, n_sub):
    qlat_ref, qrope_ref, ckn_ref, krn_ref = ins
    ckv_buf, kr_buf, kbuf, krbuf, m_ref, l_ref, acc_ref = scratch
    rows = qlat_ref.shape[1]
    sub = n_pg // n_sub
    width = sub * PAGE_SIZE

    def begin(sl):
        q_lat = qlat_ref[sl]
        ckn = ckn_ref[sl]
        n_new = ckn.shape[0]
        s = _dot_nt(q_lat, ckn) + _dot_nt(qrope_ref[sl][:, :MLA_ROPE], krn_ref[sl])
        s = jnp.where(_iota((rows, n_new), 1) <= _iota((rows, n_new), 0) // MLA_HEADS, s, -jnp.inf)
        m_new, _, l_new, p_b = _online_softmax(
            s, jnp.full((rows, LANES), -jnp.inf, F32), jnp.zeros((rows, LANES), F32))
        m_ref[...] = m_new
        l_ref[...] = l_new
        acc_ref[...] = _dot(p_b, ckn)

    def consume(sl, slot):
        q_lat = qlat_ref[sl]
        q_rope = qrope_ref[sl][:, :MLA_ROPE]

        def scores(b):
            for j in range(b * sub, (b + 1) * sub):
                kbuf[j * PAGE_SIZE:(j + 1) * PAGE_SIZE, :] = ckv_buf[slot, j].astype(BF16)
                krbuf[:, j * PAGE_SIZE:(j + 1) * PAGE_SIZE] = kr_buf[slot, j].astype(BF16)
            return (_dot_nt(q_lat, kbuf[b * width:(b + 1) * width, :])
                    + _dot(q_rope, krbuf[:, b * width:(b + 1) * width]))

        def update(b, s):
            m_new, alpha, l_new, p_b = _online_softmax(s, m_ref[...], l_ref[...])
            l_ref[...] = l_new
            acc_ref[...] = (jnp.concatenate([alpha] * (MLA_KV_LORA // LANES), axis=1) * acc_ref[...]
                            + _dot(p_b, kbuf[b * width:(b + 1) * width, :]))
            m_ref[...] = m_new

        _pipelined(n_sub, scores, update)

    def finish(sl):
        l_tot = jnp.sum(l_ref[...], axis=1, keepdims=True)
        o_ref[sl] = (acc_ref[...] / l_tot).astype(o_ref.dtype)

    return begin, consume, finish


def _mla_scratch(rows, n_pg):
    return [pltpu.VMEM((PAGE_SLOTS, n_pg, PAGE_SIZE, MLA_KV_LORA), F32),
            pltpu.VMEM((PAGE_SLOTS, n_pg, MLA_ROPE, PAGE_SIZE), F32),
            pltpu.VMEM((n_pg * PAGE_SIZE, MLA_KV_LORA), BF16),
            pltpu.VMEM((MLA_ROPE, n_pg * PAGE_SIZE), BF16),
            pltpu.VMEM((rows, LANES), F32), pltpu.VMEM((rows, LANES), F32),
            pltpu.VMEM((rows, MLA_KV_LORA), F32)]


def _mla_uv_kernel(o_ref, wuv_ref, out_ref):
    for hd in range(MLA_HEADS):
        out_ref[:, hd * MLA_V:(hd + 1) * MLA_V] = _dot(
            o_ref[:, hd * MLA_KV_LORA:(hd + 1) * MLA_KV_LORA], wuv_ref[hd]).astype(BF16)


def _mla_uv(o_lat2, w_uv3):
    rows = o_lat2.shape[0]
    return pl.pallas_call(
        _mla_uv_kernel,
        out_shape=jax.ShapeDtypeStruct((rows, MLA_HEADS * MLA_V), BF16),
        compiler_params=pltpu.CompilerParams(vmem_limit_bytes=VMEM_LIMIT_BYTES),
        name="mla_uv",
    )(o_lat2, w_uv3)


def _fox_proj_kernel(*refs, prompt):
    it = iter(refs)
    x_ref, sh_ref, sc_ref, g_ref, win_ref, bf_ref = (next(it) for _ in range(6))
    if prompt:
        sel_ref, selc_ref = next(it), next(it)
        qa_ref, klo_ref, khi_ref, vlo_ref, vhi_ref, k_ref, v_ref, lf_ref, carry_ref = (
            next(it) for _ in range(9))
    else:
        q_ref, k_ref, v_ref, lf_ref = (next(it) for _ in range(4))
    nb, tt, d = x_ref.shape
    m = nb * tt
    qw = FOX_HEADS * FOX_HEAD_DIM
    kw = FOX_KV_HEADS * FOX_HEAD_DIM
    _, h = _modulated(x_ref, sh_ref, sc_ref, g_ref)
    hb = h.reshape(m, d).astype(BF16)
    proj = _dot(hb, win_ref[...])
    q = proj[:, :qw] * (FOX_SCALE * LOG2E)
    k = proj[:, qw:qw + kw]
    v = proj[:, qw + kw:qw + 2 * kw]
    z = proj[:, qw + 2 * kw:] + bf_ref[...]
    logf = jnp.minimum(z, 0.0) - jnp.log(1.0 + jnp.exp(-jnp.abs(z)))
    if prompt:
        k_ref[...] = k.T
        v_ref[...] = v.T
    else:
        k_ref[...] = k
        v_ref[...] = v
    lf_ref[...] = logf[:, :FOX_HEADS]
    if not prompt:
        q_ref[...] = q.astype(BF16)
        return

    @pl.when(pl.program_id(1) == 0)
    def _():
        carry_ref[...] = jnp.zeros(carry_ref.shape, F32)

    lane = _iota((m, LANES), 1)
    logf = jnp.where(lane < FOX_HEADS, logf, 0.0)
    tri = (_iota((m, m), 0) >= _iota((m, m), 1)).astype(BF16)
    f_hi, f_mid, f_lo = _split3(logf)
    cum = _dot(tri, f_hi) + _dot(tri, f_mid) + _dot(tri, f_lo) + carry_ref[...]
    carry_ref[...] = cum[m - 1:m, :]
    hi, mid, lo = _split3(cum * LOG2E)
    comb = (hi.astype(F32) + pltpu.roll(mid.astype(F32), FOX_HEADS, 1)
            + pltpu.roll(lo.astype(F32), 2 * FOX_HEADS, 1)).astype(BF16)
    aug = _dot(comb, sel_ref[...]) + selc_ref[...]
    n_pairs = FOX_HEADS // 2
    for i in range(n_pairs):
        qa_ref[:, i * 256:i * 256 + 128] = q[:, i * 128:(i + 1) * 128].astype(BF16)
        qa_ref[:, i * 256 + 128:(i + 1) * 256] = aug[:, i * 128:(i + 1) * 128].astype(BF16)
    low_half = lane < FOX_HEAD_DIM
    a0 = n_pairs * 128
    for grp in range(FOX_KV_HEADS):
        blk = (grp // 2) * 128
        for src, lo_ref, hi_ref, stride, has_aug in ((k, klo_ref, khi_ref, 256, True),
                                                     (v, vlo_ref, vhi_ref, 128, False)):
            same = src[:, blk:blk + 128]
            swapped = pltpu.roll(same, FOX_HEAD_DIM, 1)
            lo_src, hi_src = (same, swapped) if grp % 2 == 0 else (swapped, same)
            b0 = grp * stride
            lo_ref[:, b0:b0 + 128] = jnp.where(low_half, lo_src, 0.0).astype(BF16)
            hi_ref[:, b0:b0 + 128] = jnp.where(low_half, 0.0, hi_src).astype(BF16)
            if has_aug:
                lo_ref[:, b0 + 128:b0 + 256] = aug[:, a0 + grp * 128:a0 + (grp + 1) * 128].astype(BF16)
                hi_ref[:, b0 + 128:b0 + 256] = aug[:, a0 + 512 + grp * 128:
                                                   a0 + 512 + (grp + 1) * 128].astype(BF16)


def _fox_aug_selectors():
    n_pairs = FOX_HEADS // 2
    sel = np.zeros((LANES, (n_pairs + 2 * FOX_KV_HEADS) * LANES), np.float32)
    const = np.zeros((1, sel.shape[1]), np.float32)
    klo0 = n_pairs * LANES
    khi0 = klo0 + FOX_KV_HEADS * LANES
    for part in range(3):
        src = part * FOX_HEADS
        for i in range(n_pairs):
            sel[src + 2 * i, i * LANES + part] = 1.0
            sel[src + 2 * i + 1, i * LANES + 3 + part] = 1.0
            const[0, i * LANES + 6 + 3 * (i % 2) + part] = 1.0
        for g in range(FOX_KV_HEADS):
            const[0, klo0 + g * LANES + part] = 1.0
            const[0, khi0 + g * LANES + 3 + part] = 1.0
            sel[src + 4 * g, klo0 + g * LANES + 6 + part] = -1.0
            sel[src + 4 * g + 2, klo0 + g * LANES + 9 + part] = -1.0
            sel[src + 4 * g + 1, khi0 + g * LANES + 6 + part] = -1.0
            sel[src + 4 * g + 3, khi0 + g * LANES + 9 + part] = -1.0
    return jnp.asarray(sel, BF16), jnp.asarray(const, F32)


def _fox_proj(x3, mods3, norm_g, w_in, b_f, nb, tt, prompt):
    n_b, n_t, d = x3.shape
    m = nb * tt
    n_tt = n_t // tt
    rows = n_b * n_t
    x_spec = pl.BlockSpec((nb, tt, d), lambda i, j: (i, j, 0))

    def mod_spec(col):
        return pl.BlockSpec((nb, 1, d), lambda i, j, col=col: (i, 0, col))

    def const_spec(shape):
        return pl.BlockSpec(shape, lambda i, j: (0,) * len(shape))

    def row_spec(width):
        return pl.BlockSpec((m, width), lambda i, j: (i * n_tt + j, 0))

    kw = FOX_KV_HEADS * FOX_HEAD_DIM
    ops = [x3, mods3, mods3, norm_g.reshape(1, 1, d), w_in, b_f]
    specs = [x_spec, mod_spec(3), mod_spec(4), const_spec((1, 1, d)), const_spec(w_in.shape),
             const_spec(b_f.shape)]
    scratch = []
    if prompt:
        sel, selc = _fox_aug_selectors()
        ops += [sel, selc]
        specs += [const_spec(sel.shape), const_spec(selc.shape)]
        widths = [(FOX_HEADS // 2 * 256, BF16), (FOX_KV_HEADS * 256, BF16), (FOX_KV_HEADS * 256, BF16),
                  (FOX_KV_HEADS * 128, BF16), (FOX_KV_HEADS * 128, BF16)]
        scratch = [pltpu.VMEM((1, LANES), F32)]
    else:
        widths = [(FOX_HEADS * FOX_HEAD_DIM, BF16)]
    out_specs = [row_spec(wd) for wd, _ in widths]
    out_shape = [jax.ShapeDtypeStruct((rows, wd), dt) for wd, dt in widths]
    if prompt:
        out_specs += [pl.BlockSpec((None, kw, tt), lambda i, j: (i, 0, j))] * 2
        out_shape += [jax.ShapeDtypeStruct((n_b, kw, n_t), F32)] * 2
    else:
        out_specs += [row_spec(kw)] * 2
        out_shape += [jax.ShapeDtypeStruct((rows, kw), F32)] * 2
    out_specs.append(row_spec(FOX_HEADS))
    out_shape.append(jax.ShapeDtypeStruct((rows, FOX_HEADS), F32))
    return pl.pallas_call(
        functools.partial(_fox_proj_kernel, prompt=prompt),
        grid=(n_b // nb, n_tt),
        in_specs=specs,
        out_specs=out_specs,
        out_shape=out_shape,
        scratch_shapes=scratch,
        compiler_params=_params("parallel", "arbitrary"),
        name="fox_proj",
    )(*ops)


def _fox_engine(ins, o_ref, scratch, n_pg, n_sub, dec_seq):
    q_ref, lfn_ref, kn_ref, vn_ref = ins
    k_buf, v_buf, lf_buf, kbuf, vbuf, m_ref, l_ref, acc_ref, run_ref, ncol_ref = scratch
    _, rows, width = q_ref.shape

    def add_head_rows(s, per_head):
        return jnp.concatenate([s[hd * dec_seq:(hd + 1) * dec_seq, :] + per_head[hd:hd + 1, :]
                                for hd in range(FOX_HEADS)], axis=0)

    def begin(sl):
        q = q_ref[sl]
        lfn = lfn_ref[sl]
        n_new = lfn.shape[0]
        row = _iota((n_new, LANES), 0)
        n_cum = jnp.zeros((n_new, LANES), F32)
        for t in range(dec_seq):
            n_cum = n_cum + jnp.where(row >= t, lfn[t:t + 1, :], 0.0)
        n_cum = n_cum * LOG2E
        eye = (_iota((LANES, LANES), 0) == _iota((LANES, LANES), 1)).astype(BF16)
        hi, mid, lo = _split3(n_cum)
        n_cum_t = _dot_nt(eye, hi) + _dot_nt(eye, mid) + _dot_nt(eye, lo)
        lane = _iota((dec_seq, LANES), 1)
        for hd in range(FOX_HEADS):
            col = jnp.sum(jnp.where(lane == hd, n_cum[:dec_seq, :], 0.0), axis=1, keepdims=True)
            ncol_ref[hd * dec_seq:(hd + 1) * dec_seq, :] = jnp.broadcast_to(col, (dec_seq, LANES))
        s = add_head_rows(_dot_nt(q, kn_ref[sl]), -n_cum_t)
        s = jnp.where(_iota((rows, n_new), 1) <= _iota((rows, n_new), 0) % dec_seq, s, -jnp.inf)
        m_new, _, l_new, p_b = _online_softmax(
            s, jnp.full((rows, LANES), -jnp.inf, F32), jnp.zeros((rows, LANES), F32), ncol_ref[...])
        m_ref[...] = m_new
        l_ref[...] = l_new
        acc_ref[...] = _dot(p_b, vn_ref[sl])
        run_ref[...] = jnp.zeros(run_ref.shape, F32)

    sub = n_pg // n_sub
    span = sub * PAGE_SIZE

    def consume(sl, slot):
        q = q_ref[sl]
        later = (_iota((LANES, LANES), 0) > _iota((LANES, LANES), 1)).astype(BF16)
        suffix_and_total = jnp.concatenate([later, jnp.ones((LANES, LANES), BF16)], axis=1)
        lft = lf_buf[slot].reshape(n_pg * FOX_HEADS, PAGE_SIZE)
        both = _dot3(lft, suffix_and_total) * LOG2E
        run = run_ref[...]
        bias_pages = [None] * n_pg
        for j in reversed(range(n_pg)):
            bias_pages[j] = run + both[j * FOX_HEADS:(j + 1) * FOX_HEADS, :LANES]
            run = run + both[j * FOX_HEADS:(j + 1) * FOX_HEADS, LANES:]
        run_ref[...] = run
        ncol = ncol_ref[...]

        def scores(b):
            for j in range(b * sub, (b + 1) * sub):
                kbuf[:, j * PAGE_SIZE:(j + 1) * PAGE_SIZE] = k_buf[slot, j].astype(BF16)
                vbuf[:, j * PAGE_SIZE:(j + 1) * PAGE_SIZE] = v_buf[slot, j].astype(BF16)
            return _dot(q, kbuf[:, b * span:(b + 1) * span])

        def update(b, s):
            bias = jnp.concatenate(bias_pages[b * sub:(b + 1) * sub], axis=1)
            m_new, alpha, l_new, p_b = _online_softmax(add_head_rows(s, bias), m_ref[...], l_ref[...],
                                                       ncol)
            l_ref[...] = l_new
            acc_ref[...] = (jnp.concatenate([alpha] * (width // LANES), axis=1) * acc_ref[...]
                            + _dot_nt(p_b, vbuf[:, b * span:(b + 1) * span]))
            m_ref[...] = m_new

        _pipelined(n_sub, scores, update)

    def finish(sl):
        o_ref[sl] = acc_ref[...] / jnp.sum(l_ref[...], axis=1, keepdims=True)

    return begin, consume, finish


def _fox_scratch(rows, width, n_pg):
    return [pltpu.VMEM((PAGE_SLOTS, n_pg, width, PAGE_SIZE), F32),
            pltpu.VMEM((PAGE_SLOTS, n_pg, width, PAGE_SIZE), F32),
            pltpu.VMEM((PAGE_SLOTS, n_pg, FOX_HEADS, PAGE_SIZE), F32),
            pltpu.VMEM((width, n_pg * PAGE_SIZE), BF16),
            pltpu.VMEM((width, n_pg * PAGE_SIZE), BF16),
            pltpu.VMEM((rows, LANES), F32), pltpu.VMEM((rows, LANES), F32),
            pltpu.VMEM((rows, width), F32),
            pltpu.VMEM((FOX_HEADS, LANES), F32),
            pltpu.VMEM((rows, LANES), F32)]


def _ffn_paged_kernel(pt_ref, *refs, fox, has_mix, final, n_pg, n_sub, dec_seq, n_ff, chunk, n_chunks):
    it = iter(refs)
    x_ref, sh_ref, sc_ref, gt_ref, g_ref, wgu_ref, wdn_ref = (next(it) for _ in range(7))
    if has_mix:
        oin_ref, wo_ref, gm_ref = next(it), next(it), next(it)
    if final:
        fg_ref = next(it)
    att_in = [next(it) for _ in range(4)]
    n_cache = 3 if fox else 2
    caches = [next(it) for _ in range(n_cache)]
    out_ref, o_ref, acc_ref, hb_ref, xs_ref, sems = (next(it) for _ in range(6))
    scratch = list(it)
    if fox:
        begin, consume, finish = _fox_engine(att_in, o_ref, scratch, n_pg, n_sub, dec_seq)
    else:
        begin, consume, finish = _mla_engine(att_in, o_ref, scratch, n_pg, n_sub)
    start, wait = _page_stream(pt_ref, caches, scratch[:n_cache], sems, n_pg)

    nb, tt, d = x_ref.shape
    m = nb * tt
    d_ff = wdn_ref.shape[0]
    x = x_ref[...]
    if has_mix:
        x = x + gm_ref[...] * _dot(oin_ref[...], wo_ref[...]).reshape(nb, tt, d)
    xs_ref[...] = x.reshape(m, d)
    h = _rms(x, g_ref[...]) * (1.0 + sc_ref[...]) + sh_ref[...]
    hb_ref[...] = h.reshape(m, d).astype(BF16)

    def ffn_chunk(c):
        lo = c * chunk
        hb = hb_ref[...]
        gate = _dot(hb, wgu_ref[:, lo:lo + chunk])
        up = _dot(hb, wgu_ref[:, d_ff + lo:d_ff + lo + chunk])
        y = _dot((gate * _sigmoid(gate) * up).astype(BF16), wdn_ref[lo:lo + chunk, :])
        if c == 0:
            acc_ref[...] = y
        else:
            acc_ref[...] += y

    seq_per_step = att_in[0].shape[0]
    g = pl.program_id(0) * pl.num_programs(1) + pl.program_id(1)
    n_g = pl.num_programs(0) * pl.num_programs(1)
    order = [(sl, k) for sl in range(seq_per_step) for k in range(n_chunks)]

    def chunk_no(k):
        return n_chunks - 1 - k if fox else k

    n_slots = scratch[0].shape[0]
    ahead = n_slots - 1

    def start_ahead(idx):
        step, pos = divmod(idx, len(order))
        sl2, k2 = order[pos]
        if step == 0:
            start(g * seq_per_step + sl2, chunk_no(k2), idx % n_slots)
        else:
            @pl.when(g + 1 < n_g)
            def _():
                start((g + 1) * seq_per_step + sl2, chunk_no(k2), idx % n_slots)

    @pl.when(g == 0)
    def _():
        for idx in range(ahead):
            start(order[idx][0], chunk_no(order[idx][1]), idx)

    for idx, (sl, k) in enumerate(order):
        slot = idx % n_slots
        start_ahead(idx + ahead)
        wait(g * seq_per_step + sl, chunk_no(k), slot)
        if k == 0:
            begin(sl)
        consume(sl, slot)
        if k == n_chunks - 1:
            finish(sl)
        for c in range(n_ff):
            if c * len(order) // n_ff == idx:
                ffn_chunk(c)

    y = xs_ref[...].reshape(nb, tt, d) + (0.5 * gt_ref[...]) * acc_ref[...].reshape(nb, tt, d)
    if final:
        y = _rms(y, fg_ref[...])
    out_ref[...] = y


def _ffn_paged(x3, mods3, sub, norm_g, w_gu, w_dn, tt, page_table, att_in, caches, fox, n_pg, n_sub, dec_seq,
               mix=None, final_g=None):
    n_b, n_t, d = x3.shape
    d_ff = w_dn.shape[0]
    chunk = 256
    n_steps = n_b * (n_t // tt)
    n_seq, rows, _ = att_in[0].shape
    assert n_seq % n_steps == 0, "sample sequences must split evenly over the prompt tiles"
    seq_per_step = n_seq // n_steps
    n_chunks = page_table.shape[1] // n_pg
    assert (seq_per_step * n_chunks) % PAGE_SLOTS == 0
    n_j = n_t // tt
    x_spec = pl.BlockSpec((1, tt, d), lambda i, j, pt: (i, j, 0))

    def mod_spec(col):
        return pl.BlockSpec((1, 1, d), lambda i, j, pt, col=col: (i, 0, col))

    def const_spec(shape):
        return pl.BlockSpec(shape, lambda i, j, pt: (0,) * len(shape), pipeline_mode=pl.Buffered(1))

    def seq_spec(a):
        return pl.BlockSpec((seq_per_step,) + a.shape[1:], lambda i, j, pt: (i * n_j + j, 0, 0))

    hbm_spec = pl.BlockSpec(memory_space=pl.ANY)
    if fox:
        out_w, out_dt = att_in[0].shape[2], F32
        att_scratch = _fox_scratch(rows, out_w, n_pg)
    else:
        out_w, out_dt = MLA_KV_LORA, BF16
        att_scratch = _mla_scratch(rows, n_pg)
    ops = [x3, mods3, mods3, mods3, norm_g.reshape(1, 1, d), w_gu, w_dn]
    specs = [x_spec, mod_spec(3 * sub), mod_spec(3 * sub + 1), mod_spec(3 * sub + 2),
             const_spec((1, 1, d)), const_spec(w_gu.shape), const_spec(w_dn.shape)]
    if mix is not None:
        o2, w_o = mix
        ops += [o2, w_o, mods3]
        specs += [pl.BlockSpec((tt, o2.shape[1]), lambda i, j, pt: (i * n_j + j, 0)),
                  const_spec(w_o.shape), mod_spec(3 * 1 + 2)]
    if final_g is not None:
        ops.append(final_g.reshape(1, 1, d))
        specs.append(const_spec((1, 1, d)))
    grid_spec = pltpu.PrefetchScalarGridSpec(
        num_scalar_prefetch=1,
        grid=(n_b, n_j),
        in_specs=specs + [seq_spec(a) for a in att_in] + [hbm_spec] * len(caches),
        out_specs=[x_spec, pl.BlockSpec((seq_per_step, rows, out_w), lambda i, j, pt: (i * n_j + j, 0, 0))],
        scratch_shapes=[pltpu.VMEM((tt, d), F32), pltpu.VMEM((tt, d), BF16), pltpu.VMEM((tt, d), F32),
                        pltpu.SemaphoreType.DMA((len(caches), PAGE_SLOTS))] + att_scratch,
    )
    return pl.pallas_call(
        functools.partial(_ffn_paged_kernel, fox=fox, has_mix=mix is not None, final=final_g is not None,
                          n_pg=n_pg, n_sub=n_sub, dec_seq=dec_seq,
                          n_ff=d_ff // chunk, chunk=chunk, n_chunks=n_chunks),
        grid_spec=grid_spec,
        out_shape=[jax.ShapeDtypeStruct(x3.shape, F32),
                   jax.ShapeDtypeStruct((n_seq, rows, out_w), out_dt)],
        compiler_params=_params("arbitrary", "arbitrary"),
        name="ffn_paged",
    )(page_table, *ops, *att_in, *caches)


def _rope_tables(pos):
    half = MLA_ROPE // 2
    inv_freq = ROPE_THETA ** (-jnp.arange(half, dtype=F32) / half)
    ang = pos.astype(F32)[:, None] * inv_freq[None, :]
    cos, sin, zero = jnp.cos(ang), jnp.sin(ang), jnp.zeros_like(ang)
    return (jnp.concatenate([cos, cos, zero, zero], axis=1),
            jnp.concatenate([-sin, sin, zero, zero], axis=1))


def _mla_weights(w_in, g_q, w_uq, g_kv, w_uk, w_uv, w_o):
    lat = MLA_Q_LORA + MLA_KV_LORA
    kr_cols = np.concatenate([np.arange(lat, lat + MLA_ROPE)] * 2)
    in_cols = np.concatenate([np.arange(lat), kr_cols])
    uq_cols = []
    for hd in range(MLA_HEADS):
        b0 = hd * (MLA_NOPE + MLA_ROPE)
        rope_cols = np.arange(b0 + MLA_NOPE, b0 + MLA_NOPE + MLA_ROPE)
        uq_cols += [np.arange(b0, b0 + MLA_NOPE), rope_cols, rope_cols]
    return dict(
        w_in=w_in[:, in_cols].astype(BF16),
        g_q=g_q.reshape(1, -1), g_kv=g_kv.reshape(1, -1),
        w_uq=w_uq[:, np.concatenate(uq_cols)].astype(BF16),
        w_uk=w_uk.reshape(MLA_KV_LORA, MLA_HEADS * MLA_NOPE).astype(BF16),
        w_uv=w_uv.reshape(MLA_KV_LORA, MLA_HEADS * MLA_V).astype(BF16),
        w_ukt=jnp.transpose(w_uk, (1, 2, 0)).astype(BF16),
        w_uv3=jnp.transpose(w_uv, (1, 0, 2)).astype(BF16),
        w_o=w_o.astype(BF16),
    )


def _pad_rows(a, rows):
    return jnp.pad(a, ((0, 0), (0, rows - a.shape[1])) + ((0, 0),) * (a.ndim - 2))


def _pages_per_step(n_pages, want):
    n = min(want, n_pages)
    while n_pages % n:
        n -= 1
    return n


def kernel(x_prompt, x_sample, c_prompt, c_sample, cache_mla_ckv, cache_mla_krope, cache_fox_k, cache_fox_v, cache_fox_logf, page_table, ada_w, ada_b, norm_g, ffn_w_gu, ffn_w_dn, final_g, mla_w_in, mla_g_q, mla_w_uq, mla_g_kv, mla_w_uk, mla_w_uv, mla_w_o, fox_w_in, fox_b_f, fox_w_o):
    n_p, seq, d = x_prompt.shape
    n_s, dec_seq, _ = x_sample.shape
    n_pages = page_table.shape[1]
    past_len = n_pages * PAGE_SIZE
    tile = min(512, seq)
    tk = min(1024, seq)
    n_pad = LANES

    mods = _adaln(jnp.concatenate([c_prompt, c_sample], axis=0), ada_w, ada_b)
    w_gu = ffn_w_gu.astype(BF16)
    w_dn = ffn_w_dn.astype(BF16)
    xp, xs = x_prompt, x_sample

    mp = mods[0, :n_p].reshape(n_p, 1, -1)
    ms = mods[0, n_p:].reshape(n_s, 1, -1)
    w = _mla_weights(mla_w_in[0], mla_g_q[0], mla_w_uq[0], mla_g_kv[0], mla_w_uk[0], mla_w_uv[0],
                     mla_w_o[0])
    xs = _ffn(xs, ms, 0, norm_g[0, 0], w_gu[0, 0], w_dn[0, 0], n_s, dec_seq)
    cos_s, sin_s = _rope_tables(past_len + jnp.arange(dec_seq, dtype=jnp.int32))
    cos_s, sin_s = jnp.tile(cos_s, (n_s, 1)), jnp.tile(sin_s, (n_s, 1))
    q_lat, q_rope, s_ckv, s_kr = _mla_proj(xs, ms, norm_g[0, 1], w, cos_s, sin_s, n_s, dec_seq,
                                           absorb=True)
    rows = dec_seq * MLA_HEADS
    n_pg = _pages_per_step(n_pages, 8)
    n_sub = 1
    half = n_s // 2
    mla_att = [q_lat.reshape(n_s, rows, MLA_KV_LORA), q_rope.reshape(n_s, rows, LANES),
               _pad_rows(s_ckv.reshape(n_s, dec_seq, MLA_KV_LORA), n_pad).astype(BF16),
               _pad_rows(s_kr.reshape(n_s, dec_seq, MLA_ROPE), n_pad).astype(BF16)]
    mla_caches = [cache_mla_ckv[0], jnp.swapaxes(cache_mla_krope[0], 1, 2)]
    xp, o_lat_a = _ffn_paged(
        xp, mp, 0, norm_g[0, 0], w_gu[0, 0], w_dn[0, 0], tile, page_table[:half],
        [a[:half] for a in mla_att], mla_caches, False, n_pg, n_sub, dec_seq)

    cos_p, sin_p = _rope_tables(jnp.arange(seq, dtype=jnp.int32))
    q, k, v, p_ckv, p_kr = _mla_proj(xp, mp, norm_g[0, 1], w, cos_p, sin_p, 1, tile, absorb=False)
    units = [(slice(h * 256, (h + 1) * 256), 0, slice(h * 256, (h + 1) * 256), 1,
              slice(h * 128, (h + 1) * 128)) for h in range(MLA_HEADS)]
    o_p = _flash(q, [k, v], units, [[h] for h in range(MLA_HEADS)], n_p, seq, tile, tk)

    xp, o_lat_b = _ffn_paged(
        xp, mp, 2, norm_g[0, 2], w_gu[0, 1], w_dn[0, 1], tile, page_table[half:],
        [a[half:] for a in mla_att], mla_caches, False, n_pg, n_sub, dec_seq, mix=(o_p, w["w_o"]))
    o_lat = jnp.concatenate([o_lat_a, o_lat_b], axis=0)
    o_s = _mla_uv(o_lat.reshape(n_s * dec_seq, MLA_HEADS * MLA_KV_LORA), w["w_uv3"])
    xs = _ffn(xs, ms, 2, norm_g[0, 2], w_gu[0, 1], w_dn[0, 1], n_s, dec_seq, mix=(o_s, w["w_o"]))

    mp = mods[1, :n_p].reshape(n_p, 1, -1)
    ms = mods[1, n_p:].reshape(n_s, 1, -1)
    qw = FOX_HEADS * FOX_HEAD_DIM
    kw = FOX_KV_HEADS * FOX_HEAD_DIM
    fw_in = jnp.pad(fox_w_in[0], ((0, 0), (0, LANES - FOX_HEADS))).astype(BF16)
    fb = jnp.pad(fox_b_f[0], (0, LANES - FOX_HEADS)).reshape(1, LANES)
    fw_o = fox_w_o[0].astype(BF16)
    xs = _ffn(xs, ms, 0, norm_g[1, 0], w_gu[1, 0], w_dn[1, 0], n_s, dec_seq)
    q_s, s_k, s_v, s_lf = _fox_proj(xs, ms, norm_g[1, 1], fw_in, fb, n_s, dec_seq, prompt=False)
    q4 = jnp.transpose(q_s.reshape(n_s, dec_seq, FOX_HEADS, FOX_HEAD_DIM), (0, 2, 1, 3))
    grp_of_head = jnp.arange(FOX_HEADS) // (FOX_HEADS // FOX_KV_HEADS)
    onehot = (grp_of_head[:, None] == jnp.arange(FOX_KV_HEADS)[None, :]).astype(BF16)
    q_all = (q4[:, :, :, None, :] * onehot[None, :, None, :, None]).reshape(
        n_s, FOX_HEADS * dec_seq, kw)
    lf_new = jnp.pad(s_lf.reshape(n_s, dec_seq, FOX_HEADS),
                     ((0, 0), (0, n_pad - dec_seq), (0, LANES - FOX_HEADS)))
    cache_lft = jnp.swapaxes(cache_fox_logf[0], 1, 2)
    cache_kt = jnp.transpose(cache_fox_k[0], (0, 2, 3, 1)).reshape(-1, kw, PAGE_SIZE)
    cache_vt = jnp.transpose(cache_fox_v[0], (0, 2, 3, 1)).reshape(-1, kw, PAGE_SIZE)
    fox_att = [q_all, lf_new,
               _pad_rows(s_k.reshape(n_s, dec_seq, kw), n_pad).astype(BF16),
               _pad_rows(s_v.reshape(n_s, dec_seq, kw), n_pad).astype(BF16)]
    fox_caches = [cache_kt, cache_vt, cache_lft]
    xp, o_all_a = _ffn_paged(
        xp, mp, 0, norm_g[1, 0], w_gu[1, 0], w_dn[1, 0], tile, page_table[:half],
        [a[:half] for a in fox_att], fox_caches, True, n_pg, n_sub, dec_seq)

    qa, klo, khi, vlo, vhi, p_k, p_v, p_lf = _fox_proj(xp, mp, norm_g[1, 1], fw_in, fb, 1, tile,
                                                       prompt=True)
    units = []
    for h in range(FOX_HEADS):
        i, par, g = h // 2, h % 2, h // (FOX_HEADS // FOX_KV_HEADS)
        units.append((slice(i * 256, (i + 1) * 256), par, slice(g * 256, (g + 1) * 256), 2 + par,
                      slice(g * 128, (g + 1) * 128)))
    o_p = _flash(qa, [klo, khi, vlo, vhi], units, [[2 * i, 2 * i + 1] for i in range(FOX_HEADS // 2)],
                 n_p, seq, tile, tk)

    xp, o_all_b = _ffn_paged(
        xp, mp, 2, norm_g[1, 2], w_gu[1, 1], w_dn[1, 1], tile, page_table[half:],
        [a[half:] for a in fox_att], fox_caches, True, n_pg, n_sub, dec_seq, mix=(o_p, fw_o),
        final_g=final_g)
    o5 = jnp.concatenate([o_all_a, o_all_b], axis=0).reshape(
        n_s, FOX_HEADS, dec_seq, FOX_KV_HEADS, FOX_HEAD_DIM)
    o_sel = jnp.take_along_axis(o5, grp_of_head[None, :, None, None, None], axis=3)[:, :, :, 0, :]
    o_s = jnp.transpose(o_sel, (0, 2, 1, 3)).reshape(n_s * dec_seq, qw).astype(BF16)
    xs = _ffn(xs, ms, 2, norm_g[1, 2], w_gu[1, 1], w_dn[1, 1], n_s, dec_seq, mix=(o_s, fw_o),
              final_g=final_g)

    def prompt_state(a, *tail):
        return a.reshape(1, n_p, seq, *tail)

    def prompt_kv_state(a_t):
        a4 = a_t.reshape(n_p, FOX_KV_HEADS, FOX_HEAD_DIM, seq)
        return jnp.transpose(a4, (0, 3, 1, 2))[None]

    def sample_state(a, *tail):
        return a.reshape(1, n_s, dec_seq, *tail)

    return (xp, xs,
            prompt_state(p_ckv, MLA_KV_LORA), prompt_state(p_kr, MLA_ROPE),
            prompt_kv_state(p_k), prompt_kv_state(p_v),
            prompt_state(p_lf, FOX_HEADS),
            sample_state(s_ckv, MLA_KV_LORA), sample_state(s_kr, MLA_ROPE),
            sample_state(s_k, FOX_KV_HEADS, FOX_HEAD_DIM), sample_state(s_v, FOX_KV_HEADS, FOX_HEAD_DIM),
            sample_state(s_lf, FOX_HEADS))
```
